```python
import jax, jax.numpy as jnp
from jax import lax
import numpy as np

D_MODEL = 1024
BATCH = 32
SEQ = 256
DEPTH = 2
DEC_BATCH = 8
DEC_SEQ = 2048
PAST_LEN = 256

GRID_W = 64
MIX_W = D_MODEL
DN_HEADS = 4
DK = 128
DV = 128
DN_W = DN_HEADS * DV
POOL_GROUPS = 4
POOL_W = MIX_W - DN_W
POOL_GC = POOL_W // POOL_GROUPS
POOL_WINDOWS = (2, 4, 8, 16)
CONV_K = 5
CHUNK = 64
D_FF = -(-(8 * D_MODEL) // (3 * 256)) * 256
IN_COLS = 3 * DN_W + DN_W + 4 * DN_HEADS + POOL_W
EPS = 1e-6

kernel_name = "hymba_gdn_pool_diffusion_step"


def rmsnorm(x, gain):
    xf = x.astype(jnp.float32)
    y = xf * lax.rsqrt(jnp.mean(xf * xf, axis=-1, keepdims=True) + EPS)
    return (y * gain.astype(jnp.float32)).astype(x.dtype)


def l2norm(x):
    xf = x.astype(jnp.float32)
    return xf * lax.rsqrt(jnp.sum(xf * xf, axis=-1, keepdims=True) + EPS)


def centred_conv(x, w):
    pad = CONV_K // 2
    n = x.shape[1]
    xp = jnp.pad(x, ((0, 0), (pad, pad), (0, 0)))
    out = xp[:, 0:n] * w[0]
    for j in range(1, CONV_K):
        out = out + xp[:, j:j + n] * w[j]
    return out


def box_mean(x, window, axis):
    n = x.shape[axis]
    cs = jnp.cumsum(x.astype(jnp.float32), axis=axis)
    pad_cfg = [(0, 0)] * x.ndim
    pad_cfg[axis] = (1, 0)
    cs = jnp.pad(cs, pad_cfg)
    t = jnp.arange(n)
    lo = jnp.clip(t - window // 2, 0, n)
    hi = jnp.clip(t - window // 2 + window, 0, n)
    s = jnp.take(cs, hi, axis=axis) - jnp.take(cs, lo, axis=axis)
    cnt_shape = [1] * x.ndim
    cnt_shape[axis] = n
    cnt = (hi - lo).astype(jnp.float32).reshape(cnt_shape)
    return (s / cnt).astype(x.dtype)


def pool_mix(u, pool_w, pool_scale, rows):
    b, n, _ = u.shape
    outs = []
    for gi, win in enumerate(POOL_WINDOWS):
        ug = u[..., gi * POOL_GC:(gi + 1) * POOL_GC]
        if rows is None:
            m = box_mean(ug, win, 1)
        else:
            grid = ug.reshape(b, rows, GRID_W, POOL_GC)
            m = box_mean(box_mean(grid, win, 1), win, 2).reshape(b, n, POOL_GC)
        outs.append((m - ug) @ pool_w[gi])
    return jnp.concatenate(outs, axis=-1) * pool_scale


def gated_delta_chunked(q, k, v, g, beta, s0):
    b, n, h, _ = q.shape
    nc = n // CHUNK
    f32 = jnp.float32

    def chunks(x):
        return x.astype(f32).reshape(b, nc, CHUNK, h, -1).transpose(0, 3, 1, 2, 4)

    q, k, v = chunks(q), chunks(k), chunks(v)
    beta = chunks(beta[..., None])[..., 0]
    gc = jnp.cumsum(chunks(g[..., None])[..., 0], axis=-1)
    idx = jnp.arange(CHUNK)
    causal = idx[:, None] >= idx[None, :]
    strict = idx[:, None] > idx[None, :]
    diff = gc[..., :, None] - gc[..., None, :]
    decay = jnp.where(causal, jnp.exp(jnp.where(causal, diff, 0.0)), 0.0)
    k_beta = k * beta[..., None]
    lower = jnp.where(strict, jnp.einsum('bhnik,bhnjk->bhnij', k_beta, k) * decay, 0.0)
    a_mat = lower + jnp.eye(CHUNK, dtype=f32)
    u = lax.linalg.triangular_solve(a_mat, v * beta[..., None], left_side=True, lower=True)
    w = lax.linalg.triangular_solve(a_mat, k_beta * jnp.exp(gc)[..., None], left_side=True, lower=True)
    attn = jnp.einsum('bhnik,bhnjk->bhnij', q, k) * decay
    xs = tuple(jnp.moveaxis(t, 2, 0) for t in (q, k, u, w, attn, gc))

    def step(s, inp):
        qi, ki, ui, wi, ai, gi = inp
        v_new = ui - jnp.einsum('bhck,bhkv->bhcv', wi, s)
        o = jnp.einsum('bhck,bhkv->bhcv', qi * jnp.exp(gi)[..., None], s) + jnp.einsum('bhij,bhjv->bhiv', ai, v_new)
        g_last = gi[..., -1]
        s = s * jnp.exp(g_last)[..., None, None] + jnp.einsum(
            'bhck,bhcv->bhkv', ki * jnp.exp(g_last[..., None] - gi)[..., None], v_new)
        return s, o

    s_final, o = lax.scan(step, s0.astype(f32), xs)
    o = jnp.moveaxis(o, 0, 2)
    o = o.transpose(0, 2, 3, 1, 4).reshape(b, n, h, DV)
    return o, s_final


def mixer(h, w_in, conv_w, a_log, dt_bias, dn_norm, pool_w, pool_scale, w_out, s0_f, s0_b, rows):
    b, n, _ = h.shape
    proj = h @ w_in
    qkv = jax.nn.silu(centred_conv(proj[..., :3 * DN_W], conv_w))
    z = proj[..., 3 * DN_W:4 * DN_W]
    ab = proj[..., 4 * DN_W:4 * DN_W + 4 * DN_HEADS].astype(jnp.float32).reshape(b, n, 4, DN_HEADS)
    u_pool = proj[..., 4 * DN_W + 4 * DN_HEADS:]
    q = l2norm(qkv[..., :DN_W].reshape(b, n, DN_HEADS, DK)) * (DK ** -0.5)
    k = l2norm(qkv[..., DN_W:2 * DN_W].reshape(b, n, DN_HEADS, DK))
    v = qkv[..., 2 * DN_W:].reshape(b, n, DN_HEADS, DV)
    beta = jax.nn.sigmoid(ab[:, :, 0:2])
    g = -jnp.exp(a_log.astype(jnp.float32)) * jax.nn.softplus(ab[:, :, 2:4] + dt_bias.astype(jnp.float32))
    o_f, s_f = gated_delta_chunked(q, k, v, g[:, :, 0], beta[:, :, 0], s0_f)
    o_b, s_b = gated_delta_chunked(jnp.flip(q, 1), jnp.flip(k, 1), jnp.flip(v, 1),
                                   jnp.flip(g[:, :, 1], 1), jnp.flip(beta[:, :, 1], 1), s0_b)
    o = o_f + jnp.flip(o_b, 1)
    o = rmsnorm(o, dn_norm) * jax.nn.silu(z.reshape(b, n, DN_HEADS, DV).astype(jnp.float32))
    o = o.reshape(b, n, DN_W).astype(h.dtype)
    p = pool_mix(u_pool, pool_w, pool_scale, rows).astype(h.dtype)
    out = jnp.concatenate([o, p], axis=-1) @ w_out
    return out, s_f, s_b


def layer(x, ada, norm_mix, norm_ffn, w_in, conv_w, a_log, dt_bias, dn_norm, pool_w, pool_scale,
          w_out, w_gu, w_down, s0_f, s0_b, rows):
    sh_m, sc_m, g_m, sh_f, sc_f, g_f = jnp.split(ada, 6, axis=-1)
    h = (rmsnorm(x, norm_mix) * (1 + sc_m) + sh_m).astype(x.dtype)
    mix, s_f, s_b = mixer(h, w_in, conv_w, a_log, dt_bias, dn_norm, pool_w, pool_scale, w_out, s0_f, s0_b, rows)
    x = (x + g_m * mix).astype(x.dtype)
    h = (rmsnorm(x, norm_ffn) * (1 + sc_f) + sh_f).astype(x.dtype)
    gu = h @ w_gu
    x = (x + g_f * ((jax.nn.silu(gu[..., :D_FF]) * gu[..., D_FF:]) @ w_down)).astype(x.dtype)
    return x, s_f, s_b


def setup_inputs(seed: int = 0) -> dict:
    key = jax.random.key(seed)
    ks = jax.random.split(key, 20)
    nrm = jax.random.normal
    f32 = jnp.float32
    return {
        "x_prompt": nrm(ks[0], (BATCH, SEQ, D_MODEL), f32),
        "x_sample": nrm(ks[1], (DEC_BATCH, DEC_SEQ, D_MODEL), f32),
        "c": nrm(ks[2], (DEC_BATCH, D_MODEL), f32),
        "state_delta": 0.5 * nrm(ks[3], (DEC_BATCH, DEPTH, 2, DN_HEADS, DK, DV), f32),
        "c_ctx": nrm(ks[4], (D_MODEL,), f32),
        "w_ada": 0.5 * D_MODEL ** -0.5 * nrm(ks[5], (DEPTH, D_MODEL, 6 * D_MODEL), f32),
        "b_ada": 0.02 * nrm(ks[6], (DEPTH, 6 * D_MODEL), f32),
        "norm_mix": 1.0 + 0.05 * nrm(ks[7], (DEPTH, D_MODEL), f32),
        "norm_ffn": 1.0 + 0.05 * nrm(ks[8], (DEPTH, D_MODEL), f32),
        "w_in": D_MODEL ** -0.5 * nrm(ks[9], (DEPTH, D_MODEL, IN_COLS), f32),
        "conv_w": CONV_K ** -0.5 * nrm(ks[10], (DEPTH, CONV_K, 3 * DN_W), f32),
        "a_log": jnp.log(jax.random.uniform(ks[11], (DEPTH, 2, DN_HEADS), f32, 1.0, 16.0)),
        "dt_bias": jnp.log(jnp.expm1(jax.random.uniform(ks[12], (DEPTH, 2, DN_HEADS), f32, 0.001, 0.1))),
        "dn_norm": 1.0 + 0.05 * nrm(ks[13], (DEPTH, DV), f32),
        "pool_w": POOL_GC ** -0.5 * nrm(ks[14], (DEPTH, POOL_GROUPS, POOL_GC, POOL_GC), f32),
        "pool_scale": 1.0 + 0.1 * nrm(ks[15], (DEPTH, POOL_W), f32),
        "w_out": MIX_W ** -0.5 * nrm(ks[16], (DEPTH, MIX_W, D_MODEL), f32),
        "w_gu": D_MODEL ** -0.5 * nrm(ks[17], (DEPTH, D_MODEL, 2 * D_FF), f32),
        "w_down": D_FF ** -0.5 * nrm(ks[18], (DEPTH, D_FF, D_MODEL), f32),
        "final_norm": 1.0 + 0.05 * nrm(ks[19], (D_MODEL,), f32),
    }


def reference(x_prompt, x_sample, c, state_delta, c_ctx, w_ada, b_ada, norm_mix, norm_ffn, w_in, conv_w,
              a_log, dt_bias, dn_norm, pool_w, pool_scale, w_out, w_gu, w_down, final_norm):
    xp = x_prompt
    bp = x_prompt.shape[0]
    zero_state = jnp.zeros((bp, DN_HEADS, DK, DV), jnp.float32)
    ctx_states = []
    for l in range(DEPTH):
        ada = (jax.nn.silu(c_ctx) @ w_ada[l] + b_ada[l])[None, None, :]
        xp, s_f, s_b = layer(xp, ada, norm_mix[l], norm_ffn[l], w_in[l], conv_w[l], a_log[l], dt_bias[l],
                             dn_norm[l], pool_w[l], pool_scale[l], w_out[l], w_gu[l], w_down[l],
                             zero_state, zero_state, None)
        ctx_states.append(jnp.stack([s_f, s_b], axis=1))
    new_state_delta = jnp.stack(ctx_states, axis=1)
    y_prompt = rmsnorm(xp, final_norm)

    xs = x_sample
    rows = x_sample.shape[1] // GRID_W
    for l in range(DEPTH):
        ada = (jax.nn.silu(c) @ w_ada[l] + b_ada[l])[:, None, :]
        xs, _, _ = layer(xs, ada, norm_mix[l], norm_ffn[l], w_in[l], conv_w[l], a_log[l], dt_bias[l],
                         dn_norm[l], pool_w[l], pool_scale[l], w_out[l], w_gu[l], w_down[l],
                         state_delta[:, l, 0], state_delta[:, l, 1], rows)
    y_sample = rmsnorm(xs, final_norm)
    return (y_prompt, y_sample, new_state_delta)
```

```python
import functools

import jax
import jax.numpy as jnp
from jax import lax
from jax.experimental import pallas as pl
from jax.experimental.pallas import tpu as pltpu

F32 = jnp.float32
BF16 = jnp.bfloat16

D_MODEL = 1024
DN_HEADS = 4
HEAD_W = 128
DN_W = DN_HEADS * HEAD_W
POOL_GROUPS = 4
POOL_W = POOL_GROUPS * HEAD_W
POOL_WINDOWS = (2, 4, 8, 16)
GRID_W = 64
CONV_K = 5
CHUNK = 64
D_FF = 2816
EPS = 1e-6

PROJ_W = 3 * DN_W + DN_W + POOL_W + HEAD_W
Z_BLK = 3 * DN_HEADS
POOL_BLK512 = 4
GATE_BLK = 20
ADA_ROWS = 16

V7X_VMEM_LIMIT = 56 * 1024 * 1024

TM_INPROJ = 256
TM_POST = 256
POOL_TB = 256
PREP_UNROLL = 2


def _sigmoid(x):
    return 1.0 / (1.0 + jnp.exp(-x))


def _silu(x):
    return x * _sigmoid(x)


def _softplus(x):
    return jnp.maximum(x, 0.0) + jnp.log(1.0 + jnp.exp(-jnp.abs(x)))


def _bdot(a, b):
    return jnp.dot(a.astype(BF16), b.astype(BF16), preferred_element_type=F32)


def _iota(shape, dim):
    return lax.broadcasted_iota(jnp.int32, shape, dim)


def _ada_kernel(c_ref, w_ref, b_ref, o_ref):
    o_ref[...] = jnp.dot(_silu(c_ref[...]), w_ref[...], preferred_element_type=F32,
                         precision=lax.Precision.HIGHEST) + b_ref[...]


def _ada_call(c_all, w_ada, b_ada):
    depth, _, n_out = w_ada.shape
    tn = 1536
    return pl.pallas_call(
        _ada_kernel,
        out_shape=jax.ShapeDtypeStruct((depth, ADA_ROWS, n_out), F32),
        grid=(depth, n_out // tn),
        in_specs=[
            pl.BlockSpec((ADA_ROWS, D_MODEL), lambda l, j: (0, 0)),
            pl.BlockSpec((None, D_MODEL, tn), lambda l, j: (l, 0, j)),
            pl.BlockSpec((None, 1, tn), lambda l, j: (l, 0, j)),
        ],
        out_specs=pl.BlockSpec((None, ADA_ROWS, tn), lambda l, j: (l, 0, j)),
        compiler_params=pltpu.CompilerParams(vmem_limit_bytes=V7X_VMEM_LIMIT),
        name="ada",
    )(c_all, w_ada, b_ada.reshape(depth, 1, n_out))


def _modulated_norm(x, gain, shift, scale):
    y = x * lax.rsqrt(jnp.mean(x * x, axis=-1, keepdims=True) + EPS) * gain
    return y * (1.0 + scale) + shift


def _inproj_kernel(x_ref, ada_ref, nw_ref, w_ref, o_ref):
    h = _modulated_norm(x_ref[...], nw_ref[...], ada_ref[0:1, :], ada_ref[1:2, :])
    o_ref[...] = jnp.dot(h.astype(BF16), w_ref[...], preferred_element_type=F32)


def _ada_row_map(n_seq, seq_len, tm, per_seq):
    if not per_seq:
        return lambda i: (0, 0, 0)
    return lambda i: (1 + (i * tm) // seq_len, 0, 0)


def _inproj_call(x2, ada_l, norm_w, w_in_p, seq_len, per_seq):
    n_tok = x2.shape[0]
    tm = TM_INPROJ
    return pl.pallas_call(
        _inproj_kernel,
        out_shape=jax.ShapeDtypeStruct((n_tok, PROJ_W), F32),
        grid=(n_tok // tm,),
        in_specs=[
            pl.BlockSpec((tm, D_MODEL), lambda i: (i, 0)),
            pl.BlockSpec((None, 6, D_MODEL), _ada_row_map(n_tok // seq_len, seq_len, tm, per_seq)),
            pl.BlockSpec((1, D_MODEL), lambda i: (0, 0)),
            pl.BlockSpec((D_MODEL, PROJ_W), lambda i: (0, 0), pipeline_mode=pl.Buffered(1)),
        ],
        out_specs=pl.BlockSpec((tm, PROJ_W), lambda i: (i, 0)),
        compiler_params=pltpu.CompilerParams(vmem_limit_bytes=V7X_VMEM_LIMIT),
        name="inproj",
    )(x2, ada_l, norm_w.reshape(1, D_MODEL), w_in_p)


def _col_bcast(x, lane, idx):
    col = jnp.sum(jnp.where(lane == idx, x, 0.0), axis=-1, keepdims=True)
    return jnp.broadcast_to(col, x.shape)


def _unit_tri_inverse(lbd, eye, blk16):
    d = jnp.where(blk16, lbd, 0.0)
    lo = jnp.where(blk16, 0.0, lbd)
    d2 = _bdot(d, d)
    d3 = _bdot(d, d2)
    d4 = _bdot(d2, d2)
    d8 = _bdot(d4, d4)
    p1 = eye - d + d2 - d3
    p2 = p1 + _bdot(p1, d4)
    dinv = p2 + _bdot(p2, d8)
    m = _bdot(dinv, lo)
    m2 = _bdot(m, m)
    m3 = _bdot(m, m2)
    ninv = eye - m + m2 - m3
    return _bdot(ninv, dinv)


def _dn_kernel(*refs, seq_len, has_s0, out_state):
    n_ch = seq_len // CHUNK
    it = iter(refs)
    q_ref, k_ref, v_ref, z_ref, g_ref = (next(it) for _ in range(5))
    cwq_ref, cwk_ref, cwv_ref = (next(it) for _ in range(3))
    alog_ref, dtb_ref, dnn_ref = (next(it) for _ in range(3))
    s0_ref = next(it) if has_s0 else None
    o_ref = next(it)
    st_ref = next(it) if out_state else None
    wq_s, u_s, kdt_s, dl_s, at_s, vn_s, qs_s = (next(it) for _ in range(7))

    head = pl.program_id(1)
    row = _iota((CHUNK, 128), 0)
    lane = _iota((CHUNK, 128), 1)
    is_f = lane < CHUNK
    j = lane & (CHUNK - 1)
    incl = (is_f & (row >= j)) | (~is_f & (row <= j))
    strict = (is_f & (row > j)) | (~is_f & (row < j))
    diag = row == j
    r128 = _iota((128, 128), 0)
    c128 = _iota((128, 128), 1)
    eye = jnp.where(r128 == c128, 1.0, 0.0).astype(F32)
    blk16 = (r128 >> 4) == (c128 >> 4)
    lane_f128 = c128 < CHUNK
    cum_sel = ((r128 < CHUNK) & (c128 <= r128)) | ((r128 >= CHUNK) & (c128 >= r128 - CHUNK))
    cum_mat = jnp.where(cum_sel, 1.0, 0.0).astype(F32)[:, :CHUNK]

    neg_a = -jnp.exp(alog_ref[...])
    dtb = dtb_ref[...]

    def conv_silu(ref, cw_ref, c, r0):
        cur = ref[pl.ds(r0, CHUNK), :]
        p0 = pl.multiple_of(jnp.maximum(r0 - 8, 0), 8)
        prev = jnp.where(c > 0, ref[pl.ds(p0, 8), :], 0.0)
        n0 = pl.multiple_of(jnp.minimum(r0 + CHUNK, seq_len - 8), 8)
        nxt = jnp.where(c < n_ch - 1, ref[pl.ds(n0, 8), :], 0.0)
        ext = jnp.concatenate([prev, cur, nxt], axis=0)
        acc = ext[6:6 + CHUNK] * cw_ref[0:1, :]
        for t in range(1, CONV_K):
            acc = acc + ext[6 + t:6 + t + CHUNK] * cw_ref[t:t + 1, :]
        return _silu(acc)

    def l2n(x):
        return x * lax.rsqrt(jnp.sum(x * x, axis=-1, keepdims=True) + EPS)

    def prep(c):
        r0 = pl.multiple_of(c * CHUNK, CHUNK)
        q = l2n(conv_silu(q_ref, cwq_ref, c, r0)) * (HEAD_W ** -0.5)
        k = l2n(conv_silu(k_ref, cwk_ref, c, r0))
        v = conv_silu(v_ref, cwv_ref, c, r0)

        ab = g_ref[pl.ds(r0, CHUNK), :]
        gact = jnp.where(lane < 2 * DN_HEADS, _sigmoid(ab), neg_a * _softplus(ab + dtb))
        cums = jnp.dot(cum_mat, gact, preferred_element_type=F32, precision=lax.Precision.HIGHEST)
        beta_f = _col_bcast(gact, lane, head)
        beta_b = _col_bcast(gact, lane, DN_HEADS + head)
        gc_f = _col_bcast(cums[:CHUNK], lane, 2 * DN_HEADS + head)
        gc_b = _col_bcast(cums[CHUNK:], lane, 3 * DN_HEADS + head)

        gc_col = jnp.where(is_f, gc_f, gc_b)
        gc_row = jnp.sum(jnp.where(diag, gc_col, 0.0), axis=0, keepdims=True)
        decay = jnp.where(incl, jnp.exp(jnp.where(incl, gc_col - gc_row, 0.0)), 0.0)

        kb = k.astype(BF16)
        gram = lax.dot_general(jnp.concatenate([q.astype(BF16), kb], axis=0),
                               jnp.concatenate([kb, kb], axis=0),
                               (((1,), (1,)), ((), ())), preferred_element_type=F32)
        attn = gram[:CHUNK] * decay
        beta2 = jnp.where(is_f, beta_f, beta_b)
        l2 = jnp.where(strict, beta2 * gram[CHUNK:] * decay, 0.0)
        lbd = jnp.concatenate([jnp.where(is_f, l2, 0.0), jnp.where(is_f, 0.0, l2)], axis=0)
        t_inv = _unit_tri_inverse(lbd, eye, blk16)

        e_f = jnp.exp(gc_f)
        e_b = jnp.exp(gc_b)
        kbeta_f = k * beta_f
        kbeta_b = k * beta_b
        rhs = jnp.concatenate([
            jnp.concatenate([v * beta_f, kbeta_f * e_f], axis=1),
            jnp.concatenate([v * beta_b, kbeta_b * e_b], axis=1)], axis=0)
        uw = _bdot(t_inv, rhs)
        u_s[0, c] = uw[:CHUNK, :HEAD_W]
        u_s[1, c] = uw[CHUNK:, :HEAD_W]
        wq_s[0, c] = jnp.concatenate([uw[:CHUNK, HEAD_W:], q * e_f], axis=0).astype(BF16)
        wq_s[1, c] = jnp.concatenate([uw[CHUNK:, HEAD_W:], q * e_b], axis=0).astype(BF16)

        gl_f = gc_f[CHUNK - 1:CHUNK, :]
        gl_b = gc_b[0:1, :]
        kdec = jnp.concatenate([k * jnp.exp(gl_f - gc_f), k * jnp.exp(gl_b - gc_b)], axis=0)
        kdec_t = kdec.T
        kdt_s[0, c] = jnp.where(lane_f128, kdec_t, 0.0).astype(BF16)
        kdt_s[1, c] = jnp.where(lane_f128, 0.0, kdec_t).astype(BF16)
        dl_s[0, c] = jnp.broadcast_to(jnp.exp(gl_f), (8, 128))
        dl_s[1, c] = jnp.broadcast_to(jnp.exp(gl_b), (8, 128))
        at_s[c] = attn.astype(BF16)

    def prep_body(i, carry):
        for t in range(PREP_UNROLL):
            prep(i * PREP_UNROLL + t)
        return carry

    lax.fori_loop(0, n_ch // PREP_UNROLL, prep_body, 0)

    def scan_dir(d, c, s):
        r = jnp.dot(wq_s[d, c], s.astype(BF16), preferred_element_type=F32)
        vn = (u_s[d, c] - r[:CHUNK]).astype(BF16)
        vn_s[d, c] = vn
        qs_s[d, c] = r[CHUNK:]
        upd = jnp.dot(kdt_s[d, c], jnp.concatenate([vn, vn], axis=0), preferred_element_type=F32)
        return s * dl_s[d, c][0:1, :] + upd

    def scan_body(i, carry):
        s_f, s_b = carry
        return scan_dir(0, i, s_f), scan_dir(1, n_ch - 1 - i, s_b)

    if has_s0:
        init = (s0_ref[0], s0_ref[1])
    else:
        init = (jnp.zeros((HEAD_W, HEAD_W), F32), jnp.zeros((HEAD_W, HEAD_W), F32))
    s_f, s_b = lax.fori_loop(0, n_ch, scan_body, init)
    if out_state:
        st_ref[0] = s_f
        st_ref[1] = s_b

    dnn = dnn_ref[...]

    def out_chunk(c):
        r0 = pl.multiple_of(c * CHUNK, CHUNK)
        vst = jnp.concatenate([vn_s[0, c], vn_s[1, c]], axis=0)
        o = qs_s[0, c] + qs_s[1, c] + jnp.dot(at_s[c], vst, preferred_element_type=F32)
        y = o * lax.rsqrt(jnp.mean(o * o, axis=-1, keepdims=True) + EPS) * dnn
        o_ref[pl.ds(r0, CHUNK), :] = (y * _silu(z_ref[pl.ds(r0, CHUNK), :])).astype(o_ref.dtype)

    def out_body(i, carry):
        for t in range(PREP_UNROLL):
            out_chunk(i * PREP_UNROLL + t)
        return carry

    lax.fori_loop(0, n_ch // PREP_UNROLL, out_body, 0)


def _dn_call(proj3, conv_w, alog_row, dtb_row, dn_norm, state_delta, layer, out_state):
    n_seq, seq_len, _ = proj3.shape
    n_ch = seq_len // CHUNK
    has_s0 = state_delta is not None
    tile = lambda off: pl.BlockSpec((None, seq_len, HEAD_W), lambda b, h: (b, 0, off + h))
    cw = lambda off: pl.BlockSpec((CONV_K, HEAD_W), lambda b, h: (0, off + h))
    row = pl.BlockSpec((1, HEAD_W), lambda b, h: (0, 0))
    in_specs = [tile(0), tile(DN_HEADS), tile(2 * DN_HEADS), tile(Z_BLK),
                pl.BlockSpec((None, seq_len, HEAD_W), lambda b, h: (b, 0, GATE_BLK)),
                cw(0), cw(DN_HEADS), cw(2 * DN_HEADS), row, row, row]
    args = [proj3, proj3, proj3, proj3, proj3, conv_w, conv_w, conv_w, alog_row, dtb_row,
            dn_norm.reshape(1, HEAD_W)]
    if has_s0:
        in_specs.append(pl.BlockSpec((None, None, 2, None, HEAD_W, HEAD_W),
                                     lambda b, h: (b, layer, 0, h, 0, 0)))
        args.append(state_delta)
    out_shape = [jax.ShapeDtypeStruct((n_seq, seq_len, DN_W), BF16)]
    out_specs = [pl.BlockSpec((None, seq_len, HEAD_W), lambda b, h: (b, 0, h))]
    if out_state:
        out_shape.append(jax.ShapeDtypeStruct((n_seq, 2, DN_HEADS, HEAD_W, HEAD_W), F32))
        out_specs.append(pl.BlockSpec((None, 2, None, HEAD_W, HEAD_W), lambda b, h: (b, 0, h, 0, 0)))
    scratch = [
        pltpu.VMEM((2, n_ch, 2 * CHUNK, HEAD_W), BF16),
        pltpu.VMEM((2, n_ch, CHUNK, HEAD_W), F32),
        pltpu.VMEM((2, n_ch, HEAD_W, 2 * CHUNK), BF16),
        pltpu.VMEM((2, n_ch, 8, HEAD_W), F32),
        pltpu.VMEM((n_ch, CHUNK, 2 * CHUNK), BF16),
        pltpu.VMEM((2, n_ch, CHUNK, HEAD_W), BF16),
        pltpu.VMEM((2, n_ch, CHUNK, HEAD_W), F32),
    ]
    outs = pl.pallas_call(
        functools.partial(_dn_kernel, seq_len=seq_len, has_s0=has_s0, out_state=out_state),
        out_shape=out_shape,
        grid=(n_seq, DN_HEADS),
        in_specs=in_specs,
        out_specs=out_specs,
        scratch_shapes=scratch,
        compiler_params=pltpu.CompilerParams(vmem_limit_bytes=V7X_VMEM_LIMIT),
        name="deltanet",
    )(*args)
    return (outs[0], outs[1]) if out_state else (outs[0], None)


def _split3(x):
    hi = x.astype(BF16)
    r1 = x - hi.astype(F32)
    mid = r1.astype(BF16)
    lo = (r1 - mid.astype(F32)).astype(BF16)
    return hi, mid, lo


def _window(pos, win, n):
    lo = jnp.clip(pos - win // 2, 0, n)
    hi = jnp.clip(pos - win // 2 + win, 0, n)
    return lo, hi


def _pool_kernel(u_ref, pw_ref, ps_ref, o_ref, m_s, *, seq_len, rows):
    period = GRID_W if rows else seq_len
    tb = min(POOL_TB, seq_len)
    r_i = _iota((tb, tb), 0)
    c_i = _iota((tb, tb), 1)
    shift = period.bit_length() - 1
    same_line = (r_i >> shift) == (c_i >> shift)
    pos_r = r_i & (period - 1)
    pos_c = c_i & (period - 1)
    pos_col = _iota((tb, HEAD_W), 0) & (period - 1)
    for g, win in enumerate(POOL_WINDOWS):
        cols = slice(g * HEAD_W, (g + 1) * HEAD_W)
        if rows:
            run = None
            prev_lo = prev_hi = 0
            for r in range(rows):
                lo, hi = max(r - win // 2, 0), min(r - win // 2 + win, rows)
                for a in range(prev_hi, hi):
                    slab = u_ref[a * GRID_W:(a + 1) * GRID_W, cols]
                    run = slab if run is None else run + slab
                for a in range(prev_lo, lo):
                    run = run - u_ref[a * GRID_W:(a + 1) * GRID_W, cols]
                prev_lo, prev_hi = lo, hi
                m_s[r * GRID_W:(r + 1) * GRID_W, :] = run / float(hi - lo)
        lo_r, hi_r = _window(pos_r, win, period)
        band = jnp.where(same_line & (pos_c >= lo_r) & (pos_c < hi_r), 1.0, 0.0).astype(BF16)
        lo_c, hi_c = _window(pos_col, win, period)
        cnt = (hi_c - lo_c).astype(F32)
        pw = pw_ref[g]
        scale = ps_ref[:, cols]
        for t0 in range(0, seq_len, tb):
            ug = u_ref[t0:t0 + tb, cols]
            src = m_s[t0:t0 + tb, :] if rows else ug
            hi3, mid3, lo3 = _split3(src)
            box = (jnp.dot(band, hi3, preferred_element_type=F32)
                   + jnp.dot(band, mid3, preferred_element_type=F32)
                   + jnp.dot(band, lo3, preferred_element_type=F32))
            mean = box / cnt
            o_ref[t0:t0 + tb, cols] = (_bdot(mean - ug, pw) * scale).astype(o_ref.dtype)


def _pool_call(proj3, pool_w_bf, pool_scale, rows):
    n_seq, seq_len, _ = proj3.shape
    return pl.pallas_call(
        functools.partial(_pool_kernel, seq_len=seq_len, rows=rows),
        out_shape=jax.ShapeDtypeStruct((n_seq, seq_len, POOL_W), BF16),
        grid=(n_seq,),
        in_specs=[
            pl.BlockSpec((None, seq_len, POOL_W), lambda b: (b, 0, POOL_BLK512)),
            pl.BlockSpec((POOL_GROUPS, HEAD_W, HEAD_W), lambda b: (0, 0, 0)),
            pl.BlockSpec((1, POOL_W), lambda b: (0, 0)),
        ],
        out_specs=pl.BlockSpec((None, seq_len, POOL_W), lambda b: (b, 0, 0)),
        scratch_shapes=[pltpu.VMEM((seq_len, HEAD_W), F32)],
        compiler_params=pltpu.CompilerParams(vmem_limit_bytes=V7X_VMEM_LIMIT),
        name="pool",
    )(proj3, pool_w_bf, pool_scale.reshape(1, POOL_W))


def _post_kernel(x_ref, o_ref, p_ref, ada_ref, nf_ref, fin_ref, wo_ref, wgu_ref, wd_ref, y_ref, *, final):
    mix = (jnp.dot(o_ref[...], wo_ref[:DN_W, :], preferred_element_type=F32)
           + jnp.dot(p_ref[...], wo_ref[DN_W:, :], preferred_element_type=F32))
    x = x_ref[...] + ada_ref[2:3, :] * mix
    h = _modulated_norm(x, nf_ref[...], ada_ref[3:4, :], ada_ref[4:5, :])
    gu = jnp.dot(h.astype(BF16), wgu_ref[...], preferred_element_type=F32)
    act = _silu(gu[:, :D_FF]) * gu[:, D_FF:]
    x = x + ada_ref[5:6, :] * jnp.dot(act.astype(BF16), wd_ref[...], preferred_element_type=F32)
    if final:
        x = x * lax.rsqrt(jnp.mean(x * x, axis=-1, keepdims=True) + EPS) * fin_ref[...]
    y_ref[...] = x


def _post_call(x2, o2, p2, ada_l, norm_ffn, final_norm, w_out_bf, w_gu_bf, w_down_bf, seq_len, per_seq, final):
    n_tok = x2.shape[0]
    tm = TM_POST
    const = lambda shape: pl.BlockSpec(shape, lambda i: (0, 0), pipeline_mode=pl.Buffered(1))
    return pl.pallas_call(
        functools.partial(_post_kernel, final=final),
        out_shape=jax.ShapeDtypeStruct((n_tok, D_MODEL), F32),
        grid=(n_tok // tm,),
        in_specs=[
            pl.BlockSpec((tm, D_MODEL), lambda i: (i, 0)),
            pl.BlockSpec((tm, DN_W), lambda i: (i, 0)),
            pl.BlockSpec((tm, POOL_W), lambda i: (i, 0)),
            pl.BlockSpec((None, 6, D_MODEL), _ada_row_map(n_tok // seq_len, seq_len, tm, per_seq)),
            pl.BlockSpec((1, D_MODEL), lambda i: (0, 0)),
            pl.BlockSpec((1, D_MODEL), lambda i: (0, 0)),
            const((D_MODEL, D_MODEL)),
            const((D_MODEL, 2 * D_FF)),
            const((D_FF, D_MODEL)),
        ],
        out_specs=pl.BlockSpec((tm, D_MODEL), lambda i: (i, 0)),
        compiler_params=pltpu.CompilerParams(vmem_limit_bytes=V7X_VMEM_LIMIT),
        name="post",
    )(x2, o2, p2, ada_l, norm_ffn.reshape(1, D_MODEL), final_norm.reshape(1, D_MODEL),
      w_out_bf, w_gu_bf, w_down_bf)


def _gate_row(p):
    return jnp.zeros((1, HEAD_W), F32).at[0, 2 * DN_HEADS:4 * DN_HEADS].set(p.reshape(-1))


def _stream_layer(x2, seq_len, rows, per_seq, ada_l, lw, state_delta, layer, final):
    n_seq = x2.shape[0] // seq_len
    proj = _inproj_call(x2, ada_l, lw["norm_mix"], lw["w_in"], seq_len, per_seq)
    proj3 = proj.reshape(n_seq, seq_len, PROJ_W)
    o, st = _dn_call(proj3, lw["conv_w"], lw["alog_row"], lw["dtb_row"], lw["dn_norm"],
                     state_delta, layer, out_state=state_delta is None)
    p = _pool_call(proj3, lw["pool_w"], lw["pool_scale"], rows)
    x2 = _post_call(x2, o.reshape(-1, DN_W), p.reshape(-1, POOL_W), ada_l, lw["norm_ffn"], lw["final_norm"],
                    lw["w_out"], lw["w_gu"], lw["w_down"], seq_len, per_seq, final)
    return x2, st


def kernel(x_prompt, x_sample, c, state_delta, c_ctx, w_ada, b_ada, norm_mix, norm_ffn, w_in, conv_w, a_log, dt_bias, dn_norm, pool_w, pool_scale, w_out, w_gu, w_down, final_norm):
    depth = w_ada.shape[0]
    n_ctx, ctx_len, _ = x_prompt.shape
    n_lat, lat_len, _ = x_sample.shape
    assert 1 + n_lat <= ADA_ROWS
    c_all = jnp.concatenate([c_ctx[None, :], c, jnp.zeros((ADA_ROWS - 1 - n_lat, D_MODEL), F32)], axis=0)
    ada = _ada_call(c_all, w_ada, b_ada).reshape(depth, ADA_ROWS, 6, D_MODEL)

    qkvz = 4 * DN_W
    n_gate = 4 * DN_HEADS
    xp = x_prompt.reshape(-1, D_MODEL)
    xs = x_sample.reshape(-1, D_MODEL)
    states = []
    for l in range(depth):
        w_in_p = jnp.concatenate(
            [w_in[l][:, :qkvz], w_in[l][:, qkvz + n_gate:], w_in[l][:, qkvz:qkvz + n_gate],
             jnp.zeros((D_MODEL, HEAD_W - n_gate), F32)], axis=1).astype(BF16)
        lw = dict(norm_mix=norm_mix[l], norm_ffn=norm_ffn[l], w_in=w_in_p, conv_w=conv_w[l],
                  alog_row=_gate_row(a_log[l]), dtb_row=_gate_row(dt_bias[l]), dn_norm=dn_norm[l],
                  pool_w=pool_w[l].astype(BF16), pool_scale=pool_scale[l], w_out=w_out[l].astype(BF16),
                  w_gu=w_gu[l].astype(BF16), w_down=w_down[l].astype(BF16), final_norm=final_norm)
        final = l == depth - 1
        xp, st = _stream_layer(xp, ctx_len, None, False, ada[l], lw, None, l, final)
        states.append(st)
        xs, _ = _stream_layer(xs, lat_len, lat_len // GRID_W, True, ada[l], lw, state_delta, l, final)
    new_state = jnp.stack(states, axis=1)
    return (xp.reshape(x_prompt.shape), xs.reshape(x_sample.shape), new_state)
```

```python
import functools

import jax
import jax.numpy as jnp
from jax import lax
from jax.experimental import pallas as pl
from jax.experimental.pallas import tpu as pltpu

F32 = jnp.float32
BF16 = jnp.bfloat16

D_MODEL = 1024
DN_HEADS = 4
HEAD_W = 128
DN_W = DN_HEADS * HEAD_W
POOL_GROUPS = 4
POOL_W = POOL_GROUPS * HEAD_W
POOL_WINDOWS = (2, 4, 8, 16)
GRID_W = 64
CONV_K = 5
CHUNK = 64
D_FF = 2816
EPS = 1e-6

PROJ_W = 3 * DN_W + DN_W + POOL_W + HEAD_W
POOL_BLK512 = 4
GATE_BLK = 20
ADA_ROWS = 16

V7X_VMEM_LIMIT = 56 * 1024 * 1024

TM_INPROJ = 256
TM_POST = 256
POOL_TB = 256
PREP_UNROLL = 4
DN_HEADS_PER_STEP = 2


def _sigmoid(x):
    return 1.0 / (1.0 + jnp.exp(-x))


def _silu(x):
    return x * _sigmoid(x)


def _softplus(x):
    return jnp.maximum(x, 0.0) + jnp.log(1.0 + jnp.exp(-jnp.abs(x)))


def _bdot(a, b):
    return jnp.dot(a.astype(BF16), b.astype(BF16), preferred_element_type=F32)


def _iota(shape, dim):
    return lax.broadcasted_iota(jnp.int32, shape, dim)


def _ada_kernel(c_ref, w_ref, b_ref, o_ref):
    o_ref[...] = jnp.dot(_silu(c_ref[...]), w_ref[...], preferred_element_type=F32,
                         precision=lax.Precision.HIGHEST) + b_ref[...]


def _ada_call(c_all, w_ada, b_ada):
    depth, _, n_out = w_ada.shape
    tn = 1536
    return pl.pallas_call(
        _ada_kernel,
        out_shape=jax.ShapeDtypeStruct((depth, ADA_ROWS, n_out), F32),
        grid=(depth, n_out // tn),
        in_specs=[
            pl.BlockSpec((ADA_ROWS, D_MODEL), lambda l, j: (0, 0)),
            pl.BlockSpec((None, D_MODEL, tn), lambda l, j: (l, 0, j)),
            pl.BlockSpec((None, 1, tn), lambda l, j: (l, 0, j)),
        ],
        out_specs=pl.BlockSpec((None, ADA_ROWS, tn), lambda l, j: (l, 0, j)),
        compiler_params=pltpu.CompilerParams(vmem_limit_bytes=V7X_VMEM_LIMIT),
        name="ada",
    )(c_all, w_ada, b_ada.reshape(depth, 1, n_out))


def _modulated_norm(x, gain, shift, scale):
    y = x * lax.rsqrt(jnp.mean(x * x, axis=-1, keepdims=True) + EPS) * gain
    return y * (1.0 + scale) + shift


def _inproj_kernel(x_ref, ada_ref, nw_ref, w_ref, o_ref):
    h = _modulated_norm(x_ref[...], nw_ref[...], ada_ref[0:1, :], ada_ref[1:2, :])
    o_ref[...] = jnp.dot(h.astype(BF16), w_ref[...], preferred_element_type=F32)


def _ada_row_map(seq_len, tm, per_seq):
    if not per_seq:
        return lambda i: (0, 0, 0)
    return lambda i: (1 + (i * tm) // seq_len, 0, 0)


def _inproj_call(x2, ada_l, norm_w, w_in_p, seq_len, per_seq):
    n_tok = x2.shape[0]
    tm = TM_INPROJ
    return pl.pallas_call(
        _inproj_kernel,
        out_shape=jax.ShapeDtypeStruct((n_tok, PROJ_W), F32),
        grid=(n_tok // tm,),
        in_specs=[
            pl.BlockSpec((tm, D_MODEL), lambda i: (i, 0)),
            pl.BlockSpec((None, 6, D_MODEL), _ada_row_map(seq_len, tm, per_seq)),
            pl.BlockSpec((1, D_MODEL), lambda i: (0, 0)),
            pl.BlockSpec((D_MODEL, PROJ_W), lambda i: (0, 0), pipeline_mode=pl.Buffered(1)),
        ],
        out_specs=pl.BlockSpec((tm, PROJ_W), lambda i: (i, 0)),
        compiler_params=pltpu.CompilerParams(vmem_limit_bytes=V7X_VMEM_LIMIT),
        name="inproj",
    )(x2, ada_l, norm_w.reshape(1, D_MODEL), w_in_p)


def _col_bcast(x, lane, idx):
    col = jnp.sum(jnp.where(lane == idx, x, 0.0), axis=-1, keepdims=True)
    return jnp.broadcast_to(col, x.shape)


def _each(fn, *lists):
    return [fn(*args) for args in zip(*lists)]


def _unit_tri_inverse(lbds, eye, blk16):
    d = _each(lambda x: jnp.where(blk16, x, 0.0), lbds)
    lo = _each(lambda x: jnp.where(blk16, 0.0, x), lbds)
    d2 = _each(_bdot, d, d)
    d3 = _each(_bdot, d, d2)
    d4 = _each(_bdot, d2, d2)
    d8 = _each(_bdot, d4, d4)
    p1 = _each(lambda a, b, c: eye - a + b - c, d, d2, d3)
    p2 = _each(lambda p, x: p + _bdot(p, x), p1, d4)
    dinv = _each(lambda p, x: p + _bdot(p, x), p2, d8)
    m = _each(_bdot, dinv, lo)
    m2 = _each(_bdot, m, m)
    m3 = _each(_bdot, m, m2)
    ninv = _each(lambda a, b, c: eye - a + b - c, m, m2, m3)
    return _each(_bdot, ninv, dinv)


def _dn_kernel(*refs, seq_len, hp, has_s0, out_state):
    n_ch = seq_len // CHUNK
    unroll = min(PREP_UNROLL, n_ch)
    it = iter(refs)
    q_ref, k_ref, v_ref, z_ref, g_ref = (next(it) for _ in range(5))
    cwq_ref, cwk_ref, cwv_ref = (next(it) for _ in range(3))
    alog_ref, dtb_ref, dnn_ref = (next(it) for _ in range(3))
    s0_ref = next(it) if has_s0 else None
    o_ref = next(it)
    st_ref = next(it) if out_state else None
    wq_s, u_s, kdt_s, dl_s, at_s, vn_s, qs_s, s_s = (next(it) for _ in range(8))

    head0 = pl.program_id(1) * hp
    row = _iota((CHUNK, 128), 0)
    lane = _iota((CHUNK, 128), 1)
    is_f = lane < CHUNK
    j = lane & (CHUNK - 1)
    incl = (is_f & (row >= j)) | (~is_f & (row <= j))
    strict = (is_f & (row > j)) | (~is_f & (row < j))
    diag = row == j
    r128 = _iota((128, 128), 0)
    c128 = _iota((128, 128), 1)
    eye = jnp.where(r128 == c128, 1.0, 0.0).astype(F32)
    blk16 = (r128 >> 4) == (c128 >> 4)
    lane_f128 = c128 < CHUNK
    cum_sel = ((r128 < CHUNK) & (c128 <= r128)) | ((r128 >= CHUNK) & (c128 >= r128 - CHUNK))
    cum_mat = jnp.where(cum_sel, 1.0, 0.0).astype(F32)[:, :CHUNK]

    neg_a = -jnp.exp(alog_ref[...])
    dtb = dtb_ref[...]

    def conv_silu(ref, cw_ref, cols, c, r0):
        cur = ref[pl.ds(r0, CHUNK), cols]
        p0 = pl.multiple_of(jnp.maximum(r0 - 8, 0), 8)
        prev = jnp.where(c > 0, ref[pl.ds(p0, 8), cols], 0.0)
        n0 = pl.multiple_of(jnp.minimum(r0 + CHUNK, seq_len - 8), 8)
        nxt = jnp.where(c < n_ch - 1, ref[pl.ds(n0, 8), cols], 0.0)
        ext = jnp.concatenate([prev, cur, nxt], axis=0)
        acc = ext[6:6 + CHUNK] * cw_ref[0:1, cols]
        for t in range(1, CONV_K):
            acc = acc + ext[6 + t:6 + t + CHUNK] * cw_ref[t:t + 1, cols]
        return _silu(acc)

    def l2n(x):
        return x * lax.rsqrt(jnp.sum(x * x, axis=-1, keepdims=True) + EPS)

    def gates(r0):
        ab = g_ref[pl.ds(r0, CHUNK), :]
        gact = jnp.where(lane < 2 * DN_HEADS, _sigmoid(ab), neg_a * _softplus(ab + dtb))
        return gact, jnp.dot(cum_mat, gact, preferred_element_type=F32, precision=lax.Precision.HIGHEST)

    def prep_body(i, carry):
        cs = [i * unroll + t for t in range(unroll)]
        r0s = [pl.multiple_of(c * CHUNK, CHUNK) for c in cs]
        gact, cums = zip(*[gates(r0) for r0 in r0s])
        units = [(t, hh) for t in range(unroll) for hh in range(hp)]
        cols = lambda hh: slice(hh * HEAD_W, (hh + 1) * HEAD_W)
        q = [l2n(conv_silu(q_ref, cwq_ref, cols(hh), cs[t], r0s[t])) * (HEAD_W ** -0.5) for t, hh in units]
        k = [l2n(conv_silu(k_ref, cwk_ref, cols(hh), cs[t], r0s[t])) for t, hh in units]
        v = [conv_silu(v_ref, cwv_ref, cols(hh), cs[t], r0s[t]) for t, hh in units]
        beta_f = [_col_bcast(gact[t], lane, head0 + hh) for t, hh in units]
        beta_b = [_col_bcast(gact[t], lane, DN_HEADS + head0 + hh) for t, hh in units]
        gc_f = [_col_bcast(cums[t][:CHUNK], lane, 2 * DN_HEADS + head0 + hh) for t, hh in units]
        gc_b = [_col_bcast(cums[t][CHUNK:], lane, 3 * DN_HEADS + head0 + hh) for t, hh in units]

        def decay_of(gf, gb):
            gc_col = jnp.where(is_f, gf, gb)
            gc_row = jnp.sum(jnp.where(diag, gc_col, 0.0), axis=0, keepdims=True)
            return jnp.where(incl, jnp.exp(jnp.where(incl, gc_col - gc_row, 0.0)), 0.0)

        decay = _each(decay_of, gc_f, gc_b)

        def gram_of(qq, kk):
            kb = kk.astype(BF16)
            return lax.dot_general(jnp.concatenate([qq.astype(BF16), kb], axis=0),
                                   jnp.concatenate([kb, kb], axis=0),
                                   (((1,), (1,)), ((), ())), preferred_element_type=F32)

        gram = _each(gram_of, q, k)

        def lbd_of(g, dec, bf, bb):
            l2 = jnp.where(strict, jnp.where(is_f, bf, bb) * g[CHUNK:] * dec, 0.0)
            return jnp.concatenate([jnp.where(is_f, l2, 0.0), jnp.where(is_f, 0.0, l2)], axis=0)

        t_inv = _unit_tri_inverse(_each(lbd_of, gram, decay, beta_f, beta_b), eye, blk16)
        e_f = _each(jnp.exp, gc_f)
        e_b = _each(jnp.exp, gc_b)

        def rhs_of(kk, vv, bf, bb, ef, eb):
            return jnp.concatenate([
                jnp.concatenate([vv * bf, kk * bf * ef], axis=1),
                jnp.concatenate([vv * bb, kk * bb * eb], axis=1)], axis=0)

        uw = _each(_bdot, t_inv, _each(rhs_of, k, v, beta_f, beta_b, e_f, e_b))
        for n, (t, hh) in enumerate(units):
            c = cs[t]
            u_s[hh, 0, c] = uw[n][:CHUNK, :HEAD_W]
            u_s[hh, 1, c] = uw[n][CHUNK:, :HEAD_W]
            wq_s[hh, 0, c] = jnp.concatenate([uw[n][:CHUNK, HEAD_W:], q[n] * e_f[n]], axis=0).astype(BF16)
            wq_s[hh, 1, c] = jnp.concatenate([uw[n][CHUNK:, HEAD_W:], q[n] * e_b[n]], axis=0).astype(BF16)
            gl_f = gc_f[n][CHUNK - 1:CHUNK, :]
            gl_b = gc_b[n][0:1, :]
            kdec = jnp.concatenate([k[n] * jnp.exp(gl_f - gc_f[n]), k[n] * jnp.exp(gl_b - gc_b[n])], axis=0)
            kdt_s[hh, c] = kdec.T.astype(BF16)
            dl_s[hh, 0, c] = jnp.broadcast_to(jnp.exp(gl_f), (8, 128))
            dl_s[hh, 1, c] = jnp.broadcast_to(jnp.exp(gl_b), (8, 128))
            at_s[hh, c] = (gram[n][:CHUNK] * decay[n]).astype(BF16)
        return carry

    lax.fori_loop(0, n_ch // unroll, prep_body, 0)

    for hh in range(hp):
        for d in range(2):
            s_s[hh, d] = s0_ref[d, hh] if has_s0 else jnp.zeros((HEAD_W, HEAD_W), F32)

    def scan_dir(hh, d, c, first):
        s = s_s[hh, d]
        r = jnp.dot(wq_s[hh, d, c], s.astype(BF16), preferred_element_type=F32)
        vn = (u_s[hh, d, c] - r[:CHUNK]).astype(BF16)
        vn_s[hh, d, c] = vn
        qs_s[hh, c] = r[CHUNK:] if first else qs_s[hh, c] + r[CHUNK:]
        kdt = kdt_s[hh, c]
        zero = jnp.zeros_like(kdt)
        kdt = jnp.where(lane_f128, kdt, zero) if d == 0 else jnp.where(lane_f128, zero, kdt)
        upd = jnp.dot(kdt, jnp.concatenate([vn, vn], axis=0), preferred_element_type=F32)
        s_s[hh, d] = s * dl_s[hh, d, c][0:1, :] + upd

    def scan_body(first, i, carry):
        for hh in range(hp):
            scan_dir(hh, 0, i, first)
            scan_dir(hh, 1, n_ch - 1 - i, first)
        return carry

    lax.fori_loop(0, n_ch // 2, functools.partial(scan_body, True), 0)
    lax.fori_loop(n_ch // 2, n_ch, functools.partial(scan_body, False), 0)
    if out_state:
        for hh in range(hp):
            for d in range(2):
                st_ref[d, hh] = s_s[hh, d]

    dnn = dnn_ref[...]

    def out_chunk(c):
        r0 = pl.multiple_of(c * CHUNK, CHUNK)
        for hh in range(hp):
            cols = slice(hh * HEAD_W, (hh + 1) * HEAD_W)
            vst = jnp.concatenate([vn_s[hh, 0, c], vn_s[hh, 1, c]], axis=0)
            o = qs_s[hh, c] + jnp.dot(at_s[hh, c], vst, preferred_element_type=F32)
            y = o * lax.rsqrt(jnp.mean(o * o, axis=-1, keepdims=True) + EPS) * dnn
            o_ref[pl.ds(r0, CHUNK), cols] = (y * _silu(z_ref[pl.ds(r0, CHUNK), cols])).astype(o_ref.dtype)

    def out_body(i, carry):
        for t in range(unroll):
            out_chunk(i * unroll + t)
        return carry

    lax.fori_loop(0, n_ch // unroll, out_body, 0)


def _dn_call(proj3, conv_w, alog_row, dtb_row, dn_norm, state_delta, layer, out_state):
    n_seq, seq_len, _ = proj3.shape
    n_ch = seq_len // CHUNK
    hp = DN_HEADS_PER_STEP
    wid = hp * HEAD_W
    nblk = DN_HEADS // hp
    has_s0 = state_delta is not None
    tile = lambda off: pl.BlockSpec((None, seq_len, wid), lambda b, h: (b, 0, off + h))
    cw = lambda off: pl.BlockSpec((CONV_K, wid), lambda b, h: (0, off + h))
    row = pl.BlockSpec((1, HEAD_W), lambda b, h: (0, 0))
    in_specs = [tile(0), tile(nblk), tile(2 * nblk), tile(3 * nblk),
                pl.BlockSpec((None, seq_len, HEAD_W), lambda b, h: (b, 0, GATE_BLK)),
                cw(0), cw(nblk), cw(2 * nblk), row, row, row]
    args = [proj3, proj3, proj3, proj3, proj3, conv_w, conv_w, conv_w, alog_row, dtb_row,
            dn_norm.reshape(1, HEAD_W)]
    if has_s0:
        in_specs.append(pl.BlockSpec((None, None, 2, hp, HEAD_W, HEAD_W),
                                     lambda b, h: (b, layer, 0, h, 0, 0)))
        args.append(state_delta)
    out_shape = [jax.ShapeDtypeStruct((n_seq, seq_len, DN_W), BF16)]
    out_specs = [pl.BlockSpec((None, seq_len, wid), lambda b, h: (b, 0, h))]
    if out_state:
        out_shape.append(jax.ShapeDtypeStruct((n_seq, 2, DN_HEADS, HEAD_W, HEAD_W), F32))
        out_specs.append(pl.BlockSpec((None, 2, hp, HEAD_W, HEAD_W), lambda b, h: (b, 0, h, 0, 0)))
    scratch = [
        pltpu.VMEM((hp, 2, n_ch, 2 * CHUNK, HEAD_W), BF16),
        pltpu.VMEM((hp, 2, n_ch, CHUNK, HEAD_W), F32),
        pltpu.VMEM((hp, n_ch, HEAD_W, 2 * CHUNK), BF16),
        pltpu.VMEM((hp, 2, n_ch, 8, HEAD_W), F32),
        pltpu.VMEM((hp, n_ch, CHUNK, 2 * CHUNK), BF16),
        pltpu.VMEM((hp, 2, n_ch, CHUNK, HEAD_W), BF16),
        pltpu.VMEM((hp, n_ch, CHUNK, HEAD_W), F32),
        pltpu.VMEM((hp, 2, HEAD_W, HEAD_W), F32),
    ]
    outs = pl.pallas_call(
        functools.partial(_dn_kernel, seq_len=seq_len, hp=hp, has_s0=has_s0, out_state=out_state),
        out_shape=out_shape,
        grid=(n_seq, nblk),
        in_specs=in_specs,
        out_specs=out_specs,
        scratch_shapes=scratch,
        compiler_params=pltpu.CompilerParams(vmem_limit_bytes=V7X_VMEM_LIMIT),
        name="deltanet",
    )(*args)
    return (outs[0], outs[1]) if out_state else (outs[0], None)


def _split3(x):
    hi = x.astype(BF16)
    r1 = x - hi.astype(F32)
    mid = r1.astype(BF16)
    lo = (r1 - mid.astype(F32)).astype(BF16)
    return hi, mid, lo


def _window(pos, win, n):
    lo = jnp.clip(pos - win // 2, 0, n)
    hi = jnp.clip(pos - win // 2 + win, 0, n)
    return lo, hi


def _pool_kernel(u_ref, pw_ref, ps_ref, o_ref, m_s, *, seq_len, rows):
    period = GRID_W if rows else seq_len
    tb = min(POOL_TB, seq_len)
    r_i = _iota((tb, tb), 0)
    c_i = _iota((tb, tb), 1)
    shift = period.bit_length() - 1
    same_line = (r_i >> shift) == (c_i >> shift)
    pos_r = r_i & (period - 1)
    pos_c = c_i & (period - 1)
    pos_col = _iota((tb, HEAD_W), 0) & (period - 1)
    for g, win in enumerate(POOL_WINDOWS):
        cols = slice(g * HEAD_W, (g + 1) * HEAD_W)
        if rows:
            run = None
            prev_lo = prev_hi = 0
            for r in range(rows):
                lo, hi = max(r - win // 2, 0), min(r - win // 2 + win, rows)
                for a in range(prev_hi, hi):
                    slab = u_ref[a * GRID_W:(a + 1) * GRID_W, cols]
                    run = slab if run is None else run + slab
                for a in range(prev_lo, lo):
                    run = run - u_ref[a * GRID_W:(a + 1) * GRID_W, cols]
                prev_lo, prev_hi = lo, hi
                m_s[r * GRID_W:(r + 1) * GRID_W, :] = run / float(hi - lo)
        lo_r, hi_r = _window(pos_r, win, period)
        band = jnp.where(same_line & (pos_c >= lo_r) & (pos_c < hi_r), 1.0, 0.0).astype(BF16)
        lo_c, hi_c = _window(pos_col, win, period)
        cnt = (hi_c - lo_c).astype(F32)
        pw = pw_ref[g]
        scale = ps_ref[:, cols]
        for t0 in range(0, seq_len, tb):
            ug = u_ref[t0:t0 + tb, cols]
            src = m_s[t0:t0 + tb, :] if rows else ug
            hi3, mid3, lo3 = _split3(src)
            box = (jnp.dot(band, hi3, preferred_element_type=F32)
                   + jnp.dot(band, mid3, preferred_element_type=F32)
                   + jnp.dot(band, lo3, preferred_element_type=F32))
            mean = box / cnt
            o_ref[t0:t0 + tb, cols] = (_bdot(mean - ug, pw) * scale).astype(o_ref.dtype)


def _pool_call(proj3, pool_w_bf, pool_scale, rows):
    n_seq, seq_len, _ = proj3.shape
    return pl.pallas_call(
        functools.partial(_pool_kernel, seq_len=seq_len, rows=rows),
        out_shape=jax.ShapeDtypeStruct((n_seq, seq_len, POOL_W), BF16),
        grid=(n_seq,),
        in_specs=[
            pl.BlockSpec((None, seq_len, POOL_W), lambda b: (b, 0, POOL_BLK512)),
            pl.BlockSpec((POOL_GROUPS, HEAD_W, HEAD_W), lambda b: (0, 0, 0)),
            pl.BlockSpec((1, POOL_W), lambda b: (0, 0)),
        ],
        out_specs=pl.BlockSpec((None, seq_len, POOL_W), lambda b: (b, 0, 0)),
        scratch_shapes=[pltpu.VMEM((seq_len, HEAD_W), F32)],
        compiler_params=pltpu.CompilerParams(vmem_limit_bytes=V7X_VMEM_LIMIT),
        name="pool",
    )(proj3, pool_w_bf, pool_scale.reshape(1, POOL_W))


def _post_kernel(x_ref, o_ref, p_ref, ada_ref, nf_ref, fin_ref, wo_ref, wgu_ref, wd_ref, y_ref, *, final):
    mix = (jnp.dot(o_ref[...], wo_ref[:DN_W, :], preferred_element_type=F32)
           + jnp.dot(p_ref[...], wo_ref[DN_W:, :], preferred_element_type=F32))
    x = x_ref[...] + ada_ref[2:3, :] * mix
    h = _modulated_norm(x, nf_ref[...], ada_ref[3:4, :], ada_ref[4:5, :])
    gu = jnp.dot(h.astype(BF16), wgu_ref[...], preferred_element_type=F32)
    act = _silu(gu[:, :D_FF]) * gu[:, D_FF:]
    x = x + ada_ref[5:6, :] * jnp.dot(act.astype(BF16), wd_ref[...], preferred_element_type=F32)
    if final:
        x = x * lax.rsqrt(jnp.mean(x * x, axis=-1, keepdims=True) + EPS) * fin_ref[...]
    y_ref[...] = x


def _post_call(x2, o2, p2, ada_l, norm_ffn, final_norm, w_out_bf, w_gu_bf, w_down_bf, seq_len, per_seq, final):
    n_tok = x2.shape[0]
    tm = TM_POST
    const = lambda shape: pl.BlockSpec(shape, lambda i: (0, 0), pipeline_mode=pl.Buffered(1))
    return pl.pallas_call(
        functools.partial(_post_kernel, final=final),
        out_shape=jax.ShapeDtypeStruct((n_tok, D_MODEL), F32),
        grid=(n_tok // tm,),
        in_specs=[
            pl.BlockSpec((tm, D_MODEL), lambda i: (i, 0)),
            pl.BlockSpec((tm, DN_W), lambda i: (i, 0)),
            pl.BlockSpec((tm, POOL_W), lambda i: (i, 0)),
            pl.BlockSpec((None, 6, D_MODEL), _ada_row_map(seq_len, tm, per_seq)),
            pl.BlockSpec((1, D_MODEL), lambda i: (0, 0)),
            pl.BlockSpec((1, D_MODEL), lambda i: (0, 0)),
            const((D_MODEL, D_MODEL)),
            const((D_MODEL, 2 * D_FF)),
            const((D_FF, D_MODEL)),
        ],
        out_specs=pl.BlockSpec((tm, D_MODEL), lambda i: (i, 0)),
        compiler_params=pltpu.CompilerParams(vmem_limit_bytes=V7X_VMEM_LIMIT),
        name="post",
    )(x2, o2, p2, ada_l, norm_ffn.reshape(1, D_MODEL), final_norm.reshape(1, D_MODEL),
      w_out_bf, w_gu_bf, w_down_bf)


def _gate_row(p):
    return jnp.zeros((1, HEAD_W), F32).at[0, 2 * DN_HEADS:4 * DN_HEADS].set(p.reshape(-1))


def _stream_layer(x2, seq_len, rows, per_seq, ada_l, lw, state_delta, layer, final):
    n_seq = x2.shape[0] // seq_len
    proj = _inproj_call(x2, ada_l, lw["norm_mix"], lw["w_in"], seq_len, per_seq)
    proj3 = proj.reshape(n_seq, seq_len, PROJ_W)
    o, st = _dn_call(proj3, lw["conv_w"], lw["alog_row"], lw["dtb_row"], lw["dn_norm"],
                     state_delta, layer, out_state=state_delta is None)
    p = _pool_call(proj3, lw["pool_w"], lw["pool_scale"], rows)
    x2 = _post_call(x2, o.reshape(-1, DN_W), p.reshape(-1, POOL_W), ada_l, lw["norm_ffn"], lw["final_norm"],
                    lw["w_out"], lw["w_gu"], lw["w_down"], seq_len, per_seq, final)
    return x2, st


def kernel(x_prompt, x_sample, c, state_delta, c_ctx, w_ada, b_ada, norm_mix, norm_ffn, w_in, conv_w, a_log, dt_bias, dn_norm, pool_w, pool_scale, w_out, w_gu, w_down, final_norm):
    depth = w_ada.shape[0]
    n_ctx, ctx_len, _ = x_prompt.shape
    n_lat, lat_len, _ = x_sample.shape
    assert 1 + n_lat <= ADA_ROWS
    c_all = jnp.concatenate([c_ctx[None, :], c, jnp.zeros((ADA_ROWS - 1 - n_lat, D_MODEL), F32)], axis=0)
    ada = _ada_call(c_all, w_ada, b_ada).reshape(depth, ADA_ROWS, 6, D_MODEL)

    qkvz = 4 * DN_W
    n_gate = 4 * DN_HEADS
    xp = x_prompt.reshape(-1, D_MODEL)
    xs = x_sample.reshape(-1, D_MODEL)
    states = []
    for l in range(depth):
        w_in_p = jnp.concatenate(
            [w_in[l][:, :qkvz], w_in[l][:, qkvz + n_gate:], w_in[l][:, qkvz:qkvz + n_gate],
             jnp.zeros((D_MODEL, HEAD_W - n_gate), F32)], axis=1).astype(BF16)
        lw = dict(norm_mix=norm_mix[l], norm_ffn=norm_ffn[l], w_in=w_in_p, conv_w=conv_w[l],
                  alog_row=_gate_row(a_log[l]), dtb_row=_gate_row(dt_bias[l]), dn_norm=dn_norm[l],
                  pool_w=pool_w[l].astype(BF16), pool_scale=pool_scale[l], w_out=w_out[l].astype(BF16),
                  w_gu=w_gu[l].astype(BF16), w_down=w_down[l].astype(BF16), final_norm=final_norm)
        final = l == depth - 1
        xp, st = _stream_layer(xp, ctx_len, None, False, ada[l], lw, None, l, final)
        states.append(st)
        xs, _ = _stream_layer(xs, lat_len, lat_len // GRID_W, True, ada[l], lw, state_delta, l, final)
    new_state = jnp.stack(states, axis=1)
    return (xp.reshape(x_prompt.shape), xs.reshape(x_sample.shape), new_state)
```

```python
import functools

import jax
import jax.numpy as jnp
from jax import lax
from jax.experimental import pallas as pl
from jax.experimental.pallas import tpu as pltpu

F32 = jnp.float32
BF16 = jnp.bfloat16

D_MODEL = 1024
DN_HEADS = 4
HEAD_W = 128
DN_W = DN_HEADS * HEAD_W
POOL_GROUPS = 4
POOL_W = POOL_GROUPS * HEAD_W
POOL_WINDOWS = (2, 4, 8, 16)
GRID_W = 64
CONV_K = 5
CHUNK = 64
D_FF = 2816
EPS = 1e-6

PROJ_W = 3 * DN_W + DN_W + POOL_W + HEAD_W
POOL_BLK512 = 4
GATE_BLK = 20
ADA_ROWS = 16

V7X_VMEM_LIMIT = 56 * 1024 * 1024

TM_INPROJ = 256
TM_POST = 256
POOL_TB = 256
PREP_UNROLL = 4
DN_HEADS_PER_STEP = 2


def _sigmoid(x):
    return 1.0 / (1.0 + jnp.exp(-x))


def _silu(x):
    return x * _sigmoid(x)


def _softplus(x):
    return jnp.maximum(x, 0.0) + jnp.log(1.0 + jnp.exp(-jnp.abs(x)))


def _bdot(a, b):
    return jnp.dot(a.astype(BF16), b.astype(BF16), preferred_element_type=F32)


def _iota(shape, dim):
    return lax.broadcasted_iota(jnp.int32, shape, dim)


def _ada_kernel(c_ref, w_ref, b_ref, o_ref):
    o_ref[...] = jnp.dot(_silu(c_ref[...]), w_ref[...], preferred_element_type=F32,
                         precision=lax.Precision.HIGHEST) + b_ref[...]


def _ada_call(c_all, w_ada, b_ada):
    depth, _, n_out = w_ada.shape
    tn = 1536
    return pl.pallas_call(
        _ada_kernel,
        out_shape=jax.ShapeDtypeStruct((depth, ADA_ROWS, n_out), F32),
        grid=(depth, n_out // tn),
        in_specs=[
            pl.BlockSpec((ADA_ROWS, D_MODEL), lambda l, j: (0, 0)),
            pl.BlockSpec((None, D_MODEL, tn), lambda l, j: (l, 0, j)),
            pl.BlockSpec((None, 1, tn), lambda l, j: (l, 0, j)),
        ],
        out_specs=pl.BlockSpec((None, ADA_ROWS, tn), lambda l, j: (l, 0, j)),
        compiler_params=pltpu.CompilerParams(vmem_limit_bytes=V7X_VMEM_LIMIT),
        name="ada",
    )(c_all, w_ada, b_ada.reshape(depth, 1, n_out))


def _modulated_norm(x, gain, shift, scale):
    y = x * lax.rsqrt(jnp.mean(x * x, axis=-1, keepdims=True) + EPS) * gain
    return y * (1.0 + scale) + shift


def _inproj_kernel(x_ref, ada_ref, nw_ref, w_ref, o_ref):
    h = _modulated_norm(x_ref[...], nw_ref[...], ada_ref[0:1, :], ada_ref[1:2, :])
    o_ref[...] = jnp.dot(h.astype(BF16), w_ref[...], preferred_element_type=F32)


def _ada_row_map(seq_len, tm, per_seq):
    if not per_seq:
        return lambda i: (0, 0, 0)
    return lambda i: (1 + (i * tm) // seq_len, 0, 0)


def _inproj_call(x2, ada_l, norm_w, w_in_p, seq_len, per_seq):
    n_tok = x2.shape[0]
    tm = TM_INPROJ
    return pl.pallas_call(
        _inproj_kernel,
        out_shape=jax.ShapeDtypeStruct((n_tok, PROJ_W), F32),
        grid=(n_tok // tm,),
        in_specs=[
            pl.BlockSpec((tm, D_MODEL), lambda i: (i, 0)),
            pl.BlockSpec((None, 6, D_MODEL), _ada_row_map(seq_len, tm, per_seq)),
            pl.BlockSpec((1, D_MODEL), lambda i: (0, 0)),
            pl.BlockSpec((D_MODEL, PROJ_W), lambda i: (0, 0), pipeline_mode=pl.Buffered(1)),
        ],
        out_specs=pl.BlockSpec((tm, PROJ_W), lambda i: (i, 0)),
        compiler_params=pltpu.CompilerParams(vmem_limit_bytes=V7X_VMEM_LIMIT),
        name="inproj",
    )(x2, ada_l, norm_w.reshape(1, D_MODEL), w_in_p)


def _col_bcast(x, lane, idx):
    col = jnp.sum(jnp.where(lane == idx, x, 0.0), axis=-1, keepdims=True)
    return jnp.broadcast_to(col, x.shape)


def _each(fn, *lists):
    return [fn(*args) for args in zip(*lists)]


def _unit_tri_inverse(lbds, eye, blk16):
    d = _each(lambda x: jnp.where(blk16, x, 0.0), lbds)
    lo = _each(lambda x: jnp.where(blk16, 0.0, x), lbds)
    d2 = _each(_bdot, d, d)
    d3 = _each(_bdot, d, d2)
    d4 = _each(_bdot, d2, d2)
    d8 = _each(_bdot, d4, d4)
    p1 = _each(lambda a, b, c: eye - a + b - c, d, d2, d3)
    p2 = _each(lambda p, x: p + _bdot(p, x), p1, d4)
    dinv = _each(lambda p, x: p + _bdot(p, x), p2, d8)
    m = _each(_bdot, dinv, lo)
    m2 = _each(_bdot, m, m)
    m3 = _each(_bdot, m, m2)
    ninv = _each(lambda a, b, c: eye - a + b - c, m, m2, m3)
    return _each(_bdot, ninv, dinv)


def _dn_kernel(*refs, seq_len, hp, has_s0, out_state):
    n_ch = seq_len // CHUNK
    unroll = min(PREP_UNROLL, n_ch)
    it = iter(refs)
    q_ref, k_ref, v_ref, z_ref, g_ref = (next(it) for _ in range(5))
    cwq_ref, cwk_ref, cwv_ref = (next(it) for _ in range(3))
    alog_ref, dtb_ref, dnn_ref = (next(it) for _ in range(3))
    s0_ref = next(it) if has_s0 else None
    o_ref = next(it)
    st_ref = next(it) if out_state else None
    wq_s, u_s, kdt_s, dl_s, at_s, vn_s, qs_s, s_s, ext_s = (next(it) for _ in range(9))

    head0 = pl.program_id(1) * hp
    row = _iota((CHUNK, 128), 0)
    lane = _iota((CHUNK, 128), 1)
    is_f = lane < CHUNK
    j = lane & (CHUNK - 1)
    incl = (is_f & (row >= j)) | (~is_f & (row <= j))
    strict = (is_f & (row > j)) | (~is_f & (row < j))
    diag = row == j
    r128 = _iota((128, 128), 0)
    c128 = _iota((128, 128), 1)
    eye = jnp.where(r128 == c128, 1.0, 0.0).astype(F32)
    blk16 = (r128 >> 4) == (c128 >> 4)
    lane_f128 = c128 < CHUNK
    cum_sel = ((r128 < CHUNK) & (c128 <= r128)) | ((r128 >= CHUNK) & (c128 >= r128 - CHUNK))
    cum_mat = jnp.where(cum_sel, 1.0, 0.0).astype(F32)[:, :CHUNK]

    neg_a = -jnp.exp(alog_ref[...])
    dtb = dtb_ref[...]

    def conv_silu(ref, cw_ref, cols, c, r0, slot):
        p0 = pl.multiple_of(jnp.maximum(r0 - 8, 0), 8)
        n0 = pl.multiple_of(jnp.minimum(r0 + CHUNK, seq_len - 8), 8)
        ext_s[slot, 0:8, :] = jnp.where(c > 0, ref[pl.ds(p0, 8), cols], 0.0)
        ext_s[slot, 8:8 + CHUNK, :] = ref[pl.ds(r0, CHUNK), cols]
        ext_s[slot, 8 + CHUNK:16 + CHUNK, :] = jnp.where(c < n_ch - 1, ref[pl.ds(n0, 8), cols], 0.0)
        acc = ext_s[slot, 6:6 + CHUNK, :] * cw_ref[0:1, cols]
        for t in range(1, CONV_K):
            acc = acc + ext_s[slot, 6 + t:6 + t + CHUNK, :] * cw_ref[t:t + 1, cols]
        return _silu(acc)

    def l2n(x):
        return x * lax.rsqrt(jnp.sum(x * x, axis=-1, keepdims=True) + EPS)

    def gates(r0):
        ab = g_ref[pl.ds(r0, CHUNK), :]
        gact = jnp.where(lane < 2 * DN_HEADS, _sigmoid(ab), neg_a * _softplus(ab + dtb))
        return gact, jnp.dot(cum_mat, gact, preferred_element_type=F32, precision=lax.Precision.HIGHEST)

    def prep_body(i, carry):
        cs = [i * unroll + t for t in range(unroll)]
        r0s = [pl.multiple_of(c * CHUNK, CHUNK) for c in cs]
        gact, cums = zip(*[gates(r0) for r0 in r0s])
        units = [(t, hh) for t in range(unroll) for hh in range(hp)]
        cols = lambda hh: slice(hh * HEAD_W, (hh + 1) * HEAD_W)
        nu = len(units)
        q = [l2n(conv_silu(q_ref, cwq_ref, cols(hh), cs[t], r0s[t], n)) * (HEAD_W ** -0.5)
             for n, (t, hh) in enumerate(units)]
        k = [l2n(conv_silu(k_ref, cwk_ref, cols(hh), cs[t], r0s[t], nu + n)) for n, (t, hh) in enumerate(units)]
        v = [conv_silu(v_ref, cwv_ref, cols(hh), cs[t], r0s[t], 2 * nu + n) for n, (t, hh) in enumerate(units)]
        beta_f = [_col_bcast(gact[t], lane, head0 + hh) for t, hh in units]
        beta_b = [_col_bcast(gact[t], lane, DN_HEADS + head0 + hh) for t, hh in units]
        gc_f = [_col_bcast(cums[t][:CHUNK], lane, 2 * DN_HEADS + head0 + hh) for t, hh in units]
        gc_b = [_col_bcast(cums[t][CHUNK:], lane, 3 * DN_HEADS + head0 + hh) for t, hh in units]

        def decay_of(gf, gb):
            gc_col = jnp.where(is_f, gf, gb)
            gc_row = jnp.sum(jnp.where(diag, gc_col, 0.0), axis=0, keepdims=True)
            return jnp.where(incl, jnp.exp(jnp.where(incl, gc_col - gc_row, 0.0)), 0.0)

        decay = _each(decay_of, gc_f, gc_b)

        def gram_of(qq, kk):
            kb = kk.astype(BF16)
            return lax.dot_general(jnp.concatenate([qq.astype(BF16), kb], axis=0),
                                   jnp.concatenate([kb, kb], axis=0),
                                   (((1,), (1,)), ((), ())), preferred_element_type=F32)

        gram = _each(gram_of, q, k)

        def lbd_of(g, dec, bf, bb):
            l2 = jnp.where(strict, jnp.where(is_f, bf, bb) * g[CHUNK:] * dec, 0.0)
            return jnp.concatenate([jnp.where(is_f, l2, 0.0), jnp.where(is_f, 0.0, l2)], axis=0)

        t_inv = _unit_tri_inverse(_each(lbd_of, gram, decay, beta_f, beta_b), eye, blk16)
        e_f = _each(jnp.exp, gc_f)
        e_b = _each(jnp.exp, gc_b)

        def rhs_of(kk, vv, bf, bb, ef, eb):
            return jnp.concatenate([
                jnp.concatenate([vv * bf, kk * bf * ef], axis=1),
                jnp.concatenate([vv * bb, kk * bb * eb], axis=1)], axis=0)

        uw = _each(_bdot, t_inv, _each(rhs_of, k, v, beta_f, beta_b, e_f, e_b))
        for n, (t, hh) in enumerate(units):
            c = cs[t]
            u_s[hh, 0, c] = uw[n][:CHUNK, :HEAD_W]
            u_s[hh, 1, c] = uw[n][CHUNK:, :HEAD_W]
            wq_s[hh, 0, c] = jnp.concatenate([uw[n][:CHUNK, HEAD_W:], q[n] * e_f[n]], axis=0).astype(BF16)
            wq_s[hh, 1, c] = jnp.concatenate([uw[n][CHUNK:, HEAD_W:], q[n] * e_b[n]], axis=0).astype(BF16)
            gl_f = gc_f[n][CHUNK - 1:CHUNK, :]
            gl_b = gc_b[n][0:1, :]
            kdec = jnp.concatenate([k[n] * jnp.exp(gl_f - gc_f[n]), k[n] * jnp.exp(gl_b - gc_b[n])], axis=0)
            kdt_s[hh, c] = kdec.T.astype(BF16)
            dl_s[hh, 0, c] = jnp.broadcast_to(jnp.exp(gl_f), (8, 128))
            dl_s[hh, 1, c] = jnp.broadcast_to(jnp.exp(gl_b), (8, 128))
            at_s[hh, c] = (gram[n][:CHUNK] * decay[n]).astype(BF16)
        return carry

    lax.fori_loop(0, n_ch // unroll, prep_body, 0)

    for hh in range(hp):
        for d in range(2):
            s_s[hh, d] = s0_ref[d, hh] if has_s0 else jnp.zeros((HEAD_W, HEAD_W), F32)

    def scan_dir(hh, d, c, first):
        s = s_s[hh, d]
        r = jnp.dot(wq_s[hh, d, c], s.astype(BF16), preferred_element_type=F32)
        vn = (u_s[hh, d, c] - r[:CHUNK]).astype(BF16)
        vn_s[hh, d, c] = vn
        qs_s[hh, c] = r[CHUNK:] if first else qs_s[hh, c] + r[CHUNK:]
        kdt = kdt_s[hh, c]
        zero = jnp.zeros_like(kdt)
        kdt = jnp.where(lane_f128, kdt, zero) if d == 0 else jnp.where(lane_f128, zero, kdt)
        upd = jnp.dot(kdt, jnp.concatenate([vn, vn], axis=0), preferred_element_type=F32)
        s_s[hh, d] = s * dl_s[hh, d, c][0:1, :] + upd

    def scan_body(first, i, carry):
        for hh in range(hp):
            scan_dir(hh, 0, i, first)
            scan_dir(hh, 1, n_ch - 1 - i, first)
        return carry

    lax.fori_loop(0, n_ch // 2, functools.partial(scan_body, True), 0)
    lax.fori_loop(n_ch // 2, n_ch, functools.partial(scan_body, False), 0)
    if out_state:
        for hh in range(hp):
            for d in range(2):
                st_ref[d, hh] = s_s[hh, d]

    dnn = dnn_ref[...]

    def out_chunk(c):
        r0 = pl.multiple_of(c * CHUNK, CHUNK)
        for hh in range(hp):
            cols = slice(hh * HEAD_W, (hh + 1) * HEAD_W)
            vst = jnp.concatenate([vn_s[hh, 0, c], vn_s[hh, 1, c]], axis=0)
            o = qs_s[hh, c] + jnp.dot(at_s[hh, c], vst, preferred_element_type=F32)
            y = o * lax.rsqrt(jnp.mean(o * o, axis=-1, keepdims=True) + EPS) * dnn
            o_ref[pl.ds(r0, CHUNK), cols] = (y * _silu(z_ref[pl.ds(r0, CHUNK), cols])).astype(o_ref.dtype)

    def out_body(i, carry):
        for t in range(unroll):
            out_chunk(i * unroll + t)
        return carry

    lax.fori_loop(0, n_ch // unroll, out_body, 0)


def _dn_call(proj3, conv_w, alog_row, dtb_row, dn_norm, state_delta, layer, out_state):
    n_seq, seq_len, _ = proj3.shape
    n_ch = seq_len // CHUNK
    hp = DN_HEADS_PER_STEP
    wid = hp * HEAD_W
    nblk = DN_HEADS // hp
    has_s0 = state_delta is not None
    tile = lambda off: pl.BlockSpec((None, seq_len, wid), lambda b, h: (b, 0, off + h))
    cw = lambda off: pl.BlockSpec((CONV_K, wid), lambda b, h: (0, off + h))
    row = pl.BlockSpec((1, HEAD_W), lambda b, h: (0, 0))
    in_specs = [tile(0), tile(nblk), tile(2 * nblk), tile(3 * nblk),
                pl.BlockSpec((None, seq_len, HEAD_W), lambda b, h: (b, 0, GATE_BLK)),
                cw(0), cw(nblk), cw(2 * nblk), row, row, row]
    args = [proj3, proj3, proj3, proj3, proj3, conv_w, conv_w, conv_w, alog_row, dtb_row,
            dn_norm.reshape(1, HEAD_W)]
    if has_s0:
        in_specs.append(pl.BlockSpec((None, None, 2, hp, HEAD_W, HEAD_W),
                                     lambda b, h: (b, layer, 0, h, 0, 0)))
        args.append(state_delta)
    out_shape = [jax.ShapeDtypeStruct((n_seq, seq_len, DN_W), BF16)]
    out_specs = [pl.BlockSpec((None, seq_len, wid), lambda b, h: (b, 0, h))]
    if out_state:
        out_shape.append(jax.ShapeDtypeStruct((n_seq, 2, DN_HEADS, HEAD_W, HEAD_W), F32))
        out_specs.append(pl.BlockSpec((None, 2, hp, HEAD_W, HEAD_W), lambda b, h: (b, 0, h, 0, 0)))
    scratch = [
        pltpu.VMEM((hp, 2, n_ch, 2 * CHUNK, HEAD_W), BF16),
        pltpu.VMEM((hp, 2, n_ch, CHUNK, HEAD_W), F32),
        pltpu.VMEM((hp, n_ch, HEAD_W, 2 * CHUNK), BF16),
        pltpu.VMEM((hp, 2, n_ch, 8, HEAD_W), F32),
        pltpu.VMEM((hp, n_ch, CHUNK, 2 * CHUNK), BF16),
        pltpu.VMEM((hp, 2, n_ch, CHUNK, HEAD_W), BF16),
        pltpu.VMEM((hp, n_ch, CHUNK, HEAD_W), F32),
        pltpu.VMEM((hp, 2, HEAD_W, HEAD_W), F32),
        pltpu.VMEM((3 * hp * min(PREP_UNROLL, n_ch), CHUNK + 16, HEAD_W), F32),
    ]
    outs = pl.pallas_call(
        functools.partial(_dn_kernel, seq_len=seq_len, hp=hp, has_s0=has_s0, out_state=out_state),
        out_shape=out_shape,
        grid=(n_seq, nblk),
        in_specs=in_specs,
        out_specs=out_specs,
        scratch_shapes=scratch,
        compiler_params=pltpu.CompilerParams(vmem_limit_bytes=V7X_VMEM_LIMIT),
        name="deltanet",
    )(*args)
    return (outs[0], outs[1]) if out_state else (outs[0], None)


def _split3(x):
    hi = x.astype(BF16)
    r1 = x - hi.astype(F32)
    mid = r1.astype(BF16)
    lo = (r1 - mid.astype(F32)).astype(BF16)
    return hi, mid, lo


def _window(pos, win, n):
    lo = jnp.clip(pos - win // 2, 0, n)
    hi = jnp.clip(pos - win // 2 + win, 0, n)
    return lo, hi


def _pool_kernel(u_ref, pw_ref, ps_ref, o_ref, m_s, *, seq_len, rows):
    period = GRID_W if rows else seq_len
    tb = min(POOL_TB, seq_len)
    r_i = _iota((tb, tb), 0)
    c_i = _iota((tb, tb), 1)
    shift = period.bit_length() - 1
    same_line = (r_i >> shift) == (c_i >> shift)
    pos_r = r_i & (period - 1)
    pos_c = c_i & (period - 1)
    pos_col = _iota((tb, HEAD_W), 0) & (period - 1)
    for g, win in enumerate(POOL_WINDOWS):
        cols = slice(g * HEAD_W, (g + 1) * HEAD_W)
        if rows:
            run = None
            prev_lo = prev_hi = 0
            for r in range(rows):
                lo, hi = max(r - win // 2, 0), min(r - win // 2 + win, rows)
                for a in range(prev_hi, hi):
                    slab = u_ref[a * GRID_W:(a + 1) * GRID_W, cols]
                    run = slab if run is None else run + slab
                for a in range(prev_lo, lo):
                    run = run - u_ref[a * GRID_W:(a + 1) * GRID_W, cols]
                prev_lo, prev_hi = lo, hi
                m_s[r * GRID_W:(r + 1) * GRID_W, :] = run / float(hi - lo)
        lo_r, hi_r = _window(pos_r, win, period)
        band = jnp.where(same_line & (pos_c >= lo_r) & (pos_c < hi_r), 1.0, 0.0).astype(BF16)
        lo_c, hi_c = _window(pos_col, win, period)
        cnt = (hi_c - lo_c).astype(F32)
        pw = pw_ref[g]
        scale = ps_ref[:, cols]
        for t0 in range(0, seq_len, tb):
            ug = u_ref[t0:t0 + tb, cols]
            src = m_s[t0:t0 + tb, :] if rows else ug
            hi3, mid3, lo3 = _split3(src)
            box = (jnp.dot(band, hi3, preferred_element_type=F32)
                   + jnp.dot(band, mid3, preferred_element_type=F32)
                   + jnp.dot(band, lo3, preferred_element_type=F32))
            mean = box / cnt
            o_ref[t0:t0 + tb, cols] = (_bdot(mean - ug, pw) * scale).astype(o_ref.dtype)


def _pool_call(proj3, pool_w_bf, pool_scale, rows):
    n_seq, seq_len, _ = proj3.shape
    return pl.pallas_call(
        functools.partial(_pool_kernel, seq_len=seq_len, rows=rows),
        out_shape=jax.ShapeDtypeStruct((n_seq, seq_len, POOL_W), BF16),
        grid=(n_seq,),
        in_specs=[
            pl.BlockSpec((None, seq_len, POOL_W), lambda b: (b, 0, POOL_BLK512)),
            pl.BlockSpec((POOL_GROUPS, HEAD_W, HEAD_W), lambda b: (0, 0, 0)),
            pl.BlockSpec((1, POOL_W), lambda b: (0, 0)),
        ],
        out_specs=pl.BlockSpec((None, seq_len, POOL_W), lambda b: (b, 0, 0)),
        scratch_shapes=[pltpu.VMEM((seq_len, HEAD_W), F32)],
        compiler_params=pltpu.CompilerParams(vmem_limit_bytes=V7X_VMEM_LIMIT),
        name="pool",
    )(proj3, pool_w_bf, pool_scale.reshape(1, POOL_W))


def _post_kernel(x_ref, o_ref, p_ref, ada_ref, nf_ref, fin_ref, wo_ref, wgu_ref, wd_ref, y_ref, *, final):
    mix = (jnp.dot(o_ref[...], wo_ref[:DN_W, :], preferred_element_type=F32)
           + jnp.dot(p_ref[...], wo_ref[DN_W:, :], preferred_element_type=F32))
    x = x_ref[...] + ada_ref[2:3, :] * mix
    h = _modulated_norm(x, nf_ref[...], ada_ref[3:4, :], ada_ref[4:5, :])
    gu = jnp.dot(h.astype(BF16), wgu_ref[...], preferred_element_type=F32)
    act = _silu(gu[:, :D_FF]) * gu[:, D_FF:]
    x = x + ada_ref[5:6, :] * jnp.dot(act.astype(BF16), wd_ref[...], preferred_element_type=F32)
    if final:
        x = x * lax.rsqrt(jnp.mean(x * x, axis=-1, keepdims=True) + EPS) * fin_ref[...]
    y_ref[...] = x


def _post_call(x2, o2, p2, ada_l, norm_ffn, final_norm, w_out_bf, w_gu_bf, w_down_bf, seq_len, per_seq, final):
    n_tok = x2.shape[0]
    tm = TM_POST
    const = lambda shape: pl.BlockSpec(shape, lambda i: (0, 0), pipeline_mode=pl.Buffered(1))
    return pl.pallas_call(
        functools.partial(_post_kernel, final=final),
        out_shape=jax.ShapeDtypeStruct((n_tok, D_MODEL), F32),
        grid=(n_tok // tm,),
        in_specs=[
            pl.BlockSpec((tm, D_MODEL), lambda i: (i, 0)),
            pl.BlockSpec((tm, DN_W), lambda i: (i, 0)),
            pl.BlockSpec((tm, POOL_W), lambda i: (i, 0)),
            pl.BlockSpec((None, 6, D_MODEL), _ada_row_map(seq_len, tm, per_seq)),
            pl.BlockSpec((1, D_MODEL), lambda i: (0, 0)),
            pl.BlockSpec((1, D_MODEL), lambda i: (0, 0)),
            const((D_MODEL, D_MODEL)),
            const((D_MODEL, 2 * D_FF)),
            const((D_FF, D_MODEL)),
        ],
        out_specs=pl.BlockSpec((tm, D_MODEL), lambda i: (i, 0)),
        compiler_params=pltpu.CompilerParams(vmem_limit_bytes=V7X_VMEM_LIMIT),
        name="post",
    )(x2, o2, p2, ada_l, norm_ffn.reshape(1, D_MODEL), final_norm.reshape(1, D_MODEL),
      w_out_bf, w_gu_bf, w_down_bf)


def _gate_row(p):
    return jnp.zeros((1, HEAD_W), F32).at[0, 2 * DN_HEADS:4 * DN_HEADS].set(p.reshape(-1))


def _stream_layer(x2, seq_len, rows, per_seq, ada_l, lw, state_delta, layer, final):
    n_seq = x2.shape[0] // seq_len
    proj = _inproj_call(x2, ada_l, lw["norm_mix"], lw["w_in"], seq_len, per_seq)
    proj3 = proj.reshape(n_seq, seq_len, PROJ_W)
    o, st = _dn_call(proj3, lw["conv_w"], lw["alog_row"], lw["dtb_row"], lw["dn_norm"],
                     state_delta, layer, out_state=state_delta is None)
    p = _pool_call(proj3, lw["pool_w"], lw["pool_scale"], rows)
    x2 = _post_call(x2, o.reshape(-1, DN_W), p.reshape(-1, POOL_W), ada_l, lw["norm_ffn"], lw["final_norm"],
                    lw["w_out"], lw["w_gu"], lw["w_down"], seq_len, per_seq, final)
    return x2, st


def kernel(x_prompt, x_sample, c, state_delta, c_ctx, w_ada, b_ada, norm_mix, norm_ffn, w_in, conv_w, a_log, dt_bias, dn_norm, pool_w, pool_scale, w_out, w_gu, w_down, final_norm):
    depth = w_ada.shape[0]
    n_ctx, ctx_len, _ = x_prompt.shape
    n_lat, lat_len, _ = x_sample.shape
    assert 1 + n_lat <= ADA_ROWS
    c_all = jnp.concatenate([c_ctx[None, :], c, jnp.zeros((ADA_ROWS - 1 - n_lat, D_MODEL), F32)], axis=0)
    ada = _ada_call(c_all, w_ada, b_ada).reshape(depth, ADA_ROWS, 6, D_MODEL)

    qkvz = 4 * DN_W
    n_gate = 4 * DN_HEADS
    xp = x_prompt.reshape(-1, D_MODEL)
    xs = x_sample.reshape(-1, D_MODEL)
    states = []
    for l in range(depth):
        w_in_p = jnp.concatenate(
            [w_in[l][:, :qkvz], w_in[l][:, qkvz + n_gate:], w_in[l][:, qkvz:qkvz + n_gate],
             jnp.zeros((D_MODEL, HEAD_W - n_gate), F32)], axis=1).astype(BF16)
        lw = dict(norm_mix=norm_mix[l], norm_ffn=norm_ffn[l], w_in=w_in_p, conv_w=conv_w[l],
                  alog_row=_gate_row(a_log[l]), dtb_row=_gate_row(dt_bias[l]), dn_norm=dn_norm[l],
                  pool_w=pool_w[l].astype(BF16), pool_scale=pool_scale[l], w_out=w_out[l].astype(BF16),
                  w_gu=w_gu[l].astype(BF16), w_down=w_down[l].astype(BF16), final_norm=final_norm)
        final = l == depth - 1
        xp, st = _stream_layer(xp, ctx_len, None, False, ada[l], lw, None, l, final)
        states.append(st)
        xs, _ = _stream_layer(xs, lat_len, lat_len // GRID_W, True, ada[l], lw, state_delta, l, final)
    new_state = jnp.stack(states, axis=1)
    return (xp.reshape(x_prompt.shape), xs.reshape(x_sample.shape), new_state)
```

```python
import functools

import jax
import jax.numpy as jnp
from jax import lax
from jax.experimental import pallas as pl
from jax.experimental.pallas import tpu as pltpu

F32 = jnp.float32
BF16 = jnp.bfloat16

D_MODEL = 1024
DN_HEADS = 4
HEAD_W = 128
DN_W = DN_HEADS * HEAD_W
POOL_GROUPS = 4
POOL_W = POOL_GROUPS * HEAD_W
POOL_WINDOWS = (2, 4, 8, 16)
GRID_W = 64
CONV_K = 5
CHUNK = 64
D_FF = 2816
EPS = 1e-6

PROJ_W = 3 * DN_W + DN_W + POOL_W + HEAD_W
POOL_BLK512 = 4
GATE_BLK = 20
ADA_ROWS = 16

V7X_VMEM_LIMIT = 56 * 1024 * 1024

TM_INPROJ = 256
TM_POST = 256
POOL_TB = 256
PREP_UNROLL = 8
DN_HEADS_PER_STEP = 2
DN_CHUNKS_PER_STEP = 32


def _sigmoid(x):
    return 1.0 / (1.0 + jnp.exp(-x))


def _silu(x):
    return x * _sigmoid(x)


def _softplus(x):
    return jnp.maximum(x, 0.0) + jnp.log(1.0 + jnp.exp(-jnp.abs(x)))


def _bdot(a, b):
    return jnp.dot(a.astype(BF16), b.astype(BF16), preferred_element_type=F32)


def _iota(shape, dim):
    return lax.broadcasted_iota(jnp.int32, shape, dim)


def _ada_kernel(c_ref, w_ref, b_ref, o_ref):
    o_ref[...] = jnp.dot(_silu(c_ref[...]), w_ref[...], preferred_element_type=F32,
                         precision=lax.Precision.HIGHEST) + b_ref[...]


def _ada_call(c_all, w_ada, b_ada):
    depth, _, n_out = w_ada.shape
    tn = 1536
    return pl.pallas_call(
        _ada_kernel,
        out_shape=jax.ShapeDtypeStruct((depth, ADA_ROWS, n_out), F32),
        grid=(depth, n_out // tn),
        in_specs=[
            pl.BlockSpec((ADA_ROWS, D_MODEL), lambda l, j: (0, 0)),
            pl.BlockSpec((None, D_MODEL, tn), lambda l, j: (l, 0, j)),
            pl.BlockSpec((None, 1, tn), lambda l, j: (l, 0, j)),
        ],
        out_specs=pl.BlockSpec((None, ADA_ROWS, tn), lambda l, j: (l, 0, j)),
        compiler_params=pltpu.CompilerParams(vmem_limit_bytes=V7X_VMEM_LIMIT),
        name="ada",
    )(c_all, w_ada, b_ada.reshape(depth, 1, n_out))


def _modulated_norm(x, gain, shift, scale):
    y = x * lax.rsqrt(jnp.mean(x * x, axis=-1, keepdims=True) + EPS) * gain
    return y * (1.0 + scale) + shift


def _inproj_kernel(x_ref, ada_ref, nw_ref, w_ref, o_ref):
    h = _modulated_norm(x_ref[...], nw_ref[...], ada_ref[0:1, :], ada_ref[1:2, :])
    o_ref[...] = jnp.dot(h.astype(BF16), w_ref[...], preferred_element_type=F32)


def _ada_row_map(seq_len, tm, per_seq):
    if not per_seq:
        return lambda i: (0, 0, 0)
    return lambda i: (1 + (i * tm) // seq_len, 0, 0)


def _inproj_call(x2, ada_l, norm_w, w_in_p, seq_len, per_seq):
    n_tok = x2.shape[0]
    tm = TM_INPROJ
    return pl.pallas_call(
        _inproj_kernel,
        out_shape=jax.ShapeDtypeStruct((n_tok, PROJ_W), F32),
        grid=(n_tok // tm,),
        in_specs=[
            pl.BlockSpec((tm, D_MODEL), lambda i: (i, 0)),
            pl.BlockSpec((None, 6, D_MODEL), _ada_row_map(seq_len, tm, per_seq)),
            pl.BlockSpec((1, D_MODEL), lambda i: (0, 0)),
            pl.BlockSpec((D_MODEL, PROJ_W), lambda i: (0, 0), pipeline_mode=pl.Buffered(1)),
        ],
        out_specs=pl.BlockSpec((tm, PROJ_W), lambda i: (i, 0)),
        compiler_params=pltpu.CompilerParams(vmem_limit_bytes=V7X_VMEM_LIMIT),
        name="inproj",
    )(x2, ada_l, norm_w.reshape(1, D_MODEL), w_in_p)


def _col_bcast(x, lane, idx):
    col = jnp.sum(jnp.where(lane == idx, x, 0.0), axis=-1, keepdims=True)
    return jnp.broadcast_to(col, x.shape)


def _each(fn, *lists):
    return [fn(*args) for args in zip(*lists)]


def _interleave(*gens):
    gens = list(gens)
    while gens:
        for g in list(gens):
            try:
                next(g)
            except StopIteration:
                gens.remove(g)


def _dn_kernel(*refs, seq_len, ns, hp, has_s0, out_state):
    n_ch = seq_len // CHUNK
    gc = min(PREP_UNROLL, ns * n_ch)
    n_grp = ns * n_ch // gc
    units = [(t, hh) for t in range(gc) for hh in range(hp)]
    nu = len(units)
    it = iter(refs)
    q_ref, k_ref, v_ref, z_ref, g_ref = (next(it) for _ in range(5))
    cwq_ref, cwk_ref, cwv_ref = (next(it) for _ in range(3))
    alog_ref, dtb_ref, dnn_ref = (next(it) for _ in range(3))
    s0_ref = next(it) if has_s0 else None
    o_ref = next(it)
    st_ref = next(it) if out_state else None
    wq_s, u_s, kdt_s, dl_s, at_s, vn_s, qs_s, s_s, ext_s, lbd_s, rhs_s = (next(it) for _ in range(11))

    head0 = pl.program_id(1) * hp
    row = _iota((CHUNK, 128), 0)
    lane = _iota((CHUNK, 128), 1)
    is_f = lane < CHUNK
    j = lane & (CHUNK - 1)
    incl = (is_f & (row >= j)) | (~is_f & (row <= j))
    strict = (is_f & (row > j)) | (~is_f & (row < j))
    diag = row == j
    r128 = _iota((128, 128), 0)
    c128 = _iota((128, 128), 1)
    eye = jnp.where(r128 == c128, 1.0, 0.0).astype(F32)
    blk16 = (r128 >> 4) == (c128 >> 4)
    lane_f128 = c128 < CHUNK
    cum_sel = ((r128 < CHUNK) & (c128 <= r128)) | ((r128 >= CHUNK) & (c128 >= r128 - CHUNK))
    cum_mat = jnp.where(cum_sel, 1.0, 0.0).astype(F32)[:, :CHUNK]

    neg_a = -jnp.exp(alog_ref[...])
    dtb = dtb_ref[...]
    cols = lambda hh: slice(hh * HEAD_W, (hh + 1) * HEAD_W)

    def group_chunks(g):
        ms = [g * gc + t for t in range(gc)]
        return ms, [m // n_ch for m in ms], [m % n_ch for m in ms]

    def conv_silu(ref, cw_ref, hh, s, c, r0, slot):
        p0 = pl.multiple_of(jnp.maximum(r0 - 8, 0), 8)
        n0 = pl.multiple_of(jnp.minimum(r0 + CHUNK, seq_len - 8), 8)
        ext_s[slot, 0:8, :] = jnp.where(c > 0, ref[s, pl.ds(p0, 8), cols(hh)], 0.0)
        ext_s[slot, 8:8 + CHUNK, :] = ref[s, pl.ds(r0, CHUNK), cols(hh)]
        ext_s[slot, 8 + CHUNK:16 + CHUNK, :] = jnp.where(c < n_ch - 1, ref[s, pl.ds(n0, 8), cols(hh)], 0.0)
        acc = ext_s[slot, 6:6 + CHUNK, :] * cw_ref[0:1, cols(hh)]
        for t in range(1, CONV_K):
            acc = acc + ext_s[slot, 6 + t:6 + t + CHUNK, :] * cw_ref[t:t + 1, cols(hh)]
        return _silu(acc)

    def l2n(x):
        return x * lax.rsqrt(jnp.sum(x * x, axis=-1, keepdims=True) + EPS)

    def gates(s, r0):
        ab = g_ref[s, pl.ds(r0, CHUNK), :]
        gact = jnp.where(lane < 2 * DN_HEADS, _sigmoid(ab), neg_a * _softplus(ab + dtb))
        return gact, jnp.dot(cum_mat, gact, preferred_element_type=F32, precision=lax.Precision.HIGHEST)

    def front(g, slot):
        ms, ss, cs = group_chunks(g)
        r0s = [pl.multiple_of(c * CHUNK, CHUNK) for c in cs]
        gact, cums = zip(*[gates(s, r0) for s, r0 in zip(ss, r0s)])
        yield
        q, k, v = [], [], []
        for n, (t, hh) in enumerate(units):
            q.append(l2n(conv_silu(q_ref, cwq_ref, hh, ss[t], cs[t], r0s[t], n)) * (HEAD_W ** -0.5))
            k.append(l2n(conv_silu(k_ref, cwk_ref, hh, ss[t], cs[t], r0s[t], nu + n)))
            v.append(conv_silu(v_ref, cwv_ref, hh, ss[t], cs[t], r0s[t], 2 * nu + n))
            if n % 2 == 1:
                yield
        beta_f = [_col_bcast(gact[t], lane, head0 + hh) for t, hh in units]
        beta_b = [_col_bcast(gact[t], lane, DN_HEADS + head0 + hh) for t, hh in units]
        gc_f = [_col_bcast(cums[t][:CHUNK], lane, 2 * DN_HEADS + head0 + hh) for t, hh in units]
        gc_b = [_col_bcast(cums[t][CHUNK:], lane, 3 * DN_HEADS + head0 + hh) for t, hh in units]
        yield

        def decay_of(gf, gb):
            gc_col = jnp.where(is_f, gf, gb)
            gc_row = jnp.sum(jnp.where(diag, gc_col, 0.0), axis=0, keepdims=True)
            return jnp.where(incl, jnp.exp(jnp.where(incl, gc_col - gc_row, 0.0)), 0.0)

        decay = _each(decay_of, gc_f, gc_b)
        yield

        def gram_of(qq, kk):
            kb = kk.astype(BF16)
            return lax.dot_general(jnp.concatenate([qq.astype(BF16), kb], axis=0),
                                   jnp.concatenate([kb, kb], axis=0),
                                   (((1,), (1,)), ((), ())), preferred_element_type=F32)

        gram = _each(gram_of, q, k)
        yield
        for n, (t, hh) in enumerate(units):
            l2 = jnp.where(strict, jnp.where(is_f, beta_f[n], beta_b[n]) * gram[n][CHUNK:] * decay[n], 0.0)
            lbd_s[slot, n] = jnp.concatenate([jnp.where(is_f, l2, 0.0), jnp.where(is_f, 0.0, l2)],
                                             axis=0).astype(BF16)
            at_s[hh, ms[t]] = (gram[n][:CHUNK] * decay[n]).astype(BF16)
        yield
        e_f = _each(jnp.exp, gc_f)
        e_b = _each(jnp.exp, gc_b)
        for n, (t, hh) in enumerate(units):
            rhs_s[slot, n] = jnp.concatenate([
                jnp.concatenate([v[n] * beta_f[n], k[n] * beta_f[n] * e_f[n]], axis=1),
                jnp.concatenate([v[n] * beta_b[n], k[n] * beta_b[n] * e_b[n]], axis=1)],
                axis=0).astype(BF16)
            wq_s[hh, 0, ms[t], CHUNK:, :] = (q[n] * e_f[n]).astype(BF16)
            wq_s[hh, 1, ms[t], CHUNK:, :] = (q[n] * e_b[n]).astype(BF16)
            if n % 4 == 3:
                yield
        for n, (t, hh) in enumerate(units):
            gl_f = gc_f[n][CHUNK - 1:CHUNK, :]
            gl_b = gc_b[n][0:1, :]
            kdec = jnp.concatenate([k[n] * jnp.exp(gl_f - gc_f[n]), k[n] * jnp.exp(gl_b - gc_b[n])], axis=0)
            kdt_s[hh, ms[t]] = kdec.T.astype(BF16)
            dl_s[hh, 0, ms[t]] = jnp.broadcast_to(jnp.exp(gl_f), (8, 128))
            dl_s[hh, 1, ms[t]] = jnp.broadcast_to(jnp.exp(gl_b), (8, 128))
            if n % 4 == 3:
                yield

    def back(g, slot):
        ms = [g * gc + t for t in range(gc)]
        lbd = [lbd_s[slot, n] for n in range(nu)]
        zero = jnp.zeros((128, 128), BF16)
        d = _each(lambda x: jnp.where(blk16, x, zero), lbd)
        lo = _each(lambda x: jnp.where(blk16, zero, x), lbd)
        d2 = _each(_bdot, d, d)
        yield
        d3 = _each(_bdot, d, d2)
        d4 = _each(_bdot, d2, d2)
        yield
        d8 = _each(_bdot, d4, d4)
        p1 = _each(lambda a, b, c: eye - a.astype(F32) + b - c, d, d2, d3)
        yield
        p2 = _each(lambda p, x: p + _bdot(p, x), p1, d4)
        yield
        dinv = _each(lambda p, x: p + _bdot(p, x), p2, d8)
        yield
        m = _each(_bdot, dinv, lo)
        yield
        m2 = _each(_bdot, m, m)
        yield
        m3 = _each(_bdot, m, m2)
        ninv = _each(lambda a, b, c: eye - a + b - c, m, m2, m3)
        yield
        t_inv = _each(_bdot, ninv, dinv)
        yield
        uw = [_bdot(t_inv[n], rhs_s[slot, n]) for n in range(nu)]
        yield
        for n, (t, hh) in enumerate(units):
            u_s[hh, 0, ms[t]] = uw[n][:CHUNK, :HEAD_W]
            u_s[hh, 1, ms[t]] = uw[n][CHUNK:, :HEAD_W]
            wq_s[hh, 0, ms[t], :CHUNK, :] = uw[n][:CHUNK, HEAD_W:].astype(BF16)
            wq_s[hh, 1, ms[t], :CHUNK, :] = uw[n][CHUNK:, HEAD_W:].astype(BF16)

    _interleave(front(0, 0))

    def prep_body(i, carry):
        slot = i & 1
        _interleave(back(i, slot), front(i + 1, 1 - slot))
        return carry

    lax.fori_loop(0, n_grp - 1, prep_body, 0)
    _interleave(back(n_grp - 1, (n_grp - 1) & 1))

    chains = [(s, hh) for s in range(ns) for hh in range(hp)]
    for s, hh in chains:
        for d in range(2):
            s_s[s * hp + hh, d] = s0_ref[s, d, hh] if has_s0 else jnp.zeros((HEAD_W, HEAD_W), F32)

    def scan_body(first, i, carry):
        todo = [(s, hh, d, s * n_ch + (i if d == 0 else n_ch - 1 - i)) for s, hh in chains for d in range(2)]
        st = [s_s[s * hp + hh, d] for s, hh, d, m in todo]
        r = [jnp.dot(wq_s[hh, d, m], x.astype(BF16), preferred_element_type=F32)
             for (s, hh, d, m), x in zip(todo, st)]
        vn = []
        for (s, hh, d, m), rr in zip(todo, r):
            vn.append((u_s[hh, d, m] - rr[:CHUNK]).astype(BF16))
            vn_s[hh, d, m] = vn[-1]
        upd = []
        for (s, hh, d, m), x in zip(todo, vn):
            kdt = kdt_s[hh, m]
            zero = jnp.zeros_like(kdt)
            kdt = jnp.where(lane_f128, kdt, zero) if d == 0 else jnp.where(lane_f128, zero, kdt)
            upd.append(jnp.dot(kdt, jnp.concatenate([x, x], axis=0), preferred_element_type=F32))
        for (s, hh, d, m), x, rr, up in zip(todo, st, r, upd):
            s_s[s * hp + hh, d] = x * dl_s[hh, d, m][0:1, :] + up
            qs_s[hh, m] = rr[CHUNK:] if first else qs_s[hh, m] + rr[CHUNK:]
        return carry

    lax.fori_loop(0, n_ch // 2, functools.partial(scan_body, True), 0)
    lax.fori_loop(n_ch // 2, n_ch, functools.partial(scan_body, False), 0)
    if out_state:
        for s, hh in chains:
            for d in range(2):
                st_ref[s, d, hh] = s_s[s * hp + hh, d]

    dnn = dnn_ref[...]

    def out_body(g, carry):
        ms, ss, cs = group_chunks(g)
        for t, hh in units:
            m, s = ms[t], ss[t]
            r0 = pl.multiple_of(cs[t] * CHUNK, CHUNK)
            vst = jnp.concatenate([vn_s[hh, 0, m], vn_s[hh, 1, m]], axis=0)
            o = qs_s[hh, m] + jnp.dot(at_s[hh, m], vst, preferred_element_type=F32)
            y = o * lax.rsqrt(jnp.mean(o * o, axis=-1, keepdims=True) + EPS) * dnn
            o_ref[s, pl.ds(r0, CHUNK), cols(hh)] = (
                y * _silu(z_ref[s, pl.ds(r0, CHUNK), cols(hh)])).astype(o_ref.dtype)
        return carry

    lax.fori_loop(0, n_grp, out_body, 0)


def _dn_call(proj3, conv_w, alog_row, dtb_row, dn_norm, state_delta, layer, out_state):
    n_seq, seq_len, _ = proj3.shape
    n_ch = seq_len // CHUNK
    hp = DN_HEADS_PER_STEP
    ns = max(1, min(n_seq, DN_CHUNKS_PER_STEP // n_ch))
    assert n_seq % ns == 0
    tot = ns * n_ch
    nu = min(PREP_UNROLL, tot) * hp
    wid = hp * HEAD_W
    nblk = DN_HEADS // hp
    has_s0 = state_delta is not None
    tile = lambda off: pl.BlockSpec((ns, seq_len, wid), lambda b, h: (b, 0, off + h),
                                    pipeline_mode=pl.Buffered(1))
    cw = lambda off: pl.BlockSpec((CONV_K, wid), lambda b, h: (0, off + h))
    row = pl.BlockSpec((1, HEAD_W), lambda b, h: (0, 0))
    in_specs = [tile(0), tile(nblk), tile(2 * nblk), tile(3 * nblk),
                pl.BlockSpec((ns, seq_len, HEAD_W), lambda b, h: (b, 0, GATE_BLK),
                             pipeline_mode=pl.Buffered(1)),
                cw(0), cw(nblk), cw(2 * nblk), row, row, row]
    args = [proj3, proj3, proj3, proj3, proj3, conv_w, conv_w, conv_w, alog_row, dtb_row,
            dn_norm.reshape(1, HEAD_W)]
    if has_s0:
        in_specs.append(pl.BlockSpec((ns, None, 2, hp, HEAD_W, HEAD_W),
                                     lambda b, h: (b, layer, 0, h, 0, 0)))
        args.append(state_delta)
    out_shape = [jax.ShapeDtypeStruct((n_seq, seq_len, DN_W), BF16)]
    out_specs = [pl.BlockSpec((ns, seq_len, wid), lambda b, h: (b, 0, h))]
    if out_state:
        out_shape.append(jax.ShapeDtypeStruct((n_seq, 2, DN_HEADS, HEAD_W, HEAD_W), F32))
        out_specs.append(pl.BlockSpec((ns, 2, hp, HEAD_W, HEAD_W), lambda b, h: (b, 0, h, 0, 0)))
    scratch = [
        pltpu.VMEM((hp, 2, tot, 2 * CHUNK, HEAD_W), BF16),
        pltpu.VMEM((hp, 2, tot, CHUNK, HEAD_W), F32),
        pltpu.VMEM((hp, tot, HEAD_W, 2 * CHUNK), BF16),
        pltpu.VMEM((hp, 2, tot, 8, HEAD_W), F32),
        pltpu.VMEM((hp, tot, CHUNK, 2 * CHUNK), BF16),
        pltpu.VMEM((hp, 2, tot, CHUNK, HEAD_W), BF16),
        pltpu.VMEM((hp, tot, CHUNK, HEAD_W), F32),
        pltpu.VMEM((ns * hp, 2, HEAD_W, HEAD_W), F32),
        pltpu.VMEM((3 * nu, CHUNK + 16, HEAD_W), F32),
        pltpu.VMEM((2, nu, 2 * CHUNK, 2 * CHUNK), BF16),
        pltpu.VMEM((2, nu, 2 * CHUNK, 2 * HEAD_W), BF16),
    ]
    outs = pl.pallas_call(
        functools.partial(_dn_kernel, seq_len=seq_len, ns=ns, hp=hp, has_s0=has_s0, out_state=out_state),
        out_shape=out_shape,
        grid=(n_seq // ns, nblk),
        in_specs=in_specs,
        out_specs=out_specs,
        scratch_shapes=scratch,
        compiler_params=pltpu.CompilerParams(vmem_limit_bytes=V7X_VMEM_LIMIT),
        name="deltanet",
    )(*args)
    return (outs[0], outs[1]) if out_state else (outs[0], None)


def _split3(x):
    hi = x.astype(BF16)
    r1 = x - hi.astype(F32)
    mid = r1.astype(BF16)
    lo = (r1 - mid.astype(F32)).astype(BF16)
    return hi, mid, lo


def _window(pos, win, n):
    lo = jnp.clip(pos - win // 2, 0, n)
    hi = jnp.clip(pos - win // 2 + win, 0, n)
    return lo, hi


def _pool_kernel(u_ref, pw_ref, ps_ref, o_ref, m_s, *, seq_len, rows):
    period = GRID_W if rows else seq_len
    tb = min(POOL_TB, seq_len)
    r_i = _iota((tb, tb), 0)
    c_i = _iota((tb, tb), 1)
    shift = period.bit_length() - 1
    same_line = (r_i >> shift) == (c_i >> shift)
    pos_r = r_i & (period - 1)
    pos_c = c_i & (period - 1)
    pos_col = _iota((tb, HEAD_W), 0) & (period - 1)
    for g, win in enumerate(POOL_WINDOWS):
        cols = slice(g * HEAD_W, (g + 1) * HEAD_W)
        if rows:
            run = None
            prev_lo = prev_hi = 0
            for r in range(rows):
                lo, hi = max(r - win // 2, 0), min(r - win // 2 + win, rows)
                for a in range(prev_hi, hi):
                    slab = u_ref[a * GRID_W:(a + 1) * GRID_W, cols]
                    run = slab if run is None else run + slab
                for a in range(prev_lo, lo):
                    run = run - u_ref[a * GRID_W:(a + 1) * GRID_W, cols]
                prev_lo, prev_hi = lo, hi
                m_s[r * GRID_W:(r + 1) * GRID_W, :] = run / float(hi - lo)
        lo_r, hi_r = _window(pos_r, win, period)
        band = jnp.where(same_line & (pos_c >= lo_r) & (pos_c < hi_r), 1.0, 0.0).astype(BF16)
        lo_c, hi_c = _window(pos_col, win, period)
        cnt = (hi_c - lo_c).astype(F32)
        pw = pw_ref[g]
        scale = ps_ref[:, cols]
        for t0 in range(0, seq_len, tb):
            ug = u_ref[t0:t0 + tb, cols]
            src = m_s[t0:t0 + tb, :] if rows else ug
            hi3, mid3, lo3 = _split3(src)
            box = (jnp.dot(band, hi3, preferred_element_type=F32)
                   + jnp.dot(band, mid3, preferred_element_type=F32)
                   + jnp.dot(band, lo3, preferred_element_type=F32))
            mean = box / cnt
            o_ref[t0:t0 + tb, cols] = (_bdot(mean - ug, pw) * scale).astype(o_ref.dtype)


def _pool_call(proj3, pool_w_bf, pool_scale, rows):
    n_seq, seq_len, _ = proj3.shape
    return pl.pallas_call(
        functools.partial(_pool_kernel, seq_len=seq_len, rows=rows),
        out_shape=jax.ShapeDtypeStruct((n_seq, seq_len, POOL_W), BF16),
        grid=(n_seq,),
        in_specs=[
            pl.BlockSpec((None, seq_len, POOL_W), lambda b: (b, 0, POOL_BLK512)),
            pl.BlockSpec((POOL_GROUPS, HEAD_W, HEAD_W), lambda b: (0, 0, 0)),
            pl.BlockSpec((1, POOL_W), lambda b: (0, 0)),
        ],
        out_specs=pl.BlockSpec((None, seq_len, POOL_W), lambda b: (b, 0, 0)),
        scratch_shapes=[pltpu.VMEM((seq_len, HEAD_W), F32)],
        compiler_params=pltpu.CompilerParams(vmem_limit_bytes=V7X_VMEM_LIMIT),
        name="pool",
    )(proj3, pool_w_bf, pool_scale.reshape(1, POOL_W))


def _post_kernel(x_ref, o_ref, p_ref, ada_ref, nf_ref, fin_ref, wo_ref, wgu_ref, wd_ref, y_ref, *, final):
    mix = (jnp.dot(o_ref[...], wo_ref[:DN_W, :], preferred_element_type=F32)
           + jnp.dot(p_ref[...], wo_ref[DN_W:, :], preferred_element_type=F32))
    x = x_ref[...] + ada_ref[2:3, :] * mix
    h = _modulated_norm(x, nf_ref[...], ada_ref[3:4, :], ada_ref[4:5, :])
    gu = jnp.dot(h.astype(BF16), wgu_ref[...], preferred_element_type=F32)
    act = _silu(gu[:, :D_FF]) * gu[:, D_FF:]
    x = x + ada_ref[5:6, :] * jnp.dot(act.astype(BF16), wd_ref[...], preferred_element_type=F32)
    if final:
        x = x * lax.rsqrt(jnp.mean(x * x, axis=-1, keepdims=True) + EPS) * fin_ref[...]
    y_ref[...] = x


def _post_call(x2, o2, p2, ada_l, norm_ffn, final_norm, w_out_bf, w_gu_bf, w_down_bf, seq_len, per_seq, final):
    n_tok = x2.shape[0]
    tm = TM_POST
    const = lambda shape: pl.BlockSpec(shape, lambda i: (0, 0), pipeline_mode=pl.Buffered(1))
    return pl.pallas_call(
        functools.partial(_post_kernel, final=final),
        out_shape=jax.ShapeDtypeStruct((n_tok, D_MODEL), F32),
        grid=(n_tok // tm,),
        in_specs=[
            pl.BlockSpec((tm, D_MODEL), lambda i: (i, 0)),
            pl.BlockSpec((tm, DN_W), lambda i: (i, 0)),
            pl.BlockSpec((tm, POOL_W), lambda i: (i, 0)),
            pl.BlockSpec((None, 6, D_MODEL), _ada_row_map(seq_len, tm, per_seq)),
            pl.BlockSpec((1, D_MODEL), lambda i: (0, 0)),
            pl.BlockSpec((1, D_MODEL), lambda i: (0, 0)),
            const((D_MODEL, D_MODEL)),
            const((D_MODEL, 2 * D_FF)),
            const((D_FF, D_MODEL)),
        ],
        out_specs=pl.BlockSpec((tm, D_MODEL), lambda i: (i, 0)),
        compiler_params=pltpu.CompilerParams(vmem_limit_bytes=V7X_VMEM_LIMIT),
        name="post",
    )(x2, o2, p2, ada_l, norm_ffn.reshape(1, D_MODEL), final_norm.reshape(1, D_MODEL),
      w_out_bf, w_gu_bf, w_down_bf)


def _gate_row(p):
    return jnp.zeros((1, HEAD_W), F32).at[0, 2 * DN_HEADS:4 * DN_HEADS].set(p.reshape(-1))


def _stream_layer(x2, seq_len, rows, per_seq, ada_l, lw, state_delta, layer, final):
    n_seq = x2.shape[0] // seq_len
    proj = _inproj_call(x2, ada_l, lw["norm_mix"], lw["w_in"], seq_len, per_seq)
    proj3 = proj.reshape(n_seq, seq_len, PROJ_W)
    o, st = _dn_call(proj3, lw["conv_w"], lw["alog_row"], lw["dtb_row"], lw["dn_norm"],
                     state_delta, layer, out_state=state_delta is None)
    p = _pool_call(proj3, lw["pool_w"], lw["pool_scale"], rows)
    x2 = _post_call(x2, o.reshape(-1, DN_W), p.reshape(-1, POOL_W), ada_l, lw["norm_ffn"], lw["final_norm"],
                    lw["w_out"], lw["w_gu"], lw["w_down"], seq_len, per_seq, final)
    return x2, st


def kernel(x_prompt, x_sample, c, state_delta, c_ctx, w_ada, b_ada, norm_mix, norm_ffn, w_in, conv_w, a_log, dt_bias, dn_norm, pool_w, pool_scale, w_out, w_gu, w_down, final_norm):
    depth = w_ada.shape[0]
    n_ctx, ctx_len, _ = x_prompt.shape
    n_lat, lat_len, _ = x_sample.shape
    assert 1 + n_lat <= ADA_ROWS
    c_all = jnp.concatenate([c_ctx[None, :], c, jnp.zeros((ADA_ROWS - 1 - n_lat, D_MODEL), F32)], axis=0)
    ada = _ada_call(c_all, w_ada, b_ada).reshape(depth, ADA_ROWS, 6, D_MODEL)

    qkvz = 4 * DN_W
    n_gate = 4 * DN_HEADS
    xp = x_prompt.reshape(-1, D_MODEL)
    xs = x_sample.reshape(-1, D_MODEL)
    states = []
    for l in range(depth):
        w_in_p = jnp.concatenate(
            [w_in[l][:, :qkvz], w_in[l][:, qkvz + n_gate:], w_in[l][:, qkvz:qkvz + n_gate],
             jnp.zeros((D_MODEL, HEAD_W - n_gate), F32)], axis=1).astype(BF16)
        lw = dict(norm_mix=norm_mix[l], norm_ffn=norm_ffn[l], w_in=w_in_p, conv_w=conv_w[l],
                  alog_row=_gate_row(a_log[l]), dtb_row=_gate_row(dt_bias[l]), dn_norm=dn_norm[l],
                  pool_w=pool_w[l].astype(BF16), pool_scale=pool_scale[l], w_out=w_out[l].astype(BF16),
                  w_gu=w_gu[l].astype(BF16), w_down=w_down[l].astype(BF16), final_norm=final_norm)
        final = l == depth - 1
        xp, st = _stream_layer(xp, ctx_len, None, False, ada[l], lw, None, l, final)
        states.append(st)
        xs, _ = _stream_layer(xs, lat_len, lat_len // GRID_W, True, ada[l], lw, state_delta, l, final)
    new_state = jnp.stack(states, axis=1)
    return (xp.reshape(x_prompt.shape), xs.reshape(x_sample.shape), new_state)
```

```python
import functools

import jax
import jax.numpy as jnp
from jax import lax
from jax.experimental import pallas as pl
from jax.experimental.pallas import tpu as pltpu

F32 = jnp.float32
BF16 = jnp.bfloat16

D_MODEL = 1024
DN_HEADS = 4
HEAD_W = 128
DN_W = DN_HEADS * HEAD_W
POOL_GROUPS = 4
POOL_W = POOL_GROUPS * HEAD_W
POOL_WINDOWS = (2, 4, 8, 16)
GRID_W = 64
CONV_K = 5
CHUNK = 64
D_FF = 2816
EPS = 1e-6

PROJ_W = 3 * DN_W + DN_W + POOL_W + HEAD_W
POOL_BLK512 = 4
GATE_BLK = 20
GATE_COLS = 4 * DN_HEADS
ADA_ROWS = 16

V7X_VMEM_LIMIT = 56 * 1024 * 1024

TM_INPROJ = 256
TM_POST = 256
POOL_TB = 256
PREP_UNROLL = 8
DN_HEADS_PER_STEP = 2
DN_CHUNKS_PER_STEP = 32


def _sigmoid(x):
    return 0.5 * jnp.tanh(0.5 * x) + 0.5


def _silu(x):
    return x * _sigmoid(x)


def _softplus(x):
    return jnp.maximum(x, 0.0) + jnp.log(1.0 + jnp.exp(-jnp.abs(x)))


def _bdot(a, b):
    return jnp.dot(a.astype(BF16), b.astype(BF16), preferred_element_type=F32)


def _iota(shape, dim):
    return lax.broadcasted_iota(jnp.int32, shape, dim)


def _ada_kernel(c_ref, w_ref, b_ref, o_ref):
    o_ref[...] = jnp.dot(_silu(c_ref[...]), w_ref[...], preferred_element_type=F32,
                         precision=lax.Precision.HIGHEST) + b_ref[...]


def _ada_call(c_all, w_ada, b_ada):
    depth, _, n_out = w_ada.shape
    tn = 1536
    return pl.pallas_call(
        _ada_kernel,
        out_shape=jax.ShapeDtypeStruct((depth, ADA_ROWS, n_out), F32),
        grid=(depth, n_out // tn),
        in_specs=[
            pl.BlockSpec((ADA_ROWS, D_MODEL), lambda l, j: (0, 0)),
            pl.BlockSpec((None, D_MODEL, tn), lambda l, j: (l, 0, j)),
            pl.BlockSpec((None, 1, tn), lambda l, j: (l, 0, j)),
        ],
        out_specs=pl.BlockSpec((None, ADA_ROWS, tn), lambda l, j: (l, 0, j)),
        compiler_params=pltpu.CompilerParams(vmem_limit_bytes=V7X_VMEM_LIMIT),
        name="ada",
    )(c_all, w_ada, b_ada.reshape(depth, 1, n_out))


def _modulated_norm(x, gain, shift, scale):
    y = x * lax.rsqrt(jnp.mean(x * x, axis=-1, keepdims=True) + EPS) * gain
    return y * (1.0 + scale) + shift


def _inproj_kernel(x_ref, ada_ref, nw_ref, w_ref, o_ref):
    h = _modulated_norm(x_ref[...], nw_ref[...], ada_ref[0:1, :], ada_ref[1:2, :])
    o_ref[...] = jnp.dot(h.astype(BF16), w_ref[...], preferred_element_type=F32)


def _ada_row_map(seq_len, tm, per_seq):
    if not per_seq:
        return lambda i: (0, 0, 0)
    return lambda i: (1 + (i * tm) // seq_len, 0, 0)


def _inproj_call(x2, ada_l, norm_w, w_in_p, seq_len, per_seq):
    n_tok = x2.shape[0]
    tm = TM_INPROJ
    return pl.pallas_call(
        _inproj_kernel,
        out_shape=jax.ShapeDtypeStruct((n_tok, PROJ_W), F32),
        grid=(n_tok // tm,),
        in_specs=[
            pl.BlockSpec((tm, D_MODEL), lambda i: (i, 0)),
            pl.BlockSpec((None, 6, D_MODEL), _ada_row_map(seq_len, tm, per_seq)),
            pl.BlockSpec((1, D_MODEL), lambda i: (0, 0)),
            pl.BlockSpec((D_MODEL, PROJ_W), lambda i: (0, 0), pipeline_mode=pl.Buffered(1)),
        ],
        out_specs=pl.BlockSpec((tm, PROJ_W), lambda i: (i, 0)),
        compiler_params=pltpu.CompilerParams(vmem_limit_bytes=V7X_VMEM_LIMIT),
        name="inproj",
    )(x2, ada_l, norm_w.reshape(1, D_MODEL), w_in_p)


def _col_bcast(x, lane, idx):
    col = jnp.sum(jnp.where(lane == idx, x, 0.0), axis=-1, keepdims=True)
    return jnp.broadcast_to(col, x.shape)


def _each(fn, *lists):
    return [fn(*args) for args in zip(*lists)]


def _interleave(*gens):
    gens = list(gens)
    while gens:
        for g in list(gens):
            try:
                next(g)
            except StopIteration:
                gens.remove(g)


def _dn_kernel(*refs, seq_len, ns, hp, has_s0, out_state):
    n_ch = seq_len // CHUNK
    gc = min(PREP_UNROLL, ns * n_ch)
    n_grp = ns * n_ch // gc
    assert gc * GATE_COLS <= HEAD_W
    units = [(t, hh) for t in range(gc) for hh in range(hp)]
    nu = len(units)
    it = iter(refs)
    q_ref, k_ref, v_ref, z_ref, g_ref = (next(it) for _ in range(5))
    cwq_ref, cwk_ref, cwv_ref = (next(it) for _ in range(3))
    alog_ref, dtb_ref, dnn_ref = (next(it) for _ in range(3))
    s0_ref = next(it) if has_s0 else None
    o_ref = next(it)
    st_ref = next(it) if out_state else None
    wq_s, u_s, kdt_s, dl_s, at_s, vn_s, qs_s, s_s, ext_s, lbd_s, rhs_s = (next(it) for _ in range(11))

    head0 = pl.program_id(1) * hp
    row = _iota((CHUNK, 128), 0)
    lane = _iota((CHUNK, 128), 1)
    is_f = lane < CHUNK
    j = lane & (CHUNK - 1)
    incl = (is_f & (row >= j)) | (~is_f & (row <= j))
    strict = (is_f & (row > j)) | (~is_f & (row < j))
    diag = row == j
    eye_p = jnp.where(diag, 1.0, 0.0).astype(F32)
    blk16p = (row >> 4) == (j >> 4)
    r128 = _iota((128, 128), 0)
    c128 = _iota((128, 128), 1)
    lane_f128 = c128 < CHUNK
    cum_sel = ((r128 < CHUNK) & (c128 <= r128)) | ((r128 >= CHUNK) & (c128 >= r128 - CHUNK))
    cum_mat = jnp.where(cum_sel, 1.0, 0.0).astype(F32)[:, :CHUNK]

    neg_a = -jnp.exp(alog_ref[...])
    dtb = dtb_ref[...]
    cols = lambda hh: slice(hh * HEAD_W, (hh + 1) * HEAD_W)

    def group_chunks(g):
        ms = [g * gc + t for t in range(gc)]
        return ms, [m // n_ch for m in ms], [m % n_ch for m in ms]

    def conv_silu(ref, cw_ref, hh, s, c, r0, slot):
        p0 = pl.multiple_of(jnp.maximum(r0 - 8, 0), 8)
        n0 = pl.multiple_of(jnp.minimum(r0 + CHUNK, seq_len - 8), 8)
        ext_s[slot, 0:8, :] = jnp.where(c > 0, ref[s, pl.ds(p0, 8), cols(hh)], 0.0)
        ext_s[slot, 8:8 + CHUNK, :] = ref[s, pl.ds(r0, CHUNK), cols(hh)]
        ext_s[slot, 8 + CHUNK:16 + CHUNK, :] = jnp.where(c < n_ch - 1, ref[s, pl.ds(n0, 8), cols(hh)], 0.0)
        acc = ext_s[slot, 6:6 + CHUNK, :] * cw_ref[0:1, cols(hh)]
        for t in range(1, CONV_K):
            acc = acc + ext_s[slot, 6 + t:6 + t + CHUNK, :] * cw_ref[t:t + 1, cols(hh)]
        return _silu(acc)

    def l2n(x):
        return x * lax.rsqrt(jnp.sum(x * x, axis=-1, keepdims=True) + EPS)

    def gates(ss, r0s):
        packed = g_ref[ss[0], pl.ds(r0s[0], CHUNK), :]
        for t in range(1, gc):
            packed = packed + pltpu.roll(g_ref[ss[t], pl.ds(r0s[t], CHUNK), :], GATE_COLS * t, axis=1)
        is_beta = (lane & (GATE_COLS - 1)) < 2 * DN_HEADS
        gact = jnp.where(is_beta, _sigmoid(packed), neg_a * _softplus(packed + dtb))
        cums = jnp.dot(cum_mat, gact, preferred_element_type=F32, precision=lax.Precision.HIGHEST)
        e_cum = jnp.exp(cums)
        e_rest = jnp.concatenate([jnp.exp(cums[CHUNK - 1:CHUNK] - cums[:CHUNK]),
                                  jnp.exp(cums[CHUNK:CHUNK + 1] - cums[CHUNK:])], axis=0)
        return gact, cums, e_cum, e_rest

    def front(g, slot):
        ms, ss, cs = group_chunks(g)
        r0s = [pl.multiple_of(c * CHUNK, CHUNK) for c in cs]
        gact, cums, e_cum, e_rest = gates(ss, r0s)
        yield
        q, k, v = [], [], []
        for n, (t, hh) in enumerate(units):
            q.append(l2n(conv_silu(q_ref, cwq_ref, hh, ss[t], cs[t], r0s[t], n)) * (HEAD_W ** -0.5))
            k.append(l2n(conv_silu(k_ref, cwk_ref, hh, ss[t], cs[t], r0s[t], nu + n)))
            v.append(conv_silu(v_ref, cwv_ref, hh, ss[t], cs[t], r0s[t], 2 * nu + n))
            if n % 2 == 1:
                yield
        col = lambda x, t, hh, which: _col_bcast(x, lane, GATE_COLS * t + which * DN_HEADS + head0 + hh)
        beta_f = [col(gact, t, hh, 0) for t, hh in units]
        beta_b = [col(gact, t, hh, 1) for t, hh in units]
        gc_f = [col(cums[:CHUNK], t, hh, 2) for t, hh in units]
        gc_b = [col(cums[CHUNK:], t, hh, 3) for t, hh in units]
        yield
        e_f = [col(e_cum[:CHUNK], t, hh, 2) for t, hh in units]
        e_b = [col(e_cum[CHUNK:], t, hh, 3) for t, hh in units]
        rest_f = [col(e_rest[:CHUNK], t, hh, 2) for t, hh in units]
        rest_b = [col(e_rest[CHUNK:], t, hh, 3) for t, hh in units]
        yield

        def decay_of(gf, gb):
            gc_col = jnp.where(is_f, gf, gb)
            gc_row = jnp.sum(jnp.where(diag, gc_col, 0.0), axis=0, keepdims=True)
            return jnp.where(incl, jnp.exp(jnp.where(incl, gc_col - gc_row, 0.0)), 0.0)

        decay = _each(decay_of, gc_f, gc_b)
        yield

        def gram_of(qq, kk):
            kb = kk.astype(BF16)
            return lax.dot_general(jnp.concatenate([qq.astype(BF16), kb], axis=0),
                                   jnp.concatenate([kb, kb], axis=0),
                                   (((1,), (1,)), ((), ())), preferred_element_type=F32)

        gram = _each(gram_of, q, k)
        yield
        for n, (t, hh) in enumerate(units):
            l2 = jnp.where(strict, jnp.where(is_f, beta_f[n], beta_b[n]) * gram[n][CHUNK:] * decay[n], 0.0)
            lbd_s[slot, n] = l2.astype(BF16)
            at_s[hh, ms[t]] = (gram[n][:CHUNK] * decay[n]).astype(BF16)
        yield
        for n, (t, hh) in enumerate(units):
            rhs_s[slot, n] = jnp.concatenate([
                jnp.concatenate([v[n] * beta_f[n], k[n] * beta_f[n] * e_f[n]], axis=1),
                jnp.concatenate([v[n] * beta_b[n], k[n] * beta_b[n] * e_b[n]], axis=1)],
                axis=0).astype(BF16)
            wq_s[hh, 0, ms[t], CHUNK:, :] = (q[n] * e_f[n]).astype(BF16)
            wq_s[hh, 1, ms[t], CHUNK:, :] = (q[n] * e_b[n]).astype(BF16)
            if n % 4 == 3:
                yield
        for n, (t, hh) in enumerate(units):
            kdec = jnp.concatenate([k[n] * rest_f[n], k[n] * rest_b[n]], axis=0)
            kdt_s[hh, ms[t]] = kdec.T.astype(BF16)
            dl_s[hh, 0, ms[t]] = jnp.broadcast_to(e_f[n][CHUNK - 1:CHUNK, :], (8, 128))
            dl_s[hh, 1, ms[t]] = jnp.broadcast_to(e_b[n][0:1, :], (8, 128))
            if n % 4 == 3:
                yield

    def back(g, slot):
        ms = [g * gc + t for t in range(gc)]
        l2 = [lbd_s[slot, n] for n in range(nu)]
        zero = jnp.zeros((CHUNK, 128), BF16)

        def bd(x):
            x = x.astype(BF16)
            return jnp.concatenate([jnp.where(is_f, x, zero), jnp.where(is_f, zero, x)], axis=0)

        pdot = lambda a, b_bd: jnp.dot(a.astype(BF16), b_bd, preferred_element_type=F32)
        d = _each(lambda x: jnp.where(blk16p, x, zero), l2)
        lo_bd = _each(lambda x: bd(jnp.where(blk16p, zero, x)), l2)
        d_bd = _each(bd, d)
        d2 = _each(pdot, d, d_bd)
        yield
        d2_bd = _each(bd, d2)
        d3 = _each(pdot, d, d2_bd)
        d4 = _each(pdot, d2, d2_bd)
        yield
        d4_bd = _each(bd, d4)
        d8 = _each(pdot, d4, d4_bd)
        p1 = _each(lambda a, b, c: eye_p - a.astype(F32) + b - c, d, d2, d3)
        yield
        p2 = _each(lambda p, x: p + pdot(p, x), p1, d4_bd)
        yield
        dinv = _each(lambda p, x: p + pdot(p, bd(x)), p2, d8)
        yield
        m = _each(pdot, dinv, lo_bd)
        yield
        m_bd = _each(bd, m)
        m2 = _each(pdot, m, m_bd)
        yield
        m3 = _each(lambda a, b: pdot(a, bd(b)), m, m2)
        ninv = _each(lambda a, b, c: eye_p - a + b - c, m, m2, m3)
        yield
        t_inv = _each(lambda a, b: pdot(a, bd(b)), ninv, dinv)
        yield
        uw = [jnp.dot(bd(t_inv[n]), rhs_s[slot, n], preferred_element_type=F32) for n in range(nu)]
        yield
        for n, (t, hh) in enumerate(units):
            u_s[hh, 0, ms[t]] = uw[n][:CHUNK, :HEAD_W]
            u_s[hh, 1, ms[t]] = uw[n][CHUNK:, :HEAD_W]
            wq_s[hh, 0, ms[t], :CHUNK, :] = uw[n][:CHUNK, HEAD_W:].astype(BF16)
            wq_s[hh, 1, ms[t], :CHUNK, :] = uw[n][CHUNK:, HEAD_W:].astype(BF16)

    _interleave(front(0, 0))

    def prep_body(i, carry):
        slot = i & 1
        _interleave(back(i, slot), front(i + 1, 1 - slot))
        return carry

    lax.fori_loop(0, n_grp - 1, prep_body, 0)
    _interleave(back(n_grp - 1, (n_grp - 1) & 1))

    chains = [(s, hh) for s in range(ns) for hh in range(hp)]
    for s, hh in chains:
        for d in range(2):
            s_s[s * hp + hh, d] = s0_ref[s, d, hh] if has_s0 else jnp.zeros((HEAD_W, HEAD_W), F32)

    def scan_body(first, i, carry):
        todo = [(s, hh, d, s * n_ch + (i if d == 0 else n_ch - 1 - i)) for s, hh in chains for d in range(2)]
        st = [s_s[s * hp + hh, d] for s, hh, d, m in todo]
        r = [jnp.dot(wq_s[hh, d, m], x.astype(BF16), preferred_element_type=F32)
             for (s, hh, d, m), x in zip(todo, st)]
        vn = []
        for (s, hh, d, m), rr in zip(todo, r):
            vn.append((u_s[hh, d, m] - rr[:CHUNK]).astype(BF16))
            vn_s[hh, d, m] = vn[-1]
        upd = []
        for (s, hh, d, m), x in zip(todo, vn):
            kdt = kdt_s[hh, m]
            zero = jnp.zeros_like(kdt)
            kdt = jnp.where(lane_f128, kdt, zero) if d == 0 else jnp.where(lane_f128, zero, kdt)
            upd.append(jnp.dot(kdt, jnp.concatenate([x, x], axis=0), preferred_element_type=F32))
        for (s, hh, d, m), x, rr, up in zip(todo, st, r, upd):
            s_s[s * hp + hh, d] = x * dl_s[hh, d, m][0:1, :] + up
            qs_s[hh, m] = rr[CHUNK:] if first else qs_s[hh, m] + rr[CHUNK:]
        return carry

    lax.fori_loop(0, n_ch // 2, functools.partial(scan_body, True), 0)
    lax.fori_loop(n_ch // 2, n_ch, functools.partial(scan_body, False), 0)
    if out_state:
        for s, hh in chains:
            for d in range(2):
                st_ref[s, d, hh] = s_s[s * hp + hh, d]

    dnn = dnn_ref[...]

    def out_body(g, carry):
        ms, ss, cs = group_chunks(g)
        for t, hh in units:
            m, s = ms[t], ss[t]
            r0 = pl.multiple_of(cs[t] * CHUNK, CHUNK)
            vst = jnp.concatenate([vn_s[hh, 0, m], vn_s[hh, 1, m]], axis=0)
            o = qs_s[hh, m] + jnp.dot(at_s[hh, m], vst, preferred_element_type=F32)
            y = o * lax.rsqrt(jnp.mean(o * o, axis=-1, keepdims=True) + EPS) * dnn
            o_ref[s, pl.ds(r0, CHUNK), cols(hh)] = (
                y * _silu(z_ref[s, pl.ds(r0, CHUNK), cols(hh)])).astype(o_ref.dtype)
        return carry

    lax.fori_loop(0, n_grp, out_body, 0)


def _dn_call(proj3, conv_w, alog_row, dtb_row, dn_norm, state_delta, layer, out_state):
    n_seq, seq_len, _ = proj3.shape
    n_ch = seq_len // CHUNK
    hp = DN_HEADS_PER_STEP
    ns = max(1, min(n_seq, DN_CHUNKS_PER_STEP // n_ch))
    assert n_seq % ns == 0
    tot = ns * n_ch
    nu = min(PREP_UNROLL, tot) * hp
    wid = hp * HEAD_W
    nblk = DN_HEADS // hp
    has_s0 = state_delta is not None
    tile = lambda off: pl.BlockSpec((ns, seq_len, wid), lambda b, h: (b, 0, off + h),
                                    pipeline_mode=pl.Buffered(1))
    cw = lambda off: pl.BlockSpec((CONV_K, wid), lambda b, h: (0, off + h))
    row = pl.BlockSpec((1, HEAD_W), lambda b, h: (0, 0))
    in_specs = [tile(0), tile(nblk), tile(2 * nblk), tile(3 * nblk),
                pl.BlockSpec((ns, seq_len, HEAD_W), lambda b, h: (b, 0, GATE_BLK),
                             pipeline_mode=pl.Buffered(1)),
                cw(0), cw(nblk), cw(2 * nblk), row, row, row]
    args = [proj3, proj3, proj3, proj3, proj3, conv_w, conv_w, conv_w, alog_row, dtb_row,
            dn_norm.reshape(1, HEAD_W)]
    if has_s0:
        in_specs.append(pl.BlockSpec((ns, None, 2, hp, HEAD_W, HEAD_W),
                                     lambda b, h: (b, layer, 0, h, 0, 0)))
        args.append(state_delta)
    out_shape = [jax.ShapeDtypeStruct((n_seq, seq_len, DN_W), BF16)]
    out_specs = [pl.BlockSpec((ns, seq_len, wid), lambda b, h: (b, 0, h))]
    if out_state:
        out_shape.append(jax.ShapeDtypeStruct((n_seq, 2, DN_HEADS, HEAD_W, HEAD_W), F32))
        out_specs.append(pl.BlockSpec((ns, 2, hp, HEAD_W, HEAD_W), lambda b, h: (b, 0, h, 0, 0)))
    scratch = [
        pltpu.VMEM((hp, 2, tot, 2 * CHUNK, HEAD_W), BF16),
        pltpu.VMEM((hp, 2, tot, CHUNK, HEAD_W), F32),
        pltpu.VMEM((hp, tot, HEAD_W, 2 * CHUNK), BF16),
        pltpu.VMEM((hp, 2, tot, 8, HEAD_W), F32),
        pltpu.VMEM((hp, tot, CHUNK, 2 * CHUNK), BF16),
        pltpu.VMEM((hp, 2, tot, CHUNK, HEAD_W), BF16),
        pltpu.VMEM((hp, tot, CHUNK, HEAD_W), F32),
        pltpu.VMEM((ns * hp, 2, HEAD_W, HEAD_W), F32),
        pltpu.VMEM((3 * nu, CHUNK + 16, HEAD_W), F32),
        pltpu.VMEM((2, nu, CHUNK, 2 * CHUNK), BF16),
        pltpu.VMEM((2, nu, 2 * CHUNK, 2 * HEAD_W), BF16),
    ]
    outs = pl.pallas_call(
        functools.partial(_dn_kernel, seq_len=seq_len, ns=ns, hp=hp, has_s0=has_s0, out_state=out_state),
        out_shape=out_shape,
        grid=(n_seq // ns, nblk),
        in_specs=in_specs,
        out_specs=out_specs,
        scratch_shapes=scratch,
        compiler_params=pltpu.CompilerParams(vmem_limit_bytes=V7X_VMEM_LIMIT),
        name="deltanet",
    )(*args)
    return (outs[0], outs[1]) if out_state else (outs[0], None)


def _split2(x):
    hi = x.astype(BF16)
    lo = (x - hi.astype(F32)).astype(BF16)
    return hi, lo


def _window(pos, win, n):
    lo = jnp.clip(pos - win // 2, 0, n)
    hi = jnp.clip(pos - win // 2 + win, 0, n)
    return lo, hi


def _pool_kernel(u_ref, pw_ref, ps_ref, o_ref, m_s, *, seq_len, rows):
    period = GRID_W if rows else seq_len
    tb = min(POOL_TB, seq_len)
    r_i = _iota((tb, tb), 0)
    c_i = _iota((tb, tb), 1)
    shift = period.bit_length() - 1
    same_line = (r_i >> shift) == (c_i >> shift)
    pos_r = r_i & (period - 1)
    pos_c = c_i & (period - 1)
    pos_col = _iota((tb, HEAD_W), 0) & (period - 1)
    for g, win in enumerate(POOL_WINDOWS):
        cols = slice(g * HEAD_W, (g + 1) * HEAD_W)
        if rows:
            run = None
            prev_lo = prev_hi = 0
            for r in range(rows):
                lo, hi = max(r - win // 2, 0), min(r - win // 2 + win, rows)
                for a in range(prev_hi, hi):
                    slab = u_ref[a * GRID_W:(a + 1) * GRID_W, cols]
                    run = slab if run is None else run + slab
                for a in range(prev_lo, lo):
                    run = run - u_ref[a * GRID_W:(a + 1) * GRID_W, cols]
                prev_lo, prev_hi = lo, hi
                m_s[r * GRID_W:(r + 1) * GRID_W, :] = run / float(hi - lo)
        lo_r, hi_r = _window(pos_r, win, period)
        band = jnp.where(same_line & (pos_c >= lo_r) & (pos_c < hi_r), 1.0, 0.0).astype(BF16)
        lo_c, hi_c = _window(pos_col, win, period)
        cnt = (hi_c - lo_c).astype(F32)
        pw = pw_ref[g]
        scale = ps_ref[:, cols]
        for t0 in range(0, seq_len, tb):
            ug = u_ref[t0:t0 + tb, cols]
            src = m_s[t0:t0 + tb, :] if rows else ug
            hi, lo = _split2(src)
            box = (jnp.dot(band, hi, preferred_element_type=F32)
                   + jnp.dot(band, lo, preferred_element_type=F32))
            mean = box / cnt
            o_ref[t0:t0 + tb, cols] = (_bdot(mean - ug, pw) * scale).astype(o_ref.dtype)


def _pool_call(proj3, pool_w_bf, pool_scale, rows):
    n_seq, seq_len, _ = proj3.shape
    return pl.pallas_call(
        functools.partial(_pool_kernel, seq_len=seq_len, rows=rows),
        out_shape=jax.ShapeDtypeStruct((n_seq, seq_len, POOL_W), BF16),
        grid=(n_seq,),
        in_specs=[
            pl.BlockSpec((None, seq_len, POOL_W), lambda b: (b, 0, POOL_BLK512)),
            pl.BlockSpec((POOL_GROUPS, HEAD_W, HEAD_W), lambda b: (0, 0, 0)),
            pl.BlockSpec((1, POOL_W), lambda b: (0, 0)),
        ],
        out_specs=pl.BlockSpec((None, seq_len, POOL_W), lambda b: (b, 0, 0)),
        scratch_shapes=[pltpu.VMEM((seq_len, HEAD_W), F32)],
        compiler_params=pltpu.CompilerParams(vmem_limit_bytes=V7X_VMEM_LIMIT),
        name="pool",
    )(proj3, pool_w_bf, pool_scale.reshape(1, POOL_W))


def _post_kernel(x_ref, o_ref, p_ref, ada_ref, nf_ref, fin_ref, wo_ref, wgu_ref, wd_ref, y_ref, *, final):
    mix = (jnp.dot(o_ref[...], wo_ref[:DN_W, :], preferred_element_type=F32)
           + jnp.dot(p_ref[...], wo_ref[DN_W:, :], preferred_element_type=F32))
    x = x_ref[...] + ada_ref[2:3, :] * mix
    h = _modulated_norm(x, nf_ref[...], ada_ref[3:4, :], ada_ref[4:5, :])
    gu = jnp.dot(h.astype(BF16), wgu_ref[...], preferred_element_type=F32)
    act = _silu(gu[:, :D_FF]) * gu[:, D_FF:]
    x = x + ada_ref[5:6, :] * jnp.dot(act.astype(BF16), wd_ref[...], preferred_element_type=F32)
    if final:
        x = x * lax.rsqrt(jnp.mean(x * x, axis=-1, keepdims=True) + EPS) * fin_ref[...]
    y_ref[...] = x


def _post_call(x2, o2, p2, ada_l, norm_ffn, final_norm, w_out_bf, w_gu_bf, w_down_bf, seq_len, per_seq, final):
    n_tok = x2.shape[0]
    tm = TM_POST
    const = lambda shape: pl.BlockSpec(shape, lambda i: (0, 0), pipeline_mode=pl.Buffered(1))
    return pl.pallas_call(
        functools.partial(_post_kernel, final=final),
        out_shape=jax.ShapeDtypeStruct((n_tok, D_MODEL), F32),
        grid=(n_tok // tm,),
        in_specs=[
            pl.BlockSpec((tm, D_MODEL), lambda i: (i, 0)),
            pl.BlockSpec((tm, DN_W), lambda i: (i, 0)),
            pl.BlockSpec((tm, POOL_W), lambda i: (i, 0)),
            pl.BlockSpec((None, 6, D_MODEL), _ada_row_map(seq_len, tm, per_seq)),
            pl.BlockSpec((1, D_MODEL), lambda i: (0, 0)),
            pl.BlockSpec((1, D_MODEL), lambda i: (0, 0)),
            const((D_MODEL, D_MODEL)),
            const((D_MODEL, 2 * D_FF)),
            const((D_FF, D_MODEL)),
        ],
        out_specs=pl.BlockSpec((tm, D_MODEL), lambda i: (i, 0)),
        compiler_params=pltpu.CompilerParams(vmem_limit_bytes=V7X_VMEM_LIMIT),
        name="post",
    )(x2, o2, p2, ada_l, norm_ffn.reshape(1, D_MODEL), final_norm.reshape(1, D_MODEL),
      w_out_bf, w_gu_bf, w_down_bf)


def _gate_row(p):
    one = jnp.concatenate([jnp.zeros((2 * DN_HEADS,), F32), p.reshape(-1)])
    return jnp.tile(one, HEAD_W // GATE_COLS).reshape(1, HEAD_W)


def _stream_layer(x2, seq_len, rows, per_seq, ada_l, lw, state_delta, layer, final):
    n_seq = x2.shape[0] // seq_len
    proj = _inproj_call(x2, ada_l, lw["norm_mix"], lw["w_in"], seq_len, per_seq)
    proj3 = proj.reshape(n_seq, seq_len, PROJ_W)
    o, st = _dn_call(proj3, lw["conv_w"], lw["alog_row"], lw["dtb_row"], lw["dn_norm"],
                     state_delta, layer, out_state=state_delta is None)
    p = _pool_call(proj3, lw["pool_w"], lw["pool_scale"], rows)
    x2 = _post_call(x2, o.reshape(-1, DN_W), p.reshape(-1, POOL_W), ada_l, lw["norm_ffn"], lw["final_norm"],
                    lw["w_out"], lw["w_gu"], lw["w_down"], seq_len, per_seq, final)
    return x2, st


def kernel(x_prompt, x_sample, c, state_delta, c_ctx, w_ada, b_ada, norm_mix, norm_ffn, w_in, conv_w, a_log, dt_bias, dn_norm, pool_w, pool_scale, w_out, w_gu, w_down, final_norm):
    depth = w_ada.shape[0]
    n_ctx, ctx_len, _ = x_prompt.shape
    n_lat, lat_len, _ = x_sample.shape
    assert 1 + n_lat <= ADA_ROWS
    c_all = jnp.concatenate([c_ctx[None, :], c, jnp.zeros((ADA_ROWS - 1 - n_lat, D_MODEL), F32)], axis=0)
    ada = _ada_call(c_all, w_ada, b_ada).reshape(depth, ADA_ROWS, 6, D_MODEL)

    qkvz = 4 * DN_W
    n_gate = 4 * DN_HEADS
    xp = x_prompt.reshape(-1, D_MODEL)
    xs = x_sample.reshape(-1, D_MODEL)
    states = []
    for l in range(depth):
        w_in_p = jnp.concatenate(
            [w_in[l][:, :qkvz], w_in[l][:, qkvz + n_gate:], w_in[l][:, qkvz:qkvz + n_gate],
             jnp.zeros((D_MODEL, HEAD_W - n_gate), F32)], axis=1).astype(BF16)
        lw = dict(norm_mix=norm_mix[l], norm_ffn=norm_ffn[l], w_in=w_in_p, conv_w=conv_w[l],
                  alog_row=_gate_row(a_log[l]), dtb_row=_gate_row(dt_bias[l]), dn_norm=dn_norm[l],
                  pool_w=pool_w[l].astype(BF16), pool_scale=pool_scale[l], w_out=w_out[l].astype(BF16),
                  w_gu=w_gu[l].astype(BF16), w_down=w_down[l].astype(BF16), final_norm=final_norm)
        final = l == depth - 1
        xp, st = _stream_layer(xp, ctx_len, None, False, ada[l], lw, None, l, final)
        states.append(st)
        xs, _ = _stream_layer(xs, lat_len, lat_len // GRID_W, True, ada[l], lw, state_delta, l, final)
    new_state = jnp.stack(states, axis=1)
    return (xp.reshape(x_prompt.shape), xs.reshape(x_sample.shape), new_state)
```

```python
import functools

import jax
import jax.numpy as jnp
from jax import lax
from jax.experimental import pallas as pl
from jax.experimental.pallas import tpu as pltpu

F32 = jnp.float32
BF16 = jnp.bfloat16

D_MODEL = 1024
DN_HEADS = 4
HEAD_W = 128
DN_W = DN_HEADS * HEAD_W
POOL_GROUPS = 4
POOL_W = POOL_GROUPS * HEAD_W
POOL_WINDOWS = (2, 4, 8, 16)
GRID_W = 64
CONV_K = 5
CHUNK = 64
D_FF = 2816
EPS = 1e-6

PROJ_W = 3 * DN_W + DN_W + POOL_W + HEAD_W
POOL_BLK512 = 4
GATE_BLK = 20
GATE_COLS = 4 * DN_HEADS
ADA_ROWS = 16

V7X_VMEM_LIMIT = 56 * 1024 * 1024

TM_INPROJ = 512
TM_POST = 512
POOL_TB = 256
PREP_UNROLL = 8
DN_HEADS_PER_STEP = 2
YIELD_EVERY = 16
DN_CHUNKS_PER_STEP = 32


def _sigmoid(x):
    return 0.5 * jnp.tanh(0.5 * x) + 0.5


def _silu(x):
    return x * _sigmoid(x)


def _softplus(x):
    return jnp.maximum(x, 0.0) + jnp.log(1.0 + jnp.exp(-jnp.abs(x)))


def _bdot(a, b):
    return jnp.dot(a.astype(BF16), b.astype(BF16), preferred_element_type=F32)


def _iota(shape, dim):
    return lax.broadcasted_iota(jnp.int32, shape, dim)


def _ada_kernel(c_ref, w_ref, b_ref, o_ref):
    o_ref[...] = jnp.dot(_silu(c_ref[...]), w_ref[...], preferred_element_type=F32,
                         precision=lax.Precision.HIGHEST) + b_ref[...]


def _ada_call(c_all, w_ada, b_ada):
    depth, _, n_out = w_ada.shape
    tn = 1536
    return pl.pallas_call(
        _ada_kernel,
        out_shape=jax.ShapeDtypeStruct((depth, ADA_ROWS, n_out), F32),
        grid=(depth, n_out // tn),
        in_specs=[
            pl.BlockSpec((ADA_ROWS, D_MODEL), lambda l, j: (0, 0)),
            pl.BlockSpec((None, D_MODEL, tn), lambda l, j: (l, 0, j)),
            pl.BlockSpec((None, 1, tn), lambda l, j: (l, 0, j)),
        ],
        out_specs=pl.BlockSpec((None, ADA_ROWS, tn), lambda l, j: (l, 0, j)),
        compiler_params=pltpu.CompilerParams(vmem_limit_bytes=V7X_VMEM_LIMIT),
        name="ada",
    )(c_all, w_ada, b_ada.reshape(depth, 1, n_out))


def _modulated_norm(x, gain, shift, scale):
    y = x * lax.rsqrt(jnp.mean(x * x, axis=-1, keepdims=True) + EPS) * gain
    return y * (1.0 + scale) + shift


def _inproj_kernel(x_ref, ada_ref, nw_ref, w_ref, o_ref):
    h = _modulated_norm(x_ref[...], nw_ref[...], ada_ref[0:1, :], ada_ref[1:2, :])
    o_ref[...] = jnp.dot(h.astype(BF16), w_ref[...], preferred_element_type=F32)


def _ada_row_map(seq_len, tm, per_seq):
    if not per_seq:
        return lambda i: (0, 0, 0)
    return lambda i: (1 + (i * tm) // seq_len, 0, 0)


def _inproj_call(x2, ada_l, norm_w, w_in_p, seq_len, per_seq):
    n_tok = x2.shape[0]
    tm = TM_INPROJ
    return pl.pallas_call(
        _inproj_kernel,
        out_shape=jax.ShapeDtypeStruct((n_tok, PROJ_W), F32),
        grid=(n_tok // tm,),
        in_specs=[
            pl.BlockSpec((tm, D_MODEL), lambda i: (i, 0)),
            pl.BlockSpec((None, 6, D_MODEL), _ada_row_map(seq_len, tm, per_seq)),
            pl.BlockSpec((1, D_MODEL), lambda i: (0, 0)),
            pl.BlockSpec((D_MODEL, PROJ_W), lambda i: (0, 0), pipeline_mode=pl.Buffered(1)),
        ],
        out_specs=pl.BlockSpec((tm, PROJ_W), lambda i: (i, 0)),
        compiler_params=pltpu.CompilerParams(vmem_limit_bytes=V7X_VMEM_LIMIT),
        name="inproj",
    )(x2, ada_l, norm_w.reshape(1, D_MODEL), w_in_p)


def _col_bcast(x, lane, idx):
    col = jnp.sum(jnp.where(lane == idx, x, 0.0), axis=-1, keepdims=True)
    return jnp.broadcast_to(col, x.shape)


def _each(fn, *lists):
    return [fn(*args) for args in zip(*lists)]


def _each_y(fn, *lists, every=YIELD_EVERY):
    out = []
    for i, args in enumerate(zip(*lists)):
        out.append(fn(*args))
        if i % every == every - 1:
            yield
    return out


def _interleave(*gens):
    gens = list(gens)
    while gens:
        for g in list(gens):
            try:
                next(g)
            except StopIteration:
                gens.remove(g)


def _dn_kernel(*refs, seq_len, ns, hp, has_s0, out_state):
    n_ch = seq_len // CHUNK
    gc = min(PREP_UNROLL, ns * n_ch)
    n_grp = ns * n_ch // gc
    assert gc * GATE_COLS <= HEAD_W
    units = [(t, hh) for t in range(gc) for hh in range(hp)]
    nu = len(units)
    it = iter(refs)
    q_ref, k_ref, v_ref, z_ref, g_ref = (next(it) for _ in range(5))
    cwq_ref, cwk_ref, cwv_ref = (next(it) for _ in range(3))
    alog_ref, dtb_ref, dnn_ref = (next(it) for _ in range(3))
    s0_ref = next(it) if has_s0 else None
    o_ref = next(it)
    st_ref = next(it) if out_state else None
    wq_s, u_s, kdt_s, dl_s, at_s, vn_s, qs_s, s_s, ext_s, lbd_s, rhs_s = (next(it) for _ in range(11))

    head0 = pl.program_id(1) * hp
    row = _iota((CHUNK, 128), 0)
    lane = _iota((CHUNK, 128), 1)
    is_f = lane < CHUNK
    j = lane & (CHUNK - 1)
    incl = (is_f & (row >= j)) | (~is_f & (row <= j))
    strict = (is_f & (row > j)) | (~is_f & (row < j))
    diag = row == j
    eye_p = jnp.where(diag, 1.0, 0.0).astype(F32)
    blk16p = (row >> 4) == (j >> 4)
    r128 = _iota((128, 128), 0)
    c128 = _iota((128, 128), 1)
    lane_f128 = c128 < CHUNK
    cum_sel = ((r128 < CHUNK) & (c128 <= r128)) | ((r128 >= CHUNK) & (c128 >= r128 - CHUNK))
    cum_mat = jnp.where(cum_sel, 1.0, 0.0).astype(F32)[:, :CHUNK]

    neg_a = -jnp.exp(alog_ref[...])
    dtb = dtb_ref[...]
    cols = lambda hh: slice(hh * HEAD_W, (hh + 1) * HEAD_W)

    def group_chunks(g):
        ms = [g * gc + t for t in range(gc)]
        return ms, [m // n_ch for m in ms], [m % n_ch for m in ms]

    def conv_silu(ref, cw_ref, hh, s, c, r0, slot):
        p0 = pl.multiple_of(jnp.maximum(r0 - 8, 0), 8)
        n0 = pl.multiple_of(jnp.minimum(r0 + CHUNK, seq_len - 8), 8)
        ext_s[slot, 0:8, :] = jnp.where(c > 0, ref[s, pl.ds(p0, 8), cols(hh)], 0.0)
        ext_s[slot, 8:8 + CHUNK, :] = ref[s, pl.ds(r0, CHUNK), cols(hh)]
        ext_s[slot, 8 + CHUNK:16 + CHUNK, :] = jnp.where(c < n_ch - 1, ref[s, pl.ds(n0, 8), cols(hh)], 0.0)
        acc = ext_s[slot, 6:6 + CHUNK, :] * cw_ref[0:1, cols(hh)]
        for t in range(1, CONV_K):
            acc = acc + ext_s[slot, 6 + t:6 + t + CHUNK, :] * cw_ref[t:t + 1, cols(hh)]
        return _silu(acc)

    def l2n(x):
        return x * lax.rsqrt(jnp.sum(x * x, axis=-1, keepdims=True) + EPS)

    def gates(ss, r0s):
        packed = g_ref[ss[0], pl.ds(r0s[0], CHUNK), :]
        for t in range(1, gc):
            packed = packed + pltpu.roll(g_ref[ss[t], pl.ds(r0s[t], CHUNK), :], GATE_COLS * t, axis=1)
        is_beta = (lane & (GATE_COLS - 1)) < 2 * DN_HEADS
        gact = jnp.where(is_beta, _sigmoid(packed), neg_a * _softplus(packed + dtb))
        cums = jnp.dot(cum_mat, gact, preferred_element_type=F32, precision=lax.Precision.HIGHEST)
        e_cum = jnp.exp(cums)
        e_rest = jnp.concatenate([jnp.exp(cums[CHUNK - 1:CHUNK] - cums[:CHUNK]),
                                  jnp.exp(cums[CHUNK:CHUNK + 1] - cums[CHUNK:])], axis=0)
        return gact, cums, e_cum, e_rest

    def front(g, slot):
        ms, ss, cs = group_chunks(g)
        r0s = [pl.multiple_of(c * CHUNK, CHUNK) for c in cs]
        gact, cums, e_cum, e_rest = gates(ss, r0s)
        yield
        q, k, v = [], [], []
        for n, (t, hh) in enumerate(units):
            q.append(l2n(conv_silu(q_ref, cwq_ref, hh, ss[t], cs[t], r0s[t], n)) * (HEAD_W ** -0.5))
            k.append(l2n(conv_silu(k_ref, cwk_ref, hh, ss[t], cs[t], r0s[t], nu + n)))
            v.append(conv_silu(v_ref, cwv_ref, hh, ss[t], cs[t], r0s[t], 2 * nu + n))
            yield
        col = lambda x, which: (lambda u: _col_bcast(
            x, lane, GATE_COLS * u[0] + which * DN_HEADS + head0 + u[1]))
        beta_f = yield from _each_y(col(gact, 0), units)
        beta_b = yield from _each_y(col(gact, 1), units)
        gc_f = yield from _each_y(col(cums[:CHUNK], 2), units)
        gc_b = yield from _each_y(col(cums[CHUNK:], 3), units)
        e_f = yield from _each_y(col(e_cum[:CHUNK], 2), units)
        e_b = yield from _each_y(col(e_cum[CHUNK:], 3), units)
        rest_f = yield from _each_y(col(e_rest[:CHUNK], 2), units)
        rest_b = yield from _each_y(col(e_rest[CHUNK:], 3), units)

        def decay_of(gf, gb):
            gc_col = jnp.where(is_f, gf, gb)
            gc_row = jnp.sum(jnp.where(diag, gc_col, 0.0), axis=0, keepdims=True)
            return jnp.where(incl, jnp.exp(jnp.where(incl, gc_col - gc_row, 0.0)), 0.0)

        decay = yield from _each_y(decay_of, gc_f, gc_b, every=2)

        def gram_of(qq, kk):
            kb = kk.astype(BF16)
            return lax.dot_general(jnp.concatenate([qq.astype(BF16), kb], axis=0),
                                   jnp.concatenate([kb, kb], axis=0),
                                   (((1,), (1,)), ((), ())), preferred_element_type=F32)

        gram = yield from _each_y(gram_of, q, k, every=2)
        for n, (t, hh) in enumerate(units):
            l2 = jnp.where(strict, jnp.where(is_f, beta_f[n], beta_b[n]) * gram[n][CHUNK:] * decay[n], 0.0)
            lbd_s[slot, n] = l2.astype(BF16)
            at_s[hh, ms[t]] = (gram[n][:CHUNK] * decay[n]).astype(BF16)
            if n % 2 == 1:
                yield
        for n, (t, hh) in enumerate(units):
            rhs_s[slot, n] = jnp.concatenate([
                jnp.concatenate([v[n] * beta_f[n], k[n] * beta_f[n] * e_f[n]], axis=1),
                jnp.concatenate([v[n] * beta_b[n], k[n] * beta_b[n] * e_b[n]], axis=1)],
                axis=0).astype(BF16)
            wq_s[hh, 0, ms[t], CHUNK:, :] = (q[n] * e_f[n]).astype(BF16)
            wq_s[hh, 1, ms[t], CHUNK:, :] = (q[n] * e_b[n]).astype(BF16)
            if n % 2 == 1:
                yield
        for n, (t, hh) in enumerate(units):
            kdec = jnp.concatenate([k[n] * rest_f[n], k[n] * rest_b[n]], axis=0)
            kdt_s[hh, ms[t]] = kdec.T.astype(BF16)
            dl_s[hh, 0, ms[t]] = jnp.broadcast_to(e_f[n][CHUNK - 1:CHUNK, :], (8, 128))
            dl_s[hh, 1, ms[t]] = jnp.broadcast_to(e_b[n][0:1, :], (8, 128))
            if n % 2 == 1:
                yield

    def back(g, slot):
        ms = [g * gc + t for t in range(gc)]
        l2 = [lbd_s[slot, n] for n in range(nu)]
        zero = jnp.zeros((CHUNK, 128), BF16)

        def bd(x):
            x = x.astype(BF16)
            return jnp.concatenate([jnp.where(is_f, x, zero), jnp.where(is_f, zero, x)], axis=0)

        pdot = lambda a, b_bd: jnp.dot(a.astype(BF16), b_bd, preferred_element_type=F32)
        d = _each(lambda x: jnp.where(blk16p, x, zero), l2)
        lo_bd = _each(lambda x: bd(jnp.where(blk16p, zero, x)), l2)
        d2 = yield from _each_y(lambda a: pdot(a, bd(a)), d)
        d2_bd = _each(bd, d2)
        d3 = yield from _each_y(pdot, d, d2_bd)
        d4 = yield from _each_y(pdot, d2, d2_bd)
        d4_bd = _each(bd, d4)
        d8 = yield from _each_y(pdot, d4, d4_bd)
        p1 = _each(lambda a, b, c: eye_p - a.astype(F32) + b - c, d, d2, d3)
        p2 = yield from _each_y(lambda p, x: p + pdot(p, x), p1, d4_bd)
        dinv = yield from _each_y(lambda p, x: p + pdot(p, bd(x)), p2, d8)
        m = yield from _each_y(pdot, dinv, lo_bd)
        m_bd = _each(bd, m)
        m2 = yield from _each_y(pdot, m, m_bd)
        m3 = yield from _each_y(lambda a, b: pdot(a, bd(b)), m, m2)
        ninv = _each(lambda a, b, c: eye_p - a + b - c, m, m2, m3)
        t_inv = yield from _each_y(lambda a, b: pdot(a, bd(b)), ninv, dinv)
        uw = yield from _each_y(lambda tt, n: jnp.dot(bd(tt), rhs_s[slot, n], preferred_element_type=F32),
                                t_inv, range(nu), every=2)
        for n, (t, hh) in enumerate(units):
            u_s[hh, 0, ms[t]] = uw[n][:CHUNK, :HEAD_W]
            u_s[hh, 1, ms[t]] = uw[n][CHUNK:, :HEAD_W]
            wq_s[hh, 0, ms[t], :CHUNK, :] = uw[n][:CHUNK, HEAD_W:].astype(BF16)
            wq_s[hh, 1, ms[t], :CHUNK, :] = uw[n][CHUNK:, HEAD_W:].astype(BF16)

    _interleave(front(0, 0))

    def prep_body(i, carry):
        slot = i & 1
        _interleave(back(i, slot), front(i + 1, 1 - slot))
        return carry

    lax.fori_loop(0, n_grp - 1, prep_body, 0)
    _interleave(back(n_grp - 1, (n_grp - 1) & 1))

    chains = [(s, hh) for s in range(ns) for hh in range(hp)]
    for s, hh in chains:
        for d in range(2):
            s_s[s * hp + hh, d] = s0_ref[s, d, hh] if has_s0 else jnp.zeros((HEAD_W, HEAD_W), F32)

    def scan_body(first, i, carry):
        todo = [(s, hh, d, s * n_ch + (i if d == 0 else n_ch - 1 - i)) for s, hh in chains for d in range(2)]
        st = [s_s[s * hp + hh, d] for s, hh, d, m in todo]
        r = [jnp.dot(wq_s[hh, d, m], x.astype(BF16), preferred_element_type=F32)
             for (s, hh, d, m), x in zip(todo, st)]
        vn = []
        for (s, hh, d, m), rr in zip(todo, r):
            vn.append((u_s[hh, d, m] - rr[:CHUNK]).astype(BF16))
            vn_s[hh, d, m] = vn[-1]
        upd = []
        for (s, hh, d, m), x in zip(todo, vn):
            kdt = kdt_s[hh, m]
            zero = jnp.zeros_like(kdt)
            kdt = jnp.where(lane_f128, kdt, zero) if d == 0 else jnp.where(lane_f128, zero, kdt)
            upd.append(jnp.dot(kdt, jnp.concatenate([x, x], axis=0), preferred_element_type=F32))
        for (s, hh, d, m), x, rr, up in zip(todo, st, r, upd):
            s_s[s * hp + hh, d] = x * dl_s[hh, d, m][0:1, :] + up
            qs_s[hh, m] = rr[CHUNK:] if first else qs_s[hh, m] + rr[CHUNK:]
        return carry

    lax.fori_loop(0, n_ch // 2, functools.partial(scan_body, True), 0)
    lax.fori_loop(n_ch // 2, n_ch, functools.partial(scan_body, False), 0)
    if out_state:
        for s, hh in chains:
            for d in range(2):
                st_ref[s, d, hh] = s_s[s * hp + hh, d]

    dnn = dnn_ref[...]

    def out_body(g, carry):
        ms, ss, cs = group_chunks(g)
        for t, hh in units:
            m, s = ms[t], ss[t]
            r0 = pl.multiple_of(cs[t] * CHUNK, CHUNK)
            vst = jnp.concatenate([vn_s[hh, 0, m], vn_s[hh, 1, m]], axis=0)
            o = qs_s[hh, m] + jnp.dot(at_s[hh, m], vst, preferred_element_type=F32)
            y = o * lax.rsqrt(jnp.mean(o * o, axis=-1, keepdims=True) + EPS) * dnn
            o_ref[s, pl.ds(r0, CHUNK), cols(hh)] = (
                y * _silu(z_ref[s, pl.ds(r0, CHUNK), cols(hh)])).astype(o_ref.dtype)
        return carry

    lax.fori_loop(0, n_grp, out_body, 0)


def _dn_call(proj3, conv_w, alog_row, dtb_row, dn_norm, state_delta, layer, out_state):
    n_seq, seq_len, _ = proj3.shape
    n_ch = seq_len // CHUNK
    hp = DN_HEADS_PER_STEP
    ns = max(1, min(n_seq, DN_CHUNKS_PER_STEP // n_ch))
    assert n_seq % ns == 0
    tot = ns * n_ch
    nu = min(PREP_UNROLL, tot) * hp
    wid = hp * HEAD_W
    nblk = DN_HEADS // hp
    has_s0 = state_delta is not None
    tile = lambda off: pl.BlockSpec((ns, seq_len, wid), lambda b, h: (b, 0, off + h),
                                    pipeline_mode=pl.Buffered(1))
    cw = lambda off: pl.BlockSpec((CONV_K, wid), lambda b, h: (0, off + h))
    row = pl.BlockSpec((1, HEAD_W), lambda b, h: (0, 0))
    in_specs = [tile(0), tile(nblk), tile(2 * nblk), tile(3 * nblk),
                pl.BlockSpec((ns, seq_len, HEAD_W), lambda b, h: (b, 0, GATE_BLK),
                             pipeline_mode=pl.Buffered(1)),
                cw(0), cw(nblk), cw(2 * nblk), row, row, row]
    args = [proj3, proj3, proj3, proj3, proj3, conv_w, conv_w, conv_w, alog_row, dtb_row,
            dn_norm.reshape(1, HEAD_W)]
    if has_s0:
        in_specs.append(pl.BlockSpec((ns, None, 2, hp, HEAD_W, HEAD_W),
                                     lambda b, h: (b, layer, 0, h, 0, 0)))
        args.append(state_delta)
    out_shape = [jax.ShapeDtypeStruct((n_seq, seq_len, DN_W), BF16)]
    out_specs = [pl.BlockSpec((ns, seq_len, wid), lambda b, h: (b, 0, h))]
    if out_state:
        out_shape.append(jax.ShapeDtypeStruct((n_seq, 2, DN_HEADS, HEAD_W, HEAD_W), F32))
        out_specs.append(pl.BlockSpec((ns, 2, hp, HEAD_W, HEAD_W), lambda b, h: (b, 0, h, 0, 0)))
    scratch = [
        pltpu.VMEM((hp, 2, tot, 2 * CHUNK, HEAD_W), BF16),
        pltpu.VMEM((hp, 2, tot, CHUNK, HEAD_W), F32),
        pltpu.VMEM((hp, tot, HEAD_W, 2 * CHUNK), BF16),
        pltpu.VMEM((hp, 2, tot, 8, HEAD_W), F32),
        pltpu.VMEM((hp, tot, CHUNK, 2 * CHUNK), BF16),
        pltpu.VMEM((hp, 2, tot, CHUNK, HEAD_W), BF16),
        pltpu.VMEM((hp, tot, CHUNK, HEAD_W), F32),
        pltpu.VMEM((ns * hp, 2, HEAD_W, HEAD_W), F32),
        pltpu.VMEM((3 * nu, CHUNK + 16, HEAD_W), F32),
        pltpu.VMEM((2, nu, CHUNK, 2 * CHUNK), BF16),
        pltpu.VMEM((2, nu, 2 * CHUNK, 2 * HEAD_W), BF16),
    ]
    outs = pl.pallas_call(
        functools.partial(_dn_kernel, seq_len=seq_len, ns=ns, hp=hp, has_s0=has_s0, out_state=out_state),
        out_shape=out_shape,
        grid=(n_seq // ns, nblk),
        in_specs=in_specs,
        out_specs=out_specs,
        scratch_shapes=scratch,
        compiler_params=pltpu.CompilerParams(vmem_limit_bytes=V7X_VMEM_LIMIT),
        name="deltanet",
    )(*args)
    return (outs[0], outs[1]) if out_state else (outs[0], None)


def _split2(x):
    hi = x.astype(BF16)
    lo = (x - hi.astype(F32)).astype(BF16)
    return hi, lo


def _window(pos, win, n):
    lo = jnp.clip(pos - win // 2, 0, n)
    hi = jnp.clip(pos - win // 2 + win, 0, n)
    return lo, hi


def _pool_kernel(u_ref, pw_ref, ps_ref, o_ref, m_s, *, seq_len, rows):
    period = GRID_W if rows else seq_len
    tb = min(POOL_TB, seq_len)
    r_i = _iota((tb, tb), 0)
    c_i = _iota((tb, tb), 1)
    shift = period.bit_length() - 1
    same_line = (r_i >> shift) == (c_i >> shift)
    pos_r = r_i & (period - 1)
    pos_c = c_i & (period - 1)
    pos_col = _iota((tb, HEAD_W), 0) & (period - 1)
    for g, win in enumerate(POOL_WINDOWS):
        cols = slice(g * HEAD_W, (g + 1) * HEAD_W)
        if rows:
            run = None
            prev_lo = prev_hi = 0
            for r in range(rows):
                lo, hi = max(r - win // 2, 0), min(r - win // 2 + win, rows)
                for a in range(prev_hi, hi):
                    slab = u_ref[a * GRID_W:(a + 1) * GRID_W, cols]
                    run = slab if run is None else run + slab
                for a in range(prev_lo, lo):
                    run = run - u_ref[a * GRID_W:(a + 1) * GRID_W, cols]
                prev_lo, prev_hi = lo, hi
                m_s[r * GRID_W:(r + 1) * GRID_W, :] = run / float(hi - lo)
        lo_r, hi_r = _window(pos_r, win, period)
        band = jnp.where(same_line & (pos_c >= lo_r) & (pos_c < hi_r), 1.0, 0.0).astype(BF16)
        lo_c, hi_c = _window(pos_col, win, period)
        cnt = (hi_c - lo_c).astype(F32)
        pw = pw_ref[g]
        scale = ps_ref[:, cols]
        for t0 in range(0, seq_len, tb):
            ug = u_ref[t0:t0 + tb, cols]
            src = m_s[t0:t0 + tb, :] if rows else ug
            hi, lo = _split2(src)
            box = (jnp.dot(band, hi, preferred_element_type=F32)
                   + jnp.dot(band, lo, preferred_element_type=F32))
            mean = box / cnt
            o_ref[t0:t0 + tb, cols] = (_bdot(mean - ug, pw) * scale).astype(o_ref.dtype)


def _pool_call(proj3, pool_w_bf, pool_scale, rows):
    n_seq, seq_len, _ = proj3.shape
    return pl.pallas_call(
        functools.partial(_pool_kernel, seq_len=seq_len, rows=rows),
        out_shape=jax.ShapeDtypeStruct((n_seq, seq_len, POOL_W), BF16),
        grid=(n_seq,),
        in_specs=[
            pl.BlockSpec((None, seq_len, POOL_W), lambda b: (b, 0, POOL_BLK512)),
            pl.BlockSpec((POOL_GROUPS, HEAD_W, HEAD_W), lambda b: (0, 0, 0)),
            pl.BlockSpec((1, POOL_W), lambda b: (0, 0)),
        ],
        out_specs=pl.BlockSpec((None, seq_len, POOL_W), lambda b: (b, 0, 0)),
        scratch_shapes=[pltpu.VMEM((seq_len, HEAD_W), F32)],
        compiler_params=pltpu.CompilerParams(vmem_limit_bytes=V7X_VMEM_LIMIT),
        name="pool",
    )(proj3, pool_w_bf, pool_scale.reshape(1, POOL_W))


def _post_kernel(x_ref, o_ref, p_ref, ada_ref, nf_ref, fin_ref, wo_ref, wgu_ref, wd_ref, y_ref, *, final):
    mix = (jnp.dot(o_ref[...], wo_ref[:DN_W, :], preferred_element_type=F32)
           + jnp.dot(p_ref[...], wo_ref[DN_W:, :], preferred_element_type=F32))
    x = x_ref[...] + ada_ref[2:3, :] * mix
    h = _modulated_norm(x, nf_ref[...], ada_ref[3:4, :], ada_ref[4:5, :])
    gu = jnp.dot(h.astype(BF16), wgu_ref[...], preferred_element_type=F32)
    act = _silu(gu[:, :D_FF]) * gu[:, D_FF:]
    x = x + ada_ref[5:6, :] * jnp.dot(act.astype(BF16), wd_ref[...], preferred_element_type=F32)
    if final:
        x = x * lax.rsqrt(jnp.mean(x * x, axis=-1, keepdims=True) + EPS) * fin_ref[...]
    y_ref[...] = x


def _post_call(x2, o2, p2, ada_l, norm_ffn, final_norm, w_out_bf, w_gu_bf, w_down_bf, seq_len, per_seq, final):
    n_tok = x2.shape[0]
    tm = TM_POST
    const = lambda shape: pl.BlockSpec(shape, lambda i: (0, 0), pipeline_mode=pl.Buffered(1))
    return pl.pallas_call(
        functools.partial(_post_kernel, final=final),
        out_shape=jax.ShapeDtypeStruct((n_tok, D_MODEL), F32),
        grid=(n_tok // tm,),
        in_specs=[
            pl.BlockSpec((tm, D_MODEL), lambda i: (i, 0)),
            pl.BlockSpec((tm, DN_W), lambda i: (i, 0)),
            pl.BlockSpec((tm, POOL_W), lambda i: (i, 0)),
            pl.BlockSpec((None, 6, D_MODEL), _ada_row_map(seq_len, tm, per_seq)),
            pl.BlockSpec((1, D_MODEL), lambda i: (0, 0)),
            pl.BlockSpec((1, D_MODEL), lambda i: (0, 0)),
            const((D_MODEL, D_MODEL)),
            const((D_MODEL, 2 * D_FF)),
            const((D_FF, D_MODEL)),
        ],
        out_specs=pl.BlockSpec((tm, D_MODEL), lambda i: (i, 0)),
        compiler_params=pltpu.CompilerParams(vmem_limit_bytes=V7X_VMEM_LIMIT),
        name="post",
    )(x2, o2, p2, ada_l, norm_ffn.reshape(1, D_MODEL), final_norm.reshape(1, D_MODEL),
      w_out_bf, w_gu_bf, w_down_bf)


def _gate_row(p):
    one = jnp.concatenate([jnp.zeros((2 * DN_HEADS,), F32), p.reshape(-1)])
    return jnp.tile(one, HEAD_W // GATE_COLS).reshape(1, HEAD_W)


def _stream_layer(x2, seq_len, rows, per_seq, ada_l, lw, state_delta, layer, final):
    n_seq = x2.shape[0] // seq_len
    proj = _inproj_call(x2, ada_l, lw["norm_mix"], lw["w_in"], seq_len, per_seq)
    proj3 = proj.reshape(n_seq, seq_len, PROJ_W)
    o, st = _dn_call(proj3, lw["conv_w"], lw["alog_row"], lw["dtb_row"], lw["dn_norm"],
                     state_delta, layer, out_state=state_delta is None)
    p = _pool_call(proj3, lw["pool_w"], lw["pool_scale"], rows)
    x2 = _post_call(x2, o.reshape(-1, DN_W), p.reshape(-1, POOL_W), ada_l, lw["norm_ffn"], lw["final_norm"],
                    lw["w_out"], lw["w_gu"], lw["w_down"], seq_len, per_seq, final)
    return x2, st


def kernel(x_prompt, x_sample, c, state_delta, c_ctx, w_ada, b_ada, norm_mix, norm_ffn, w_in, conv_w, a_log, dt_bias, dn_norm, pool_w, pool_scale, w_out, w_gu, w_down, final_norm):
    depth = w_ada.shape[0]
    n_ctx, ctx_len, _ = x_prompt.shape
    n_lat, lat_len, _ = x_sample.shape
    assert 1 + n_lat <= ADA_ROWS
    c_all = jnp.concatenate([c_ctx[None, :], c, jnp.zeros((ADA_ROWS - 1 - n_lat, D_MODEL), F32)], axis=0)
    ada = _ada_call(c_all, w_ada, b_ada).reshape(depth, ADA_ROWS, 6, D_MODEL)

    qkvz = 4 * DN_W
    n_gate = 4 * DN_HEADS
    xp = x_prompt.reshape(-1, D_MODEL)
    xs = x_sample.reshape(-1, D_MODEL)
    states = []
    for l in range(depth):
        w_in_p = jnp.concatenate(
            [w_in[l][:, :qkvz], w_in[l][:, qkvz + n_gate:], w_in[l][:, qkvz:qkvz + n_gate],
             jnp.zeros((D_MODEL, HEAD_W - n_gate), F32)], axis=1).astype(BF16)
        lw = dict(norm_mix=norm_mix[l], norm_ffn=norm_ffn[l], w_in=w_in_p, conv_w=conv_w[l],
                  alog_row=_gate_row(a_log[l]), dtb_row=_gate_row(dt_bias[l]), dn_norm=dn_norm[l],
                  pool_w=pool_w[l].astype(BF16), pool_scale=pool_scale[l], w_out=w_out[l].astype(BF16),
                  w_gu=w_gu[l].astype(BF16), w_down=w_down[l].astype(BF16), final_norm=final_norm)
        final = l == depth - 1
        xp, st = _stream_layer(xp, ctx_len, None, False, ada[l], lw, None, l, final)
        states.append(st)
        xs, _ = _stream_layer(xs, lat_len, lat_len // GRID_W, True, ada[l], lw, state_delta, l, final)
    new_state = jnp.stack(states, axis=1)
    return (xp.reshape(x_prompt.shape), xs.reshape(x_sample.shape), new_state)
```

```python
import functools

import jax
import jax.numpy as jnp
from jax import lax
from jax.experimental import pallas as pl
from jax.experimental.pallas import tpu as pltpu

F32 = jnp.float32
BF16 = jnp.bfloat16

D_MODEL = 1024
DN_HEADS = 4
HEAD_W = 128
DN_W = DN_HEADS * HEAD_W
POOL_GROUPS = 4
POOL_W = POOL_GROUPS * HEAD_W
POOL_WINDOWS = (2, 4, 8, 16)
GRID_W = 64
CONV_K = 5
CHUNK = 64
D_FF = 2816
EPS = 1e-6

PROJ_W = 3 * DN_W + DN_W + POOL_W + HEAD_W
POOL_BLK512 = 4
GATE_BLK = 20
GATE_COLS = 4 * DN_HEADS
ADA_ROWS = 16

V7X_VMEM_LIMIT = 56 * 1024 * 1024

TM_INPROJ = 512
TM_POST = 512
POOL_TB = 256
PREP_UNROLL = 8
DN_HEADS_PER_STEP = 2
YIELD_EVERY = 16
DN_CHUNKS_PER_STEP = 32


def _sigmoid(x):
    return 0.5 * jnp.tanh(0.5 * x) + 0.5


def _silu(x):
    return x * _sigmoid(x)


def _softplus(x):
    return jnp.maximum(x, 0.0) + jnp.log(1.0 + jnp.exp(-jnp.abs(x)))


def _bdot(a, b):
    return jnp.dot(a.astype(BF16), b.astype(BF16), preferred_element_type=F32)


def _iota(shape, dim):
    return lax.broadcasted_iota(jnp.int32, shape, dim)


def _ada_kernel(c_ref, w_ref, b_ref, o_ref):
    o_ref[...] = jnp.dot(_silu(c_ref[...]), w_ref[...], preferred_element_type=F32,
                         precision=lax.Precision.HIGHEST) + b_ref[...]


def _ada_call(c_all, w_ada, b_ada):
    depth, _, n_out = w_ada.shape
    tn = 1536
    return pl.pallas_call(
        _ada_kernel,
        out_shape=jax.ShapeDtypeStruct((depth, ADA_ROWS, n_out), F32),
        grid=(depth, n_out // tn),
        in_specs=[
            pl.BlockSpec((ADA_ROWS, D_MODEL), lambda l, j: (0, 0)),
            pl.BlockSpec((None, D_MODEL, tn), lambda l, j: (l, 0, j)),
            pl.BlockSpec((None, 1, tn), lambda l, j: (l, 0, j)),
        ],
        out_specs=pl.BlockSpec((None, ADA_ROWS, tn), lambda l, j: (l, 0, j)),
        compiler_params=pltpu.CompilerParams(vmem_limit_bytes=V7X_VMEM_LIMIT),
        name="ada",
    )(c_all, w_ada, b_ada.reshape(depth, 1, n_out))


def _modulated_norm(x, gain, shift, scale):
    y = x * lax.rsqrt(jnp.mean(x * x, axis=-1, keepdims=True) + EPS) * gain
    return y * (1.0 + scale) + shift


def _inproj_kernel(x_ref, ada_ref, nw_ref, w_ref, o_ref):
    h = _modulated_norm(x_ref[...], nw_ref[...], ada_ref[0:1, :], ada_ref[1:2, :])
    o_ref[...] = jnp.dot(h.astype(BF16), w_ref[...], preferred_element_type=F32)


def _ada_row_map(seq_len, tm, per_seq):
    if not per_seq:
        return lambda i: (0, 0, 0)
    return lambda i: (1 + (i * tm) // seq_len, 0, 0)


def _inproj_call(x2, ada_l, norm_w, w_in_p, seq_len, per_seq):
    n_tok = x2.shape[0]
    tm = TM_INPROJ
    return pl.pallas_call(
        _inproj_kernel,
        out_shape=jax.ShapeDtypeStruct((n_tok, PROJ_W), F32),
        grid=(n_tok // tm,),
        in_specs=[
            pl.BlockSpec((tm, D_MODEL), lambda i: (i, 0)),
            pl.BlockSpec((None, 6, D_MODEL), _ada_row_map(seq_len, tm, per_seq)),
            pl.BlockSpec((1, D_MODEL), lambda i: (0, 0)),
            pl.BlockSpec((D_MODEL, PROJ_W), lambda i: (0, 0), pipeline_mode=pl.Buffered(1)),
        ],
        out_specs=pl.BlockSpec((tm, PROJ_W), lambda i: (i, 0)),
        compiler_params=pltpu.CompilerParams(vmem_limit_bytes=V7X_VMEM_LIMIT),
        name="inproj",
    )(x2, ada_l, norm_w.reshape(1, D_MODEL), w_in_p)


def _col_bcast(x, lane, idx):
    col = jnp.sum(jnp.where(lane == idx, x, 0.0), axis=-1, keepdims=True)
    return jnp.broadcast_to(col, x.shape)


def _each(fn, *lists):
    return [fn(*args) for args in zip(*lists)]


def _each_y(fn, *lists, every=YIELD_EVERY):
    out = []
    for i, args in enumerate(zip(*lists)):
        out.append(fn(*args))
        if i % every == every - 1:
            yield
    return out


def _interleave(*gens):
    gens = list(gens)
    while gens:
        for g in list(gens):
            try:
                next(g)
            except StopIteration:
                gens.remove(g)


def _dn_kernel(*refs, seq_len, ns, hp, has_s0, out_state):
    n_ch = seq_len // CHUNK
    gc = min(PREP_UNROLL, ns * n_ch)
    n_grp = ns * n_ch // gc
    assert gc * GATE_COLS <= HEAD_W
    units = [(t, hh) for t in range(gc) for hh in range(hp)]
    nu = len(units)
    it = iter(refs)
    q_ref, k_ref, v_ref, z_ref, g_ref = (next(it) for _ in range(5))
    cwq_ref, cwk_ref, cwv_ref = (next(it) for _ in range(3))
    alog_ref, dtb_ref, dnn_ref = (next(it) for _ in range(3))
    s0_ref = next(it) if has_s0 else None
    o_ref = next(it)
    st_ref = next(it) if out_state else None
    wq_s, u_s, kdt_s, dl_s, at_s, vn_s, qs_s, s_s, ext_s, lbd_s, rhs_s = (next(it) for _ in range(11))

    head0 = pl.program_id(1) * hp
    row = _iota((CHUNK, 128), 0)
    lane = _iota((CHUNK, 128), 1)
    is_f = lane < CHUNK
    j = lane & (CHUNK - 1)
    incl = (is_f & (row >= j)) | (~is_f & (row <= j))
    strict = (is_f & (row > j)) | (~is_f & (row < j))
    diag = row == j
    eye_p = jnp.where(diag, 1.0, 0.0).astype(F32)
    blk16p = (row >> 4) == (j >> 4)
    r128 = _iota((128, 128), 0)
    c128 = _iota((128, 128), 1)
    lane_f128 = c128 < CHUNK
    cum_sel = ((r128 < CHUNK) & (c128 <= r128)) | ((r128 >= CHUNK) & (c128 >= r128 - CHUNK))
    cum_mat = jnp.where(cum_sel, 1.0, 0.0).astype(F32)[:, :CHUNK]

    neg_a = -jnp.exp(alog_ref[...])
    dtb = dtb_ref[...]
    cols = lambda hh: slice(hh * HEAD_W, (hh + 1) * HEAD_W)

    def group_chunks(g):
        ms = [g * gc + t for t in range(gc)]
        return ms, [m // n_ch for m in ms], [m % n_ch for m in ms]

    def conv_silu(ref, cw_ref, hh, s, c, r0, slot):
        p0 = pl.multiple_of(jnp.maximum(r0 - 8, 0), 8)
        n0 = pl.multiple_of(jnp.minimum(r0 + CHUNK, seq_len - 8), 8)
        ext_s[slot, 0:8, :] = jnp.where(c > 0, ref[s, pl.ds(p0, 8), cols(hh)], 0.0)
        ext_s[slot, 8:8 + CHUNK, :] = ref[s, pl.ds(r0, CHUNK), cols(hh)]
        ext_s[slot, 8 + CHUNK:16 + CHUNK, :] = jnp.where(c < n_ch - 1, ref[s, pl.ds(n0, 8), cols(hh)], 0.0)
        acc = ext_s[slot, 6:6 + CHUNK, :] * cw_ref[0:1, cols(hh)]
        for t in range(1, CONV_K):
            acc = acc + ext_s[slot, 6 + t:6 + t + CHUNK, :] * cw_ref[t:t + 1, cols(hh)]
        return _silu(acc)

    def l2n(x):
        return x * lax.rsqrt(jnp.sum(x * x, axis=-1, keepdims=True) + EPS)

    def gates(ss, r0s):
        packed = g_ref[ss[0], pl.ds(r0s[0], CHUNK), :]
        for t in range(1, gc):
            packed = packed + pltpu.roll(g_ref[ss[t], pl.ds(r0s[t], CHUNK), :], GATE_COLS * t, axis=1)
        is_beta = (lane & (GATE_COLS - 1)) < 2 * DN_HEADS
        gact = jnp.where(is_beta, _sigmoid(packed), neg_a * _softplus(packed + dtb))
        cums = jnp.dot(cum_mat, gact, preferred_element_type=F32, precision=lax.Precision.HIGHEST)
        e_cum = jnp.exp(cums)
        e_rest = jnp.concatenate([jnp.exp(cums[CHUNK - 1:CHUNK] - cums[:CHUNK]),
                                  jnp.exp(cums[CHUNK:CHUNK + 1] - cums[CHUNK:])], axis=0)
        return gact, cums, e_cum, e_rest

    def front(g, slot):
        ms, ss, cs = group_chunks(g)
        r0s = [pl.multiple_of(c * CHUNK, CHUNK) for c in cs]
        gact, cums, e_cum, e_rest = gates(ss, r0s)
        yield
        q, k, v = [], [], []
        for n, (t, hh) in enumerate(units):
            q.append(l2n(conv_silu(q_ref, cwq_ref, hh, ss[t], cs[t], r0s[t], n)) * (HEAD_W ** -0.5))
            k.append(l2n(conv_silu(k_ref, cwk_ref, hh, ss[t], cs[t], r0s[t], nu + n)))
            v.append(conv_silu(v_ref, cwv_ref, hh, ss[t], cs[t], r0s[t], 2 * nu + n))
            yield
        col = lambda x, which: (lambda u: _col_bcast(
            x, lane, GATE_COLS * u[0] + which * DN_HEADS + head0 + u[1]))
        beta_f = yield from _each_y(col(gact, 0), units)
        beta_b = yield from _each_y(col(gact, 1), units)
        gc_f = yield from _each_y(col(cums[:CHUNK], 2), units)
        gc_b = yield from _each_y(col(cums[CHUNK:], 3), units)
        e_f = yield from _each_y(col(e_cum[:CHUNK], 2), units)
        e_b = yield from _each_y(col(e_cum[CHUNK:], 3), units)
        rest_f = yield from _each_y(col(e_rest[:CHUNK], 2), units)
        rest_b = yield from _each_y(col(e_rest[CHUNK:], 3), units)

        def decay_of(gf, gb):
            gc_col = jnp.where(is_f, gf, gb)
            gc_row = jnp.sum(jnp.where(diag, gc_col, 0.0), axis=0, keepdims=True)
            return jnp.where(incl, jnp.exp(jnp.where(incl, gc_col - gc_row, 0.0)), 0.0)

        decay = yield from _each_y(decay_of, gc_f, gc_b, every=2)

        def gram_of(qq, kk):
            kb = kk.astype(BF16)
            return lax.dot_general(jnp.concatenate([qq.astype(BF16), kb], axis=0),
                                   jnp.concatenate([kb, kb], axis=0),
                                   (((1,), (1,)), ((), ())), preferred_element_type=F32)

        gram = yield from _each_y(gram_of, q, k, every=2)
        for n, (t, hh) in enumerate(units):
            l2 = jnp.where(strict, jnp.where(is_f, beta_f[n], beta_b[n]) * gram[n][CHUNK:] * decay[n], 0.0)
            lbd_s[slot, n] = l2.astype(BF16)
            at_s[hh, ms[t]] = (gram[n][:CHUNK] * decay[n]).astype(BF16)
            if n % 2 == 1:
                yield
        for n, (t, hh) in enumerate(units):
            rhs_s[slot, n] = jnp.concatenate([
                jnp.concatenate([v[n] * beta_f[n], k[n] * beta_f[n] * e_f[n]], axis=1),
                jnp.concatenate([v[n] * beta_b[n], k[n] * beta_b[n] * e_b[n]], axis=1)],
                axis=0).astype(BF16)
            wq_s[hh, 0, ms[t], CHUNK:, :] = (q[n] * e_f[n]).astype(BF16)
            wq_s[hh, 1, ms[t], CHUNK:, :] = (q[n] * e_b[n]).astype(BF16)
            if n % 2 == 1:
                yield
        for n, (t, hh) in enumerate(units):
            kdec = jnp.concatenate([k[n] * rest_f[n], k[n] * rest_b[n]], axis=0)
            kdt_s[hh, ms[t]] = kdec.T.astype(BF16)
            dl_s[hh, 0, ms[t]] = jnp.broadcast_to(e_f[n][CHUNK - 1:CHUNK, :], (8, 128))
            dl_s[hh, 1, ms[t]] = jnp.broadcast_to(e_b[n][0:1, :], (8, 128))
            if n % 2 == 1:
                yield

    def back(g, slot):
        ms = [g * gc + t for t in range(gc)]
        l2 = [lbd_s[slot, n] for n in range(nu)]
        zero = jnp.zeros((CHUNK, 128), BF16)

        def bd(x):
            x = x.astype(BF16)
            return jnp.concatenate([jnp.where(is_f, x, zero), jnp.where(is_f, zero, x)], axis=0)

        pdot = lambda a, b_bd: jnp.dot(a.astype(BF16), b_bd, preferred_element_type=F32)
        d = _each(lambda x: jnp.where(blk16p, x, zero), l2)
        lo_bd = _each(lambda x: bd(jnp.where(blk16p, zero, x)), l2)
        d2 = yield from _each_y(lambda a: pdot(a, bd(a)), d)
        d2_bd = _each(bd, d2)
        d3 = yield from _each_y(pdot, d, d2_bd)
        d4 = yield from _each_y(pdot, d2, d2_bd)
        d4_bd = _each(bd, d4)
        d8 = yield from _each_y(pdot, d4, d4_bd)
        p1 = _each(lambda a, b, c: eye_p - a.astype(F32) + b - c, d, d2, d3)
        p2 = yield from _each_y(lambda p, x: p + pdot(p, x), p1, d4_bd)
        dinv = yield from _each_y(lambda p, x: p + pdot(p, bd(x)), p2, d8)
        m = yield from _each_y(pdot, dinv, lo_bd)
        m_bd = _each(bd, m)
        m2 = yield from _each_y(pdot, m, m_bd)
        m3 = yield from _each_y(lambda a, b: pdot(a, bd(b)), m, m2)
        ninv = _each(lambda a, b, c: eye_p - a + b - c, m, m2, m3)
        t_inv = yield from _each_y(lambda a, b: pdot(a, bd(b)), ninv, dinv)
        uw = yield from _each_y(lambda tt, n: jnp.dot(bd(tt), rhs_s[slot, n], preferred_element_type=F32),
                                t_inv, range(nu), every=2)
        for n, (t, hh) in enumerate(units):
            u_s[hh, 0, ms[t]] = uw[n][:CHUNK, :HEAD_W]
            u_s[hh, 1, ms[t]] = uw[n][CHUNK:, :HEAD_W]
            wq_s[hh, 0, ms[t], :CHUNK, :] = uw[n][:CHUNK, HEAD_W:].astype(BF16)
            wq_s[hh, 1, ms[t], :CHUNK, :] = uw[n][CHUNK:, HEAD_W:].astype(BF16)

    _interleave(front(0, 0))

    def prep_body(i, carry):
        slot = i & 1
        _interleave(back(i, slot), front(i + 1, 1 - slot))
        return carry

    lax.fori_loop(0, n_grp - 1, prep_body, 0)
    _interleave(back(n_grp - 1, (n_grp - 1) & 1))

    chains = [(s, hh) for s in range(ns) for hh in range(hp)]
    for s, hh in chains:
        for d in range(2):
            s_s[s * hp + hh, d] = s0_ref[s, d, hh] if has_s0 else jnp.zeros((HEAD_W, HEAD_W), F32)

    dnn = dnn_ref[...]

    def emit_out(c):
        r0 = c * CHUNK if isinstance(c, int) else pl.multiple_of(c * CHUNK, CHUNK)
        for s, hh in chains:
            m = s * n_ch + c
            vst = jnp.concatenate([vn_s[hh, 0, m], vn_s[hh, 1, m]], axis=0)
            o = qs_s[hh, m] + jnp.dot(at_s[hh, m], vst, preferred_element_type=F32)
            y = o * lax.rsqrt(jnp.mean(o * o, axis=-1, keepdims=True) + EPS) * dnn
            o_ref[s, pl.ds(r0, CHUNK), cols(hh)] = (
                y * _silu(z_ref[s, pl.ds(r0, CHUNK), cols(hh)])).astype(o_ref.dtype)

    def scan_body(first, with_out, i, carry):
        if with_out:
            emit_out(i - 1)
            emit_out(n_ch - i)
        todo = [(s, hh, d, s * n_ch + (i if d == 0 else n_ch - 1 - i)) for s, hh in chains for d in range(2)]
        st = [s_s[s * hp + hh, d] for s, hh, d, m in todo]
        r = [jnp.dot(wq_s[hh, d, m], x.astype(BF16), preferred_element_type=F32)
             for (s, hh, d, m), x in zip(todo, st)]
        vn = []
        for (s, hh, d, m), rr in zip(todo, r):
            vn.append((u_s[hh, d, m] - rr[:CHUNK]).astype(BF16))
            vn_s[hh, d, m] = vn[-1]
        upd = []
        for (s, hh, d, m), x in zip(todo, vn):
            kdt = kdt_s[hh, m]
            zero = jnp.zeros_like(kdt)
            kdt = jnp.where(lane_f128, kdt, zero) if d == 0 else jnp.where(lane_f128, zero, kdt)
            upd.append(jnp.dot(kdt, jnp.concatenate([x, x], axis=0), preferred_element_type=F32))
        for (s, hh, d, m), x, rr, up in zip(todo, st, r, upd):
            s_s[s * hp + hh, d] = x * dl_s[hh, d, m][0:1, :] + up
            qs_s[hh, m] = rr[CHUNK:] if first else qs_s[hh, m] + rr[CHUNK:]
        return carry

    half = n_ch // 2
    lax.fori_loop(0, half, functools.partial(scan_body, True, False), 0)
    scan_body(False, False, half, 0)
    lax.fori_loop(half + 1, n_ch, functools.partial(scan_body, False, True), 0)
    emit_out(n_ch - 1)
    emit_out(0)
    if out_state:
        for s, hh in chains:
            for d in range(2):
                st_ref[s, d, hh] = s_s[s * hp + hh, d]


def _dn_call(proj3, conv_w, alog_row, dtb_row, dn_norm, state_delta, layer, out_state):
    n_seq, seq_len, _ = proj3.shape
    n_ch = seq_len // CHUNK
    hp = DN_HEADS_PER_STEP
    ns = max(1, min(n_seq, DN_CHUNKS_PER_STEP // n_ch))
    assert n_seq % ns == 0
    tot = ns * n_ch
    nu = min(PREP_UNROLL, tot) * hp
    wid = hp * HEAD_W
    nblk = DN_HEADS // hp
    has_s0 = state_delta is not None
    act_mode = dict(pipeline_mode=pl.Buffered(1)) if out_state else {}
    tile = lambda off: pl.BlockSpec((ns, seq_len, wid), lambda b, h: (b, 0, off + h), **act_mode)
    cw = lambda off: pl.BlockSpec((CONV_K, wid), lambda b, h: (0, off + h))
    row = pl.BlockSpec((1, HEAD_W), lambda b, h: (0, 0))
    in_specs = [tile(0), tile(nblk), tile(2 * nblk), tile(3 * nblk),
                pl.BlockSpec((ns, seq_len, HEAD_W), lambda b, h: (b, 0, GATE_BLK), **act_mode),
                cw(0), cw(nblk), cw(2 * nblk), row, row, row]
    args = [proj3, proj3, proj3, proj3, proj3, conv_w, conv_w, conv_w, alog_row, dtb_row,
            dn_norm.reshape(1, HEAD_W)]
    if has_s0:
        in_specs.append(pl.BlockSpec((ns, None, 2, hp, HEAD_W, HEAD_W),
                                     lambda b, h: (b, layer, 0, h, 0, 0)))
        args.append(state_delta)
    out_shape = [jax.ShapeDtypeStruct((n_seq, seq_len, DN_W), BF16)]
    out_specs = [pl.BlockSpec((ns, seq_len, wid), lambda b, h: (b, 0, h))]
    if out_state:
        out_shape.append(jax.ShapeDtypeStruct((n_seq, 2, DN_HEADS, HEAD_W, HEAD_W), F32))
        out_specs.append(pl.BlockSpec((ns, 2, hp, HEAD_W, HEAD_W), lambda b, h: (b, 0, h, 0, 0)))
    scratch = [
        pltpu.VMEM((hp, 2, tot, 2 * CHUNK, HEAD_W), BF16),
        pltpu.VMEM((hp, 2, tot, CHUNK, HEAD_W), F32),
        pltpu.VMEM((hp, tot, HEAD_W, 2 * CHUNK), BF16),
        pltpu.VMEM((hp, 2, tot, 8, HEAD_W), F32),
        pltpu.VMEM((hp, tot, CHUNK, 2 * CHUNK), BF16),
        pltpu.VMEM((hp, 2, tot, CHUNK, HEAD_W), BF16),
        pltpu.VMEM((hp, tot, CHUNK, HEAD_W), F32),
        pltpu.VMEM((ns * hp, 2, HEAD_W, HEAD_W), F32),
        pltpu.VMEM((3 * nu, CHUNK + 16, HEAD_W), F32),
        pltpu.VMEM((2, nu, CHUNK, 2 * CHUNK), BF16),
        pltpu.VMEM((2, nu, 2 * CHUNK, 2 * HEAD_W), BF16),
    ]
    outs = pl.pallas_call(
        functools.partial(_dn_kernel, seq_len=seq_len, ns=ns, hp=hp, has_s0=has_s0, out_state=out_state),
        out_shape=out_shape,
        grid=(n_seq // ns, nblk),
        in_specs=in_specs,
        out_specs=out_specs,
        scratch_shapes=scratch,
        compiler_params=pltpu.CompilerParams(vmem_limit_bytes=V7X_VMEM_LIMIT),
        name="deltanet",
    )(*args)
    return (outs[0], outs[1]) if out_state else (outs[0], None)


def _split2(x):
    hi = x.astype(BF16)
    lo = (x - hi.astype(F32)).astype(BF16)
    return hi, lo


def _window(pos, win, n):
    lo = jnp.clip(pos - win // 2, 0, n)
    hi = jnp.clip(pos - win // 2 + win, 0, n)
    return lo, hi


def _pool_kernel(u_ref, pw_ref, ps_ref, o_ref, m_s, *, seq_len, rows):
    period = GRID_W if rows else seq_len
    tb = min(POOL_TB, seq_len)
    r_i = _iota((tb, tb), 0)
    c_i = _iota((tb, tb), 1)
    shift = period.bit_length() - 1
    same_line = (r_i >> shift) == (c_i >> shift)
    pos_r = r_i & (period - 1)
    pos_c = c_i & (period - 1)
    pos_col = _iota((tb, HEAD_W), 0) & (period - 1)
    for g, win in enumerate(POOL_WINDOWS):
        cols = slice(g * HEAD_W, (g + 1) * HEAD_W)
        if rows:
            run = None
            prev_lo = prev_hi = 0
            for r in range(rows):
                lo, hi = max(r - win // 2, 0), min(r - win // 2 + win, rows)
                for a in range(prev_hi, hi):
                    slab = u_ref[a * GRID_W:(a + 1) * GRID_W, cols]
                    run = slab if run is None else run + slab
                for a in range(prev_lo, lo):
                    run = run - u_ref[a * GRID_W:(a + 1) * GRID_W, cols]
                prev_lo, prev_hi = lo, hi
                m_s[r * GRID_W:(r + 1) * GRID_W, :] = run / float(hi - lo)
        lo_r, hi_r = _window(pos_r, win, period)
        band = jnp.where(same_line & (pos_c >= lo_r) & (pos_c < hi_r), 1.0, 0.0).astype(BF16)
        lo_c, hi_c = _window(pos_col, win, period)
        cnt = (hi_c - lo_c).astype(F32)
        pw = pw_ref[g]
        scale = ps_ref[:, cols]
        for t0 in range(0, seq_len, tb):
            ug = u_ref[t0:t0 + tb, cols]
            src = m_s[t0:t0 + tb, :] if rows else ug
            hi, lo = _split2(src)
            box = (jnp.dot(band, hi, preferred_element_type=F32)
                   + jnp.dot(band, lo, preferred_element_type=F32))
            mean = box / cnt
            o_ref[t0:t0 + tb, cols] = (_bdot(mean - ug, pw) * scale).astype(o_ref.dtype)


def _pool_call(proj3, pool_w_bf, pool_scale, rows):
    n_seq, seq_len, _ = proj3.shape
    return pl.pallas_call(
        functools.partial(_pool_kernel, seq_len=seq_len, rows=rows),
        out_shape=jax.ShapeDtypeStruct((n_seq, seq_len, POOL_W), BF16),
        grid=(n_seq,),
        in_specs=[
            pl.BlockSpec((None, seq_len, POOL_W), lambda b: (b, 0, POOL_BLK512)),
            pl.BlockSpec((POOL_GROUPS, HEAD_W, HEAD_W), lambda b: (0, 0, 0)),
            pl.BlockSpec((1, POOL_W), lambda b: (0, 0)),
        ],
        out_specs=pl.BlockSpec((None, seq_len, POOL_W), lambda b: (b, 0, 0)),
        scratch_shapes=[pltpu.VMEM((seq_len, HEAD_W), F32)],
        compiler_params=pltpu.CompilerParams(vmem_limit_bytes=V7X_VMEM_LIMIT),
        name="pool",
    )(proj3, pool_w_bf, pool_scale.reshape(1, POOL_W))


def _post_kernel(x_ref, o_ref, p_ref, ada_ref, nf_ref, fin_ref, wo_ref, wgu_ref, wd_ref, y_ref, *, final):
    mix = (jnp.dot(o_ref[...], wo_ref[:DN_W, :], preferred_element_type=F32)
           + jnp.dot(p_ref[...], wo_ref[DN_W:, :], preferred_element_type=F32))
    x = x_ref[...] + ada_ref[2:3, :] * mix
    h = _modulated_norm(x, nf_ref[...], ada_ref[3:4, :], ada_ref[4:5, :])
    gu = jnp.dot(h.astype(BF16), wgu_ref[...], preferred_element_type=F32)
    act = _silu(gu[:, :D_FF]) * gu[:, D_FF:]
    x = x + ada_ref[5:6, :] * jnp.dot(act.astype(BF16), wd_ref[...], preferred_element_type=F32)
    if final:
        x = x * lax.rsqrt(jnp.mean(x * x, axis=-1, keepdims=True) + EPS) * fin_ref[...]
    y_ref[...] = x


def _post_call(x2, o2, p2, ada_l, norm_ffn, final_norm, w_out_bf, w_gu_bf, w_down_bf, seq_len, per_seq, final):
    n_tok = x2.shape[0]
    tm = TM_POST
    const = lambda shape: pl.BlockSpec(shape, lambda i: (0, 0), pipeline_mode=pl.Buffered(1))
    return pl.pallas_call(
        functools.partial(_post_kernel, final=final),
        out_shape=jax.ShapeDtypeStruct((n_tok, D_MODEL), F32),
        grid=(n_tok // tm,),
        in_specs=[
            pl.BlockSpec((tm, D_MODEL), lambda i: (i, 0)),
            pl.BlockSpec((tm, DN_W), lambda i: (i, 0)),
            pl.BlockSpec((tm, POOL_W), lambda i: (i, 0)),
            pl.BlockSpec((None, 6, D_MODEL), _ada_row_map(seq_len, tm, per_seq)),
            pl.BlockSpec((1, D_MODEL), lambda i: (0, 0)),
            pl.BlockSpec((1, D_MODEL), lambda i: (0, 0)),
            const((D_MODEL, D_MODEL)),
            const((D_MODEL, 2 * D_FF)),
            const((D_FF, D_MODEL)),
        ],
        out_specs=pl.BlockSpec((tm, D_MODEL), lambda i: (i, 0)),
        compiler_params=pltpu.CompilerParams(vmem_limit_bytes=V7X_VMEM_LIMIT),
        name="post",
    )(x2, o2, p2, ada_l, norm_ffn.reshape(1, D_MODEL), final_norm.reshape(1, D_MODEL),
      w_out_bf, w_gu_bf, w_down_bf)


def _gate_row(p):
    one = jnp.concatenate([jnp.zeros((2 * DN_HEADS,), F32), p.reshape(-1)])
    return jnp.tile(one, HEAD_W // GATE_COLS).reshape(1, HEAD_W)


def _stream_layer(x2, seq_len, rows, per_seq, ada_l, lw, state_delta, layer, final):
    n_seq = x2.shape[0] // seq_len
    proj = _inproj_call(x2, ada_l, lw["norm_mix"], lw["w_in"], seq_len, per_seq)
    proj3 = proj.reshape(n_seq, seq_len, PROJ_W)
    o, st = _dn_call(proj3, lw["conv_w"], lw["alog_row"], lw["dtb_row"], lw["dn_norm"],
                     state_delta, layer, out_state=state_delta is None)
    p = _pool_call(proj3, lw["pool_w"], lw["pool_scale"], rows)
    x2 = _post_call(x2, o.reshape(-1, DN_W), p.reshape(-1, POOL_W), ada_l, lw["norm_ffn"], lw["final_norm"],
                    lw["w_out"], lw["w_gu"], lw["w_down"], seq_len, per_seq, final)
    return x2, st


def kernel(x_prompt, x_sample, c, state_delta, c_ctx, w_ada, b_ada, norm_mix, norm_ffn, w_in, conv_w, a_log, dt_bias, dn_norm, pool_w, pool_scale, w_out, w_gu, w_down, final_norm):
    depth = w_ada.shape[0]
    n_ctx, ctx_len, _ = x_prompt.shape
    n_lat, lat_len, _ = x_sample.shape
    assert 1 + n_lat <= ADA_ROWS
    c_all = jnp.concatenate([c_ctx[None, :], c, jnp.zeros((ADA_ROWS - 1 - n_lat, D_MODEL), F32)], axis=0)
    ada = _ada_call(c_all, w_ada, b_ada).reshape(depth, ADA_ROWS, 6, D_MODEL)

    qkvz = 4 * DN_W
    n_gate = 4 * DN_HEADS
    xp = x_prompt.reshape(-1, D_MODEL)
    xs = x_sample.reshape(-1, D_MODEL)
    states = []
    for l in range(depth):
        w_in_p = jnp.concatenate(
            [w_in[l][:, :qkvz], w_in[l][:, qkvz + n_gate:], w_in[l][:, qkvz:qkvz + n_gate],
             jnp.zeros((D_MODEL, HEAD_W - n_gate), F32)], axis=1).astype(BF16)
        lw = dict(norm_mix=norm_mix[l], norm_ffn=norm_ffn[l], w_in=w_in_p, conv_w=conv_w[l],
                  alog_row=_gate_row(a_log[l]), dtb_row=_gate_row(dt_bias[l]), dn_norm=dn_norm[l],
                  pool_w=pool_w[l].astype(BF16), pool_scale=pool_scale[l], w_out=w_out[l].astype(BF16),
                  w_gu=w_gu[l].astype(BF16), w_down=w_down[l].astype(BF16), final_norm=final_norm)
        final = l == depth - 1
        xp, st = _stream_layer(xp, ctx_len, None, False, ada[l], lw, None, l, final)
        states.append(st)
        xs, _ = _stream_layer(xs, lat_len, lat_len // GRID_W, True, ada[l], lw, state_delta, l, final)
    new_state = jnp.stack(states, axis=1)
    return (xp.reshape(x_prompt.shape), xs.reshape(x_sample.shape), new_state)
```

```python
import functools

import jax
import jax.numpy as jnp
from jax import lax
from jax.experimental import pallas as pl
from jax.experimental.pallas import tpu as pltpu

F32 = jnp.float32
BF16 = jnp.bfloat16

D_MODEL = 1024
DN_HEADS = 4
HEAD_W = 128
DN_W = DN_HEADS * HEAD_W
POOL_GROUPS = 4
POOL_W = POOL_GROUPS * HEAD_W
POOL_WINDOWS = (2, 4, 8, 16)
GRID_W = 64
CONV_K = 5
CHUNK = 64
D_FF = 2816
EPS = 1e-6

PROJ_W = 3 * DN_W + DN_W + POOL_W + HEAD_W
POOL_BLK512 = 4
GATE_BLK = 20
GATE_COLS = 4 * DN_HEADS
ADA_ROWS = 16

V7X_VMEM_LIMIT = 56 * 1024 * 1024

TM_INPROJ = 512
TM_POST = 512
POOL_TB = 256
PREP_UNROLL = 8
DN_HEADS_PER_STEP = 2
YIELD_EVERY = 16
DN_CHUNKS_PER_STEP = 32


def _sigmoid(x):
    return 0.5 * jnp.tanh(0.5 * x) + 0.5


def _silu(x):
    return x * _sigmoid(x)


def _softplus(x):
    return jnp.maximum(x, 0.0) + jnp.log(1.0 + jnp.exp(-jnp.abs(x)))


def _bdot(a, b):
    return jnp.dot(a.astype(BF16), b.astype(BF16), preferred_element_type=F32)


def _iota(shape, dim):
    return lax.broadcasted_iota(jnp.int32, shape, dim)


def _ada_kernel(c_ref, w_ref, b_ref, o_ref):
    o_ref[...] = jnp.dot(_silu(c_ref[...]), w_ref[...], preferred_element_type=F32,
                         precision=lax.Precision.HIGHEST) + b_ref[...]


def _ada_call(c_all, w_ada, b_ada):
    depth, _, n_out = w_ada.shape
    tn = 1536
    return pl.pallas_call(
        _ada_kernel,
        out_shape=jax.ShapeDtypeStruct((depth, ADA_ROWS, n_out), F32),
        grid=(depth, n_out // tn),
        in_specs=[
            pl.BlockSpec((ADA_ROWS, D_MODEL), lambda l, j: (0, 0)),
            pl.BlockSpec((None, D_MODEL, tn), lambda l, j: (l, 0, j)),
            pl.BlockSpec((None, 1, tn), lambda l, j: (l, 0, j)),
        ],
        out_specs=pl.BlockSpec((None, ADA_ROWS, tn), lambda l, j: (l, 0, j)),
        compiler_params=pltpu.CompilerParams(vmem_limit_bytes=V7X_VMEM_LIMIT),
        name="ada",
    )(c_all, w_ada, b_ada.reshape(depth, 1, n_out))


def _modulated_norm(x, gain, shift, scale):
    y = x * lax.rsqrt(jnp.mean(x * x, axis=-1, keepdims=True) + EPS) * gain
    return y * (1.0 + scale) + shift


def _inproj_kernel(x_ref, ada_ref, nw_ref, w_ref, o_ref):
    h = _modulated_norm(x_ref[...], nw_ref[...], ada_ref[0:1, :], ada_ref[1:2, :])
    o_ref[...] = jnp.dot(h.astype(BF16), w_ref[...], preferred_element_type=F32)


def _ada_row_map(seq_len, tm, per_seq):
    if not per_seq:
        return lambda i: (0, 0, 0)
    return lambda i: (1 + (i * tm) // seq_len, 0, 0)


def _inproj_call(x2, ada_l, norm_w, w_in_p, layer, seq_len, per_seq):
    n_tok = x2.shape[0]
    tm = TM_INPROJ
    return pl.pallas_call(
        _inproj_kernel,
        out_shape=jax.ShapeDtypeStruct((n_tok, PROJ_W), F32),
        grid=(n_tok // tm,),
        in_specs=[
            pl.BlockSpec((tm, D_MODEL), lambda i: (i, 0)),
            pl.BlockSpec((None, 6, D_MODEL), _ada_row_map(seq_len, tm, per_seq)),
            pl.BlockSpec((1, D_MODEL), lambda i: (0, 0)),
            pl.BlockSpec((None, D_MODEL, PROJ_W), lambda i: (layer, 0, 0), pipeline_mode=pl.Buffered(1)),
        ],
        out_specs=pl.BlockSpec((tm, PROJ_W), lambda i: (i, 0)),
        compiler_params=pltpu.CompilerParams(vmem_limit_bytes=V7X_VMEM_LIMIT),
        name="inproj",
    )(x2, ada_l, norm_w.reshape(1, D_MODEL), w_in_p)


def _col_bcast(x, lane, idx):
    col = jnp.sum(jnp.where(lane == idx, x, 0.0), axis=-1, keepdims=True)
    return jnp.broadcast_to(col, x.shape)


def _each(fn, *lists):
    return [fn(*args) for args in zip(*lists)]


def _each_y(fn, *lists, every=YIELD_EVERY):
    out = []
    for i, args in enumerate(zip(*lists)):
        out.append(fn(*args))
        if i % every == every - 1:
            yield
    return out


def _interleave(*gens):
    gens = list(gens)
    while gens:
        for g in list(gens):
            try:
                next(g)
            except StopIteration:
                gens.remove(g)


def _dn_kernel(*refs, seq_len, ns, hp, has_s0, out_state):
    n_ch = seq_len // CHUNK
    gc = min(PREP_UNROLL, ns * n_ch)
    n_grp = ns * n_ch // gc
    assert gc * GATE_COLS <= HEAD_W
    units = [(t, hh) for t in range(gc) for hh in range(hp)]
    nu = len(units)
    it = iter(refs)
    q_ref, k_ref, v_ref, z_ref, g_ref = (next(it) for _ in range(5))
    cwq_ref, cwk_ref, cwv_ref = (next(it) for _ in range(3))
    alog_ref, dtb_ref, dnn_ref = (next(it) for _ in range(3))
    s0_ref = next(it) if has_s0 else None
    o_ref = next(it)
    st_ref = next(it) if out_state else None
    wq_s, u_s, kdt_s, dl_s, at_s, vn_s, qs_s, s_s, ext_s, lbd_s, rhs_s = (next(it) for _ in range(11))

    head0 = pl.program_id(1) * hp
    row = _iota((CHUNK, 128), 0)
    lane = _iota((CHUNK, 128), 1)
    is_f = lane < CHUNK
    j = lane & (CHUNK - 1)
    incl = (is_f & (row >= j)) | (~is_f & (row <= j))
    strict = (is_f & (row > j)) | (~is_f & (row < j))
    diag = row == j
    eye_p = jnp.where(diag, 1.0, 0.0).astype(F32)
    blk16p = (row >> 4) == (j >> 4)
    r128 = _iota((128, 128), 0)
    c128 = _iota((128, 128), 1)
    lane_f128 = c128 < CHUNK
    cum_sel = ((r128 < CHUNK) & (c128 <= r128)) | ((r128 >= CHUNK) & (c128 >= r128 - CHUNK))
    cum_mat = jnp.where(cum_sel, 1.0, 0.0).astype(F32)[:, :CHUNK]

    neg_a = -jnp.exp(alog_ref[...])
    dtb = dtb_ref[...]
    cols = lambda hh: slice(hh * HEAD_W, (hh + 1) * HEAD_W)

    def group_chunks(g):
        ms = [g * gc + t for t in range(gc)]
        return ms, [m // n_ch for m in ms], [m % n_ch for m in ms]

    def conv_silu(ref, cw_ref, hh, s, c, r0, slot):
        p0 = pl.multiple_of(jnp.maximum(r0 - 8, 0), 8)
        n0 = pl.multiple_of(jnp.minimum(r0 + CHUNK, seq_len - 8), 8)
        ext_s[slot, 0:8, :] = jnp.where(c > 0, ref[s, pl.ds(p0, 8), cols(hh)], 0.0)
        ext_s[slot, 8:8 + CHUNK, :] = ref[s, pl.ds(r0, CHUNK), cols(hh)]
        ext_s[slot, 8 + CHUNK:16 + CHUNK, :] = jnp.where(c < n_ch - 1, ref[s, pl.ds(n0, 8), cols(hh)], 0.0)
        acc = ext_s[slot, 6:6 + CHUNK, :] * cw_ref[0:1, cols(hh)]
        for t in range(1, CONV_K):
            acc = acc + ext_s[slot, 6 + t:6 + t + CHUNK, :] * cw_ref[t:t + 1, cols(hh)]
        return _silu(acc)

    def l2n(x):
        return x * lax.rsqrt(jnp.sum(x * x, axis=-1, keepdims=True) + EPS)

    def gates(ss, r0s):
        packed = g_ref[ss[0], pl.ds(r0s[0], CHUNK), :]
        for t in range(1, gc):
            packed = packed + pltpu.roll(g_ref[ss[t], pl.ds(r0s[t], CHUNK), :], GATE_COLS * t, axis=1)
        is_beta = (lane & (GATE_COLS - 1)) < 2 * DN_HEADS
        gact = jnp.where(is_beta, _sigmoid(packed), neg_a * _softplus(packed + dtb))
        cums = jnp.dot(cum_mat, gact, preferred_element_type=F32, precision=lax.Precision.HIGHEST)
        e_cum = jnp.exp(cums)
        e_rest = jnp.concatenate([jnp.exp(cums[CHUNK - 1:CHUNK] - cums[:CHUNK]),
                                  jnp.exp(cums[CHUNK:CHUNK + 1] - cums[CHUNK:])], axis=0)
        return gact, cums, e_cum, e_rest

    def front(g, slot):
        ms, ss, cs = group_chunks(g)
        r0s = [pl.multiple_of(c * CHUNK, CHUNK) for c in cs]
        gact, cums, e_cum, e_rest = gates(ss, r0s)
        yield
        q, k, v = [], [], []
        for n, (t, hh) in enumerate(units):
            q.append(l2n(conv_silu(q_ref, cwq_ref, hh, ss[t], cs[t], r0s[t], n)) * (HEAD_W ** -0.5))
            k.append(l2n(conv_silu(k_ref, cwk_ref, hh, ss[t], cs[t], r0s[t], nu + n)))
            v.append(conv_silu(v_ref, cwv_ref, hh, ss[t], cs[t], r0s[t], 2 * nu + n))
            yield
        col = lambda x, which: (lambda u: _col_bcast(
            x, lane, GATE_COLS * u[0] + which * DN_HEADS + head0 + u[1]))
        beta_f = yield from _each_y(col(gact, 0), units)
        beta_b = yield from _each_y(col(gact, 1), units)
        gc_f = yield from _each_y(col(cums[:CHUNK], 2), units)
        gc_b = yield from _each_y(col(cums[CHUNK:], 3), units)
        e_f = yield from _each_y(col(e_cum[:CHUNK], 2), units)
        e_b = yield from _each_y(col(e_cum[CHUNK:], 3), units)
        rest_f = yield from _each_y(col(e_rest[:CHUNK], 2), units)
        rest_b = yield from _each_y(col(e_rest[CHUNK:], 3), units)

        def decay_of(gf, gb):
            gc_col = jnp.where(is_f, gf, gb)
            gc_row = jnp.sum(jnp.where(diag, gc_col, 0.0), axis=0, keepdims=True)
            return jnp.where(incl, jnp.exp(jnp.where(incl, gc_col - gc_row, 0.0)), 0.0)

        decay = yield from _each_y(decay_of, gc_f, gc_b, every=2)

        def gram_of(qq, kk):
            kb = kk.astype(BF16)
            return lax.dot_general(jnp.concatenate([qq.astype(BF16), kb], axis=0),
                                   jnp.concatenate([kb, kb], axis=0),
                                   (((1,), (1,)), ((), ())), preferred_element_type=F32)

        gram = yield from _each_y(gram_of, q, k, every=2)
        for n, (t, hh) in enumerate(units):
            l2 = jnp.where(strict, jnp.where(is_f, beta_f[n], beta_b[n]) * gram[n][CHUNK:] * decay[n], 0.0)
            lbd_s[slot, n] = l2.astype(BF16)
            at_s[hh, ms[t]] = (gram[n][:CHUNK] * decay[n]).astype(BF16)
            if n % 2 == 1:
                yield
        for n, (t, hh) in enumerate(units):
            rhs_s[slot, n] = jnp.concatenate([
                jnp.concatenate([v[n] * beta_f[n], k[n] * beta_f[n] * e_f[n]], axis=1),
                jnp.concatenate([v[n] * beta_b[n], k[n] * beta_b[n] * e_b[n]], axis=1)],
                axis=0).astype(BF16)
            wq_s[hh, 0, ms[t], CHUNK:, :] = (q[n] * e_f[n]).astype(BF16)
            wq_s[hh, 1, ms[t], CHUNK:, :] = (q[n] * e_b[n]).astype(BF16)
            if n % 2 == 1:
                yield
        for n, (t, hh) in enumerate(units):
            kdec = jnp.concatenate([k[n] * rest_f[n], k[n] * rest_b[n]], axis=0)
            kdt_s[hh, ms[t]] = kdec.T.astype(BF16)
            dl_s[hh, 0, ms[t]] = jnp.broadcast_to(e_f[n][CHUNK - 1:CHUNK, :], (8, 128))
            dl_s[hh, 1, ms[t]] = jnp.broadcast_to(e_b[n][0:1, :], (8, 128))
            if n % 2 == 1:
                yield

    def back(g, slot):
        ms = [g * gc + t for t in range(gc)]
        l2 = [lbd_s[slot, n] for n in range(nu)]
        zero = jnp.zeros((CHUNK, 128), BF16)

        def bd(x):
            x = x.astype(BF16)
            return jnp.concatenate([jnp.where(is_f, x, zero), jnp.where(is_f, zero, x)], axis=0)

        pdot = lambda a, b_bd: jnp.dot(a.astype(BF16), b_bd, preferred_element_type=F32)
        d = _each(lambda x: jnp.where(blk16p, x, zero), l2)
        lo_bd = _each(lambda x: bd(jnp.where(blk16p, zero, x)), l2)
        d2 = yield from _each_y(lambda a: pdot(a, bd(a)), d)
        d2_bd = _each(bd, d2)
        d3 = yield from _each_y(pdot, d, d2_bd)
        d4 = yield from _each_y(pdot, d2, d2_bd)
        d4_bd = _each(bd, d4)
        d8 = yield from _each_y(pdot, d4, d4_bd)
        p1 = _each(lambda a, b, c: eye_p - a.astype(F32) + b - c, d, d2, d3)
        p2 = yield from _each_y(lambda p, x: p + pdot(p, x), p1, d4_bd)
        dinv = yield from _each_y(lambda p, x: p + pdot(p, bd(x)), p2, d8)
        m = yield from _each_y(pdot, dinv, lo_bd)
        m_bd = _each(bd, m)
        m2 = yield from _each_y(pdot, m, m_bd)
        m3 = yield from _each_y(lambda a, b: pdot(a, bd(b)), m, m2)
        ninv = _each(lambda a, b, c: eye_p - a + b - c, m, m2, m3)
        t_inv = yield from _each_y(lambda a, b: pdot(a, bd(b)), ninv, dinv)
        uw = yield from _each_y(lambda tt, n: jnp.dot(bd(tt), rhs_s[slot, n], preferred_element_type=F32),
                                t_inv, range(nu), every=2)
        for n, (t, hh) in enumerate(units):
            u_s[hh, 0, ms[t]] = uw[n][:CHUNK, :HEAD_W]
            u_s[hh, 1, ms[t]] = uw[n][CHUNK:, :HEAD_W]
            wq_s[hh, 0, ms[t], :CHUNK, :] = uw[n][:CHUNK, HEAD_W:].astype(BF16)
            wq_s[hh, 1, ms[t], :CHUNK, :] = uw[n][CHUNK:, HEAD_W:].astype(BF16)

    _interleave(front(0, 0))

    def prep_body(i, carry):
        slot = i & 1
        _interleave(back(i, slot), front(i + 1, 1 - slot))
        return carry

    lax.fori_loop(0, n_grp - 1, prep_body, 0)
    _interleave(back(n_grp - 1, (n_grp - 1) & 1))

    chains = [(s, hh) for s in range(ns) for hh in range(hp)]
    for s, hh in chains:
        for d in range(2):
            s_s[s * hp + hh, d] = s0_ref[s, d, hh] if has_s0 else jnp.zeros((HEAD_W, HEAD_W), F32)

    dnn = dnn_ref[...]

    def emit_out(c):
        r0 = c * CHUNK if isinstance(c, int) else pl.multiple_of(c * CHUNK, CHUNK)
        for s, hh in chains:
            m = s * n_ch + c
            vst = jnp.concatenate([vn_s[hh, 0, m], vn_s[hh, 1, m]], axis=0)
            o = qs_s[hh, m] + jnp.dot(at_s[hh, m], vst, preferred_element_type=F32)
            y = o * lax.rsqrt(jnp.mean(o * o, axis=-1, keepdims=True) + EPS) * dnn
            o_ref[s, pl.ds(r0, CHUNK), cols(hh)] = (
                y * _silu(z_ref[s, pl.ds(r0, CHUNK), cols(hh)])).astype(o_ref.dtype)

    def scan_body(first, with_out, i, carry):
        if with_out:
            emit_out(i - 1)
            emit_out(n_ch - i)
        todo = [(s, hh, d, s * n_ch + (i if d == 0 else n_ch - 1 - i)) for s, hh in chains for d in range(2)]
        st = [s_s[s * hp + hh, d] for s, hh, d, m in todo]
        r = [jnp.dot(wq_s[hh, d, m], x.astype(BF16), preferred_element_type=F32)
             for (s, hh, d, m), x in zip(todo, st)]
        vn = []
        for (s, hh, d, m), rr in zip(todo, r):
            vn.append((u_s[hh, d, m] - rr[:CHUNK]).astype(BF16))
            vn_s[hh, d, m] = vn[-1]
        upd = []
        for (s, hh, d, m), x in zip(todo, vn):
            kdt = kdt_s[hh, m]
            zero = jnp.zeros_like(kdt)
            kdt = jnp.where(lane_f128, kdt, zero) if d == 0 else jnp.where(lane_f128, zero, kdt)
            upd.append(jnp.dot(kdt, jnp.concatenate([x, x], axis=0), preferred_element_type=F32))
        for (s, hh, d, m), x, rr, up in zip(todo, st, r, upd):
            s_s[s * hp + hh, d] = x * dl_s[hh, d, m][0:1, :] + up
            qs_s[hh, m] = rr[CHUNK:] if first else qs_s[hh, m] + rr[CHUNK:]
        return carry

    half = n_ch // 2
    lax.fori_loop(0, half, functools.partial(scan_body, True, False), 0)
    scan_body(False, False, half, 0)
    lax.fori_loop(half + 1, n_ch, functools.partial(scan_body, False, True), 0)
    emit_out(n_ch - 1)
    emit_out(0)
    if out_state:
        for s, hh in chains:
            for d in range(2):
                st_ref[s, d, hh] = s_s[s * hp + hh, d]


def _dn_call(proj3, conv_w, alog_row, dtb_row, dn_norm, state_delta, layer, out_state):
    n_seq, seq_len, _ = proj3.shape
    n_ch = seq_len // CHUNK
    hp = DN_HEADS_PER_STEP
    ns = max(1, min(n_seq, DN_CHUNKS_PER_STEP // n_ch))
    assert n_seq % ns == 0
    tot = ns * n_ch
    nu = min(PREP_UNROLL, tot) * hp
    wid = hp * HEAD_W
    nblk = DN_HEADS // hp
    has_s0 = state_delta is not None
    act_mode = dict(pipeline_mode=pl.Buffered(1)) if out_state else {}
    tile = lambda off: pl.BlockSpec((ns, seq_len, wid), lambda b, h: (b, 0, off + h), **act_mode)
    cw = lambda off: pl.BlockSpec((CONV_K, wid), lambda b, h: (0, off + h))
    row = pl.BlockSpec((1, HEAD_W), lambda b, h: (0, 0))
    in_specs = [tile(0), tile(nblk), tile(2 * nblk), tile(3 * nblk),
                pl.BlockSpec((ns, seq_len, HEAD_W), lambda b, h: (b, 0, GATE_BLK), **act_mode),
                cw(0), cw(nblk), cw(2 * nblk), row, row, row]
    args = [proj3, proj3, proj3, proj3, proj3, conv_w, conv_w, conv_w, alog_row, dtb_row,
            dn_norm.reshape(1, HEAD_W)]
    if has_s0:
        in_specs.append(pl.BlockSpec((ns, None, 2, hp, HEAD_W, HEAD_W),
                                     lambda b, h: (b, layer, 0, h, 0, 0)))
        args.append(state_delta)
    out_shape = [jax.ShapeDtypeStruct((n_seq, seq_len, DN_W), BF16)]
    out_specs = [pl.BlockSpec((ns, seq_len, wid), lambda b, h: (b, 0, h))]
    if out_state:
        out_shape.append(jax.ShapeDtypeStruct((n_seq, 2, DN_HEADS, HEAD_W, HEAD_W), F32))
        out_specs.append(pl.BlockSpec((ns, 2, hp, HEAD_W, HEAD_W), lambda b, h: (b, 0, h, 0, 0)))
    scratch = [
        pltpu.VMEM((hp, 2, tot, 2 * CHUNK, HEAD_W), BF16),
        pltpu.VMEM((hp, 2, tot, CHUNK, HEAD_W), F32),
        pltpu.VMEM((hp, tot, HEAD_W, 2 * CHUNK), BF16),
        pltpu.VMEM((hp, 2, tot, 8, HEAD_W), F32),
        pltpu.VMEM((hp, tot, CHUNK, 2 * CHUNK), BF16),
        pltpu.VMEM((hp, 2, tot, CHUNK, HEAD_W), BF16),
        pltpu.VMEM((hp, tot, CHUNK, HEAD_W), F32),
        pltpu.VMEM((ns * hp, 2, HEAD_W, HEAD_W), F32),
        pltpu.VMEM((3 * nu, CHUNK + 16, HEAD_W), F32),
        pltpu.VMEM((2, nu, CHUNK, 2 * CHUNK), BF16),
        pltpu.VMEM((2, nu, 2 * CHUNK, 2 * HEAD_W), BF16),
    ]
    outs = pl.pallas_call(
        functools.partial(_dn_kernel, seq_len=seq_len, ns=ns, hp=hp, has_s0=has_s0, out_state=out_state),
        out_shape=out_shape,
        grid=(n_seq // ns, nblk),
        in_specs=in_specs,
        out_specs=out_specs,
        scratch_shapes=scratch,
        compiler_params=pltpu.CompilerParams(vmem_limit_bytes=V7X_VMEM_LIMIT),
        name="deltanet",
    )(*args)
    return (outs[0], outs[1]) if out_state else (outs[0], None)


def _split2(x):
    hi = x.astype(BF16)
    lo = (x - hi.astype(F32)).astype(BF16)
    return hi, lo


def _window(pos, win, n):
    lo = jnp.clip(pos - win // 2, 0, n)
    hi = jnp.clip(pos - win // 2 + win, 0, n)
    return lo, hi


def _pool_kernel(u_ref, pw_ref, ps_ref, o_ref, m_s, *, seq_len, rows):
    period = GRID_W if rows else seq_len
    tb = min(POOL_TB, seq_len)
    r_i = _iota((tb, tb), 0)
    c_i = _iota((tb, tb), 1)
    shift = period.bit_length() - 1
    same_line = (r_i >> shift) == (c_i >> shift)
    pos_r = r_i & (period - 1)
    pos_c = c_i & (period - 1)
    pos_col = _iota((tb, HEAD_W), 0) & (period - 1)
    for g, win in enumerate(POOL_WINDOWS):
        cols = slice(g * HEAD_W, (g + 1) * HEAD_W)
        if rows:
            run = None
            prev_lo = prev_hi = 0
            for r in range(rows):
                lo, hi = max(r - win // 2, 0), min(r - win // 2 + win, rows)
                for a in range(prev_hi, hi):
                    slab = u_ref[a * GRID_W:(a + 1) * GRID_W, cols]
                    run = slab if run is None else run + slab
                for a in range(prev_lo, lo):
                    run = run - u_ref[a * GRID_W:(a + 1) * GRID_W, cols]
                prev_lo, prev_hi = lo, hi
                m_s[r * GRID_W:(r + 1) * GRID_W, :] = run / float(hi - lo)
        lo_r, hi_r = _window(pos_r, win, period)
        band = jnp.where(same_line & (pos_c >= lo_r) & (pos_c < hi_r), 1.0, 0.0).astype(BF16)
        lo_c, hi_c = _window(pos_col, win, period)
        cnt = (hi_c - lo_c).astype(F32)
        pw = pw_ref[g]
        scale = ps_ref[:, cols]
        for t0 in range(0, seq_len, tb):
            ug = u_ref[t0:t0 + tb, cols]
            src = m_s[t0:t0 + tb, :] if rows else ug
            hi, lo = _split2(src)
            box = (jnp.dot(band, hi, preferred_element_type=F32)
                   + jnp.dot(band, lo, preferred_element_type=F32))
            mean = box / cnt
            o_ref[t0:t0 + tb, cols] = (_bdot(mean - ug, pw) * scale).astype(o_ref.dtype)


def _pool_call(proj3, pool_w_bf, pool_scale, layer, rows):
    n_seq, seq_len, _ = proj3.shape
    return pl.pallas_call(
        functools.partial(_pool_kernel, seq_len=seq_len, rows=rows),
        out_shape=jax.ShapeDtypeStruct((n_seq, seq_len, POOL_W), BF16),
        grid=(n_seq,),
        in_specs=[
            pl.BlockSpec((None, seq_len, POOL_W), lambda b: (b, 0, POOL_BLK512)),
            pl.BlockSpec((None, POOL_GROUPS, HEAD_W, HEAD_W), lambda b: (layer, 0, 0, 0)),
            pl.BlockSpec((1, POOL_W), lambda b: (0, 0)),
        ],
        out_specs=pl.BlockSpec((None, seq_len, POOL_W), lambda b: (b, 0, 0)),
        scratch_shapes=[pltpu.VMEM((seq_len, HEAD_W), F32)],
        compiler_params=pltpu.CompilerParams(vmem_limit_bytes=V7X_VMEM_LIMIT),
        name="pool",
    )(proj3, pool_w_bf, pool_scale.reshape(1, POOL_W))


def _post_kernel(x_ref, o_ref, p_ref, ada_ref, nf_ref, fin_ref, wo_ref, wgu_ref, wd_ref, y_ref, *, final):
    mix = (jnp.dot(o_ref[...], wo_ref[:DN_W, :], preferred_element_type=F32)
           + jnp.dot(p_ref[...], wo_ref[DN_W:, :], preferred_element_type=F32))
    x = x_ref[...] + ada_ref[2:3, :] * mix
    h = _modulated_norm(x, nf_ref[...], ada_ref[3:4, :], ada_ref[4:5, :])
    gu = jnp.dot(h.astype(BF16), wgu_ref[...], preferred_element_type=F32)
    act = _silu(gu[:, :D_FF]) * gu[:, D_FF:]
    x = x + ada_ref[5:6, :] * jnp.dot(act.astype(BF16), wd_ref[...], preferred_element_type=F32)
    if final:
        x = x * lax.rsqrt(jnp.mean(x * x, axis=-1, keepdims=True) + EPS) * fin_ref[...]
    y_ref[...] = x


def _post_call(x2, o2, p2, ada_l, norm_ffn, final_norm, w_out_bf, w_gu_bf, w_down_bf, layer, seq_len, per_seq,
               final):
    n_tok = x2.shape[0]
    tm = TM_POST
    const = lambda shape: pl.BlockSpec((None,) + shape, lambda i: (layer, 0, 0), pipeline_mode=pl.Buffered(1))
    return pl.pallas_call(
        functools.partial(_post_kernel, final=final),
        out_shape=jax.ShapeDtypeStruct((n_tok, D_MODEL), F32),
        grid=(n_tok // tm,),
        in_specs=[
            pl.BlockSpec((tm, D_MODEL), lambda i: (i, 0)),
            pl.BlockSpec((tm, DN_W), lambda i: (i, 0)),
            pl.BlockSpec((tm, POOL_W), lambda i: (i, 0)),
            pl.BlockSpec((None, 6, D_MODEL), _ada_row_map(seq_len, tm, per_seq)),
            pl.BlockSpec((1, D_MODEL), lambda i: (0, 0)),
            pl.BlockSpec((1, D_MODEL), lambda i: (0, 0)),
            const((D_MODEL, D_MODEL)),
            const((D_MODEL, 2 * D_FF)),
            const((D_FF, D_MODEL)),
        ],
        out_specs=pl.BlockSpec((tm, D_MODEL), lambda i: (i, 0)),
        compiler_params=pltpu.CompilerParams(vmem_limit_bytes=V7X_VMEM_LIMIT),
        name="post",
    )(x2, o2, p2, ada_l, norm_ffn.reshape(1, D_MODEL), final_norm.reshape(1, D_MODEL),
      w_out_bf, w_gu_bf, w_down_bf)


def _gate_row(p):
    one = jnp.concatenate([jnp.zeros((2 * DN_HEADS,), F32), p.reshape(-1)])
    return jnp.tile(one, HEAD_W // GATE_COLS).reshape(1, HEAD_W)


def _stream_layer(x2, seq_len, rows, per_seq, ada_l, lw, state_delta, layer, final):
    n_seq = x2.shape[0] // seq_len
    proj = _inproj_call(x2, ada_l, lw["norm_mix"], lw["w_in"], layer, seq_len, per_seq)
    proj3 = proj.reshape(n_seq, seq_len, PROJ_W)
    o, st = _dn_call(proj3, lw["conv_w"], lw["alog_row"], lw["dtb_row"], lw["dn_norm"],
                     state_delta, layer, out_state=state_delta is None)
    p = _pool_call(proj3, lw["pool_w"], lw["pool_scale"], layer, rows)
    x2 = _post_call(x2, o.reshape(-1, DN_W), p.reshape(-1, POOL_W), ada_l, lw["norm_ffn"], lw["final_norm"],
                    lw["w_out"], lw["w_gu"], lw["w_down"], layer, seq_len, per_seq, final)
    return x2, st


def kernel(x_prompt, x_sample, c, state_delta, c_ctx, w_ada, b_ada, norm_mix, norm_ffn, w_in, conv_w, a_log, dt_bias, dn_norm, pool_w, pool_scale, w_out, w_gu, w_down, final_norm):
    depth = w_ada.shape[0]
    n_ctx, ctx_len, _ = x_prompt.shape
    n_lat, lat_len, _ = x_sample.shape
    assert 1 + n_lat <= ADA_ROWS
    c_all = jnp.concatenate([c_ctx[None, :], c, jnp.zeros((ADA_ROWS - 1 - n_lat, D_MODEL), F32)], axis=0)
    ada = _ada_call(c_all, w_ada, b_ada).reshape(depth, ADA_ROWS, 6, D_MODEL)

    qkvz = 4 * DN_W
    n_gate = 4 * DN_HEADS
    xp = x_prompt.reshape(-1, D_MODEL)
    xs = x_sample.reshape(-1, D_MODEL)
    states = []
    w_in_p = jnp.concatenate(
        [w_in[:, :, :qkvz], w_in[:, :, qkvz + n_gate:], w_in[:, :, qkvz:qkvz + n_gate],
         jnp.zeros((depth, D_MODEL, HEAD_W - n_gate), F32)], axis=2).astype(BF16)
    pool_w_bf, w_out_bf, w_gu_bf, w_down_bf = (w.astype(BF16) for w in (pool_w, w_out, w_gu, w_down))
    for l in range(depth):
        lw = dict(norm_mix=norm_mix[l], norm_ffn=norm_ffn[l], w_in=w_in_p, conv_w=conv_w[l],
                  alog_row=_gate_row(a_log[l]), dtb_row=_gate_row(dt_bias[l]), dn_norm=dn_norm[l],
                  pool_w=pool_w_bf, pool_scale=pool_scale[l], w_out=w_out_bf,
                  w_gu=w_gu_bf, w_down=w_down_bf, final_norm=final_norm)
        final = l == depth - 1
        xp, st = _stream_layer(xp, ctx_len, None, False, ada[l], lw, None, l, final)
        states.append(st)
        xs, _ = _stream_layer(xs, lat_len, lat_len // GRID_W, True, ada[l], lw, state_delta, l, final)
    new_state = jnp.stack(states, axis=1)
    return (xp.reshape(x_prompt.shape), xs.reshape(x_sample.shape), new_state)
```

```python
import functools

import jax
import jax.numpy as jnp
from jax import lax
from jax.experimental import pallas as pl
from jax.experimental.pallas import tpu as pltpu

F32 = jnp.float32
BF16 = jnp.bfloat16

D_MODEL = 1024
DN_HEADS = 4
HEAD_W = 128
DN_W = DN_HEADS * HEAD_W
POOL_GROUPS = 4
POOL_W = POOL_GROUPS * HEAD_W
POOL_WINDOWS = (2, 4, 8, 16)
GRID_W = 64
CONV_K = 5
CHUNK = 64
D_FF = 2816
EPS = 1e-6

PROJ_W = 3 * DN_W + DN_W + POOL_W + HEAD_W
POOL_BLK512 = 4
GATE_BLK = 20
GATE_COLS = 4 * DN_HEADS
ADA_ROWS = 16

V7X_VMEM_LIMIT = 56 * 1024 * 1024

TM_INPROJ = 512
TM_POST = 512
POOL_TB = 256
PREP_UNROLL = 8
DN_HEADS_PER_STEP = 2
YIELD_EVERY = 16
DN_CHUNKS_PER_STEP = 32


def _sigmoid(x):
    return 0.5 * jnp.tanh(0.5 * x) + 0.5


def _silu(x):
    return x * _sigmoid(x)


def _softplus(x):
    return jnp.maximum(x, 0.0) + jnp.log(1.0 + jnp.exp(-jnp.abs(x)))


def _bdot(a, b):
    return jnp.dot(a.astype(BF16), b.astype(BF16), preferred_element_type=F32)


def _iota(shape, dim):
    return lax.broadcasted_iota(jnp.int32, shape, dim)


def _ada_kernel(c_ref, w_ref, b_ref, o_ref):
    o_ref[...] = jnp.dot(_silu(c_ref[...]), w_ref[...], preferred_element_type=F32,
                         precision=lax.Precision.HIGHEST) + b_ref[...]


def _ada_call(c_all, w_ada, b_ada):
    depth, _, n_out = w_ada.shape
    tn = 1536
    return pl.pallas_call(
        _ada_kernel,
        out_shape=jax.ShapeDtypeStruct((depth, ADA_ROWS, n_out), F32),
        grid=(depth, n_out // tn),
        in_specs=[
            pl.BlockSpec((ADA_ROWS, D_MODEL), lambda l, j: (0, 0)),
            pl.BlockSpec((None, D_MODEL, tn), lambda l, j: (l, 0, j)),
            pl.BlockSpec((None, 1, tn), lambda l, j: (l, 0, j)),
        ],
        out_specs=pl.BlockSpec((None, ADA_ROWS, tn), lambda l, j: (l, 0, j)),
        compiler_params=pltpu.CompilerParams(vmem_limit_bytes=V7X_VMEM_LIMIT),
        name="ada",
    )(c_all, w_ada, b_ada.reshape(depth, 1, n_out))


def _modulated_norm(x, gain, shift, scale):
    y = x * lax.rsqrt(jnp.mean(x * x, axis=-1, keepdims=True) + EPS) * gain
    return y * (1.0 + scale) + shift


def _inproj_kernel(x_ref, ada_ref, nw_ref, w_ref, o_ref):
    h = _modulated_norm(x_ref[...], nw_ref[...], ada_ref[0:1, :], ada_ref[1:2, :])
    o_ref[...] = jnp.dot(h.astype(BF16), w_ref[...], preferred_element_type=F32)


def _ada_row_map(seq_len, tm, per_seq):
    if not per_seq:
        return lambda i: (0, 0, 0)
    return lambda i: (1 + (i * tm) // seq_len, 0, 0)


def _inproj_call(x2, ada_l, norm_w, w_in_p, layer, seq_len, per_seq):
    n_tok = x2.shape[0]
    tm = TM_INPROJ
    return pl.pallas_call(
        _inproj_kernel,
        out_shape=jax.ShapeDtypeStruct((n_tok, PROJ_W), F32),
        grid=(n_tok // tm,),
        in_specs=[
            pl.BlockSpec((tm, D_MODEL), lambda i: (i, 0)),
            pl.BlockSpec((None, 6, D_MODEL), _ada_row_map(seq_len, tm, per_seq)),
            pl.BlockSpec((1, D_MODEL), lambda i: (0, 0)),
            pl.BlockSpec((None, D_MODEL, PROJ_W), lambda i: (layer, 0, 0), pipeline_mode=pl.Buffered(1)),
        ],
        out_specs=pl.BlockSpec((tm, PROJ_W), lambda i: (i, 0)),
        compiler_params=pltpu.CompilerParams(vmem_limit_bytes=V7X_VMEM_LIMIT),
        name="inproj",
    )(x2, ada_l, norm_w.reshape(1, D_MODEL), w_in_p)


def _col_bcast(x, lane, idx):
    col = jnp.sum(jnp.where(lane == idx, x, 0.0), axis=-1, keepdims=True)
    return jnp.broadcast_to(col, x.shape)


def _each(fn, *lists):
    return [fn(*args) for args in zip(*lists)]


def _each_y(fn, *lists, every=YIELD_EVERY):
    out = []
    for i, args in enumerate(zip(*lists)):
        out.append(fn(*args))
        if i % every == every - 1:
            yield
    return out


def _interleave(*gens):
    gens = list(gens)
    while gens:
        for g in list(gens):
            try:
                next(g)
            except StopIteration:
                gens.remove(g)


def _dn_kernel(*refs, seq_len, ns, hp, has_s0, out_state):
    n_ch = seq_len // CHUNK
    gc = min(PREP_UNROLL, ns * n_ch)
    n_grp = ns * n_ch // gc
    assert gc * GATE_COLS <= HEAD_W
    units = [(t, hh) for t in range(gc) for hh in range(hp)]
    nu = len(units)
    it = iter(refs)
    q_ref, k_ref, v_ref, z_ref, g_ref = (next(it) for _ in range(5))
    cwq_ref, cwk_ref, cwv_ref = (next(it) for _ in range(3))
    alog_ref, dtb_ref, dnn_ref = (next(it) for _ in range(3))
    s0_ref = next(it) if has_s0 else None
    o_ref = next(it)
    st_ref = next(it) if out_state else None
    wq_s, u_s, kdt_s, dl_s, at_s, vn_s, qs_s, s_s, ext_s, lbd_s, rhs_s = (next(it) for _ in range(11))

    head0 = pl.program_id(1) * hp
    row = _iota((CHUNK, 128), 0)
    lane = _iota((CHUNK, 128), 1)
    is_f = lane < CHUNK
    j = lane & (CHUNK - 1)
    incl = (is_f & (row >= j)) | (~is_f & (row <= j))
    strict = (is_f & (row > j)) | (~is_f & (row < j))
    diag = row == j
    eye_p = jnp.where(diag, 1.0, 0.0).astype(F32)
    blk16p = (row >> 4) == (j >> 4)
    r128 = _iota((128, 128), 0)
    c128 = _iota((128, 128), 1)
    lane_f128 = c128 < CHUNK
    cum_sel = ((r128 < CHUNK) & (c128 <= r128)) | ((r128 >= CHUNK) & (c128 >= r128 - CHUNK))
    cum_mat = jnp.where(cum_sel, 1.0, 0.0).astype(F32)[:, :CHUNK]

    neg_a = -jnp.exp(alog_ref[...])
    dtb = dtb_ref[...]
    cols = lambda hh: slice(hh * HEAD_W, (hh + 1) * HEAD_W)

    def group_chunks(g):
        ms = [g * gc + t for t in range(gc)]
        return ms, [m // n_ch for m in ms], [m % n_ch for m in ms]

    def conv_silu(ref, cw_ref, hh, s, c, r0, slot):
        p0 = pl.multiple_of(jnp.maximum(r0 - 8, 0), 8)
        n0 = pl.multiple_of(jnp.minimum(r0 + CHUNK, seq_len - 8), 8)
        ext_s[slot, 0:8, :] = jnp.where(c > 0, ref[s, pl.ds(p0, 8), cols(hh)], 0.0)
        ext_s[slot, 8:8 + CHUNK, :] = ref[s, pl.ds(r0, CHUNK), cols(hh)]
        ext_s[slot, 8 + CHUNK:16 + CHUNK, :] = jnp.where(c < n_ch - 1, ref[s, pl.ds(n0, 8), cols(hh)], 0.0)
        acc = ext_s[slot, 6:6 + CHUNK, :] * cw_ref[0:1, cols(hh)]
        for t in range(1, CONV_K):
            acc = acc + ext_s[slot, 6 + t:6 + t + CHUNK, :] * cw_ref[t:t + 1, cols(hh)]
        return _silu(acc)

    def l2n(x):
        return x * lax.rsqrt(jnp.sum(x * x, axis=-1, keepdims=True) + EPS)

    def gates(ss, r0s):
        packed = g_ref[ss[0], pl.ds(r0s[0], CHUNK), :]
        for t in range(1, gc):
            packed = packed + pltpu.roll(g_ref[ss[t], pl.ds(r0s[t], CHUNK), :], GATE_COLS * t, axis=1)
        is_beta = (lane & (GATE_COLS - 1)) < 2 * DN_HEADS
        gact = jnp.where(is_beta, _sigmoid(packed), neg_a * _softplus(packed + dtb))
        cums = jnp.dot(cum_mat, gact, preferred_element_type=F32, precision=lax.Precision.HIGHEST)
        e_cum = jnp.exp(cums)
        e_rest = jnp.concatenate([jnp.exp(cums[CHUNK - 1:CHUNK] - cums[:CHUNK]),
                                  jnp.exp(cums[CHUNK:CHUNK + 1] - cums[CHUNK:])], axis=0)
        return gact, cums, e_cum, e_rest

    def front(g, slot):
        ms, ss, cs = group_chunks(g)
        r0s = [pl.multiple_of(c * CHUNK, CHUNK) for c in cs]
        gact, cums, e_cum, e_rest = gates(ss, r0s)
        yield
        q, k, v = [], [], []
        for n, (t, hh) in enumerate(units):
            q.append(l2n(conv_silu(q_ref, cwq_ref, hh, ss[t], cs[t], r0s[t], n)) * (HEAD_W ** -0.5))
            k.append(l2n(conv_silu(k_ref, cwk_ref, hh, ss[t], cs[t], r0s[t], nu + n)))
            v.append(conv_silu(v_ref, cwv_ref, hh, ss[t], cs[t], r0s[t], 2 * nu + n))
            yield
        col = lambda x, which: (lambda u: _col_bcast(
            x, lane, GATE_COLS * u[0] + which * DN_HEADS + head0 + u[1]))
        beta_f = yield from _each_y(col(gact, 0), units)
        beta_b = yield from _each_y(col(gact, 1), units)
        gc_f = yield from _each_y(col(cums[:CHUNK], 2), units)
        gc_b = yield from _each_y(col(cums[CHUNK:], 3), units)
        e_f = yield from _each_y(col(e_cum[:CHUNK], 2), units)
        e_b = yield from _each_y(col(e_cum[CHUNK:], 3), units)
        rest_f = yield from _each_y(col(e_rest[:CHUNK], 2), units)
        rest_b = yield from _each_y(col(e_rest[CHUNK:], 3), units)

        def decay_of(gf, gb):
            gc_col = jnp.where(is_f, gf, gb)
            gc_row = jnp.sum(jnp.where(diag, gc_col, 0.0), axis=0, keepdims=True)
            return jnp.where(incl, jnp.exp(jnp.where(incl, gc_col - gc_row, 0.0)), 0.0)

        decay = yield from _each_y(decay_of, gc_f, gc_b, every=2)

        def gram_of(qq, kk):
            kb = kk.astype(BF16)
            return lax.dot_general(jnp.concatenate([qq.astype(BF16), kb], axis=0),
                                   jnp.concatenate([kb, kb], axis=0),
                                   (((1,), (1,)), ((), ())), preferred_element_type=F32)

        gram = yield from _each_y(gram_of, q, k, every=2)
        for n, (t, hh) in enumerate(units):
            l2 = jnp.where(strict, jnp.where(is_f, beta_f[n], beta_b[n]) * gram[n][CHUNK:] * decay[n], 0.0)
            lbd_s[slot, n] = l2.astype(BF16)
            at_s[hh, ms[t]] = (gram[n][:CHUNK] * decay[n]).astype(BF16)
            if n % 2 == 1:
                yield
        for n, (t, hh) in enumerate(units):
            rhs_s[slot, n] = jnp.concatenate([
                jnp.concatenate([v[n] * beta_f[n], k[n] * beta_f[n] * e_f[n]], axis=1),
                jnp.concatenate([v[n] * beta_b[n], k[n] * beta_b[n] * e_b[n]], axis=1)],
                axis=0).astype(BF16)
            wq_s[hh, 0, ms[t], CHUNK:, :] = (q[n] * e_f[n]).astype(BF16)
            wq_s[hh, 1, ms[t], CHUNK:, :] = (q[n] * e_b[n]).astype(BF16)
            if n % 2 == 1:
                yield
        for n, (t, hh) in enumerate(units):
            kdec = jnp.concatenate([k[n] * rest_f[n], k[n] * rest_b[n]], axis=0)
            kdt_s[hh, ms[t]] = kdec.T.astype(BF16)
            dl_s[hh, 0, ms[t]] = jnp.broadcast_to(e_f[n][CHUNK - 1:CHUNK, :], (8, 128))
            dl_s[hh, 1, ms[t]] = jnp.broadcast_to(e_b[n][0:1, :], (8, 128))
            if n % 2 == 1:
                yield

    def back(g, slot):
        ms = [g * gc + t for t in range(gc)]
        l2 = [lbd_s[slot, n] for n in range(nu)]
        zero = jnp.zeros((CHUNK, 128), BF16)

        def bd(x):
            x = x.astype(BF16)
            return jnp.concatenate([jnp.where(is_f, x, zero), jnp.where(is_f, zero, x)], axis=0)

        pdot = lambda a, b_bd: jnp.dot(a.astype(BF16), b_bd, preferred_element_type=F32)
        d = _each(lambda x: jnp.where(blk16p, x, zero), l2)
        lo_bd = _each(lambda x: bd(jnp.where(blk16p, zero, x)), l2)
        d2 = yield from _each_y(lambda a: pdot(a, bd(a)), d)
        d2_bd = _each(bd, d2)
        d3 = yield from _each_y(pdot, d, d2_bd)
        d4 = yield from _each_y(pdot, d2, d2_bd)
        d4_bd = _each(bd, d4)
        d8 = yield from _each_y(pdot, d4, d4_bd)
        p1 = _each(lambda a, b, c: eye_p - a.astype(F32) + b - c, d, d2, d3)
        p2 = yield from _each_y(lambda p, x: p + pdot(p, x), p1, d4_bd)
        dinv = yield from _each_y(lambda p, x: p + pdot(p, bd(x)), p2, d8)
        m = yield from _each_y(pdot, dinv, lo_bd)
        m_bd = _each(bd, m)
        m2 = yield from _each_y(pdot, m, m_bd)
        m3 = yield from _each_y(lambda a, b: pdot(a, bd(b)), m, m2)
        ninv = _each(lambda a, b, c: eye_p - a + b - c, m, m2, m3)
        t_inv = yield from _each_y(lambda a, b: pdot(a, bd(b)), ninv, dinv)
        uw = yield from _each_y(lambda tt, n: jnp.dot(bd(tt), rhs_s[slot, n], preferred_element_type=F32),
                                t_inv, range(nu), every=2)
        for n, (t, hh) in enumerate(units):
            u_s[hh, 0, ms[t]] = uw[n][:CHUNK, :HEAD_W]
            u_s[hh, 1, ms[t]] = uw[n][CHUNK:, :HEAD_W]
            wq_s[hh, 0, ms[t], :CHUNK, :] = uw[n][:CHUNK, HEAD_W:].astype(BF16)
            wq_s[hh, 1, ms[t], :CHUNK, :] = uw[n][CHUNK:, HEAD_W:].astype(BF16)

    chains = [(s, hh) for s in range(ns) for hh in range(hp)]

    dnn = dnn_ref[...]

    def emit_out(c):
        r0 = c * CHUNK if isinstance(c, int) else pl.multiple_of(c * CHUNK, CHUNK)
        for s, hh in chains:
            m = s * n_ch + c
            vst = jnp.concatenate([vn_s[hh, 0, m], vn_s[hh, 1, m]], axis=0)
            o = qs_s[hh, m] + jnp.dot(at_s[hh, m], vst, preferred_element_type=F32)
            y = o * lax.rsqrt(jnp.mean(o * o, axis=-1, keepdims=True) + EPS) * dnn
            o_ref[s, pl.ds(r0, CHUNK), cols(hh)] = (
                y * _silu(z_ref[s, pl.ds(r0, CHUNK), cols(hh)])).astype(o_ref.dtype)

    def scan_body(first, with_out, i, carry):
        if with_out:
            emit_out(i - 1)
            emit_out(n_ch - i)
        todo = [(s, hh, d, s * n_ch + (i if d == 0 else n_ch - 1 - i)) for s, hh in chains for d in range(2)]
        st = [s_s[s * hp + hh, d] for s, hh, d, m in todo]
        r = [jnp.dot(wq_s[hh, d, m], x.astype(BF16), preferred_element_type=F32)
             for (s, hh, d, m), x in zip(todo, st)]
        vn = []
        for (s, hh, d, m), rr in zip(todo, r):
            vn.append((u_s[hh, d, m] - rr[:CHUNK]).astype(BF16))
            vn_s[hh, d, m] = vn[-1]
        upd = []
        for (s, hh, d, m), x in zip(todo, vn):
            kdt = kdt_s[hh, m]
            zero = jnp.zeros_like(kdt)
            kdt = jnp.where(lane_f128, kdt, zero) if d == 0 else jnp.where(lane_f128, zero, kdt)
            upd.append(jnp.dot(kdt, jnp.concatenate([x, x], axis=0), preferred_element_type=F32))
        for (s, hh, d, m), x, rr, up in zip(todo, st, r, upd):
            s_s[s * hp + hh, d] = x * dl_s[hh, d, m][0:1, :] + up
            qs_s[hh, m] = rr[CHUNK:] if first else qs_s[hh, m] + rr[CHUNK:]
        return carry

    half = n_ch // 2
    early = ns == 1 and n_grp >= 3 and gc <= half
    order = (lambda i: (i + n_grp - 1) % n_grp) if early else (lambda i: i)
    _interleave(front(order(0), 0))

    def prep_body(i, carry):
        slot = i & 1
        _interleave(back(order(i), slot), front(order(i + 1), 1 - slot))
        return carry

    lax.fori_loop(0, n_grp - 1, prep_body, 0)
    for s, hh in chains:
        for d in range(2):
            s_s[s * hp + hh, d] = s0_ref[s, d, hh] if has_s0 else jnp.zeros((HEAD_W, HEAD_W), F32)

    def early_scan():
        for i in range(gc):
            scan_body(True, False, i, 0)
            yield

    last = back(order(n_grp - 1), (n_grp - 1) & 1)
    _interleave(last, early_scan()) if early else _interleave(last)
    n_early = gc if early else 0

    lax.fori_loop(n_early, half, functools.partial(scan_body, True, False), 0)
    scan_body(False, False, half, 0)
    lax.fori_loop(half + 1, n_ch, functools.partial(scan_body, False, True), 0)
    emit_out(n_ch - 1)
    emit_out(0)
    if out_state:
        for s, hh in chains:
            for d in range(2):
                st_ref[s, d, hh] = s_s[s * hp + hh, d]


def _dn_call(proj3, conv_w, alog_row, dtb_row, dn_norm, state_delta, layer, out_state):
    n_seq, seq_len, _ = proj3.shape
    n_ch = seq_len // CHUNK
    hp = DN_HEADS_PER_STEP
    ns = max(1, min(n_seq, DN_CHUNKS_PER_STEP // n_ch))
    assert n_seq % ns == 0
    tot = ns * n_ch
    nu = min(PREP_UNROLL, tot) * hp
    wid = hp * HEAD_W
    nblk = DN_HEADS // hp
    has_s0 = state_delta is not None
    act_mode = dict(pipeline_mode=pl.Buffered(1)) if out_state else {}
    tile = lambda off: pl.BlockSpec((ns, seq_len, wid), lambda b, h: (b, 0, off + h), **act_mode)
    cw = lambda off: pl.BlockSpec((CONV_K, wid), lambda b, h: (0, off + h))
    row = pl.BlockSpec((1, HEAD_W), lambda b, h: (0, 0))
    in_specs = [tile(0), tile(nblk), tile(2 * nblk), tile(3 * nblk),
                pl.BlockSpec((ns, seq_len, HEAD_W), lambda b, h: (b, 0, GATE_BLK), **act_mode),
                cw(0), cw(nblk), cw(2 * nblk), row, row, row]
    args = [proj3, proj3, proj3, proj3, proj3, conv_w, conv_w, conv_w, alog_row, dtb_row,
            dn_norm.reshape(1, HEAD_W)]
    if has_s0:
        in_specs.append(pl.BlockSpec((ns, None, 2, hp, HEAD_W, HEAD_W),
                                     lambda b, h: (b, layer, 0, h, 0, 0)))
        args.append(state_delta)
    out_shape = [jax.ShapeDtypeStruct((n_seq, seq_len, DN_W), BF16)]
    out_specs = [pl.BlockSpec((ns, seq_len, wid), lambda b, h: (b, 0, h))]
    if out_state:
        out_shape.append(jax.ShapeDtypeStruct((n_seq, 2, DN_HEADS, HEAD_W, HEAD_W), F32))
        out_specs.append(pl.BlockSpec((ns, 2, hp, HEAD_W, HEAD_W), lambda b, h: (b, 0, h, 0, 0)))
    scratch = [
        pltpu.VMEM((hp, 2, tot, 2 * CHUNK, HEAD_W), BF16),
        pltpu.VMEM((hp, 2, tot, CHUNK, HEAD_W), F32),
        pltpu.VMEM((hp, tot, HEAD_W, 2 * CHUNK), BF16),
        pltpu.VMEM((hp, 2, tot, 8, HEAD_W), F32),
        pltpu.VMEM((hp, tot, CHUNK, 2 * CHUNK), BF16),
        pltpu.VMEM((hp, 2, tot, CHUNK, HEAD_W), BF16),
        pltpu.VMEM((hp, tot, CHUNK, HEAD_W), F32),
        pltpu.VMEM((ns * hp, 2, HEAD_W, HEAD_W), F32),
        pltpu.VMEM((3 * nu, CHUNK + 16, HEAD_W), F32),
        pltpu.VMEM((2, nu, CHUNK, 2 * CHUNK), BF16),
        pltpu.VMEM((2, nu, 2 * CHUNK, 2 * HEAD_W), BF16),
    ]
    outs = pl.pallas_call(
        functools.partial(_dn_kernel, seq_len=seq_len, ns=ns, hp=hp, has_s0=has_s0, out_state=out_state),
        out_shape=out_shape,
        grid=(n_seq // ns, nblk),
        in_specs=in_specs,
        out_specs=out_specs,
        scratch_shapes=scratch,
        compiler_params=pltpu.CompilerParams(vmem_limit_bytes=V7X_VMEM_LIMIT),
        name="deltanet",
    )(*args)
    return (outs[0], outs[1]) if out_state else (outs[0], None)


def _split2(x):
    hi = x.astype(BF16)
    lo = (x - hi.astype(F32)).astype(BF16)
    return hi, lo


def _window(pos, win, n):
    lo = jnp.clip(pos - win // 2, 0, n)
    hi = jnp.clip(pos - win // 2 + win, 0, n)
    return lo, hi


def _pool_kernel(u_ref, pw_ref, ps_ref, o_ref, m_s, *, seq_len, rows):
    period = GRID_W if rows else seq_len
    tb = min(POOL_TB, seq_len)
    r_i = _iota((tb, tb), 0)
    c_i = _iota((tb, tb), 1)
    shift = period.bit_length() - 1
    same_line = (r_i >> shift) == (c_i >> shift)
    pos_r = r_i & (period - 1)
    pos_c = c_i & (period - 1)
    pos_col = _iota((tb, HEAD_W), 0) & (period - 1)
    for g, win in enumerate(POOL_WINDOWS):
        cols = slice(g * HEAD_W, (g + 1) * HEAD_W)
        if rows:
            run = None
            prev_lo = prev_hi = 0
            for r in range(rows):
                lo, hi = max(r - win // 2, 0), min(r - win // 2 + win, rows)
                for a in range(prev_hi, hi):
                    slab = u_ref[a * GRID_W:(a + 1) * GRID_W, cols]
                    run = slab if run is None else run + slab
                for a in range(prev_lo, lo):
                    run = run - u_ref[a * GRID_W:(a + 1) * GRID_W, cols]
                prev_lo, prev_hi = lo, hi
                m_s[r * GRID_W:(r + 1) * GRID_W, :] = run / float(hi - lo)
        lo_r, hi_r = _window(pos_r, win, period)
        band = jnp.where(same_line & (pos_c >= lo_r) & (pos_c < hi_r), 1.0, 0.0).astype(BF16)
        lo_c, hi_c = _window(pos_col, win, period)
        cnt = (hi_c - lo_c).astype(F32)
        pw = pw_ref[g]
        scale = ps_ref[:, cols]
        for t0 in range(0, seq_len, tb):
            ug = u_ref[t0:t0 + tb, cols]
            src = m_s[t0:t0 + tb, :] if rows else ug
            hi, lo = _split2(src)
            box = (jnp.dot(band, hi, preferred_element_type=F32)
                   + jnp.dot(band, lo, preferred_element_type=F32))
            mean = box / cnt
            o_ref[t0:t0 + tb, cols] = (_bdot(mean - ug, pw) * scale).astype(o_ref.dtype)


def _pool_call(proj3, pool_w_bf, pool_scale, layer, rows):
    n_seq, seq_len, _ = proj3.shape
    return pl.pallas_call(
        functools.partial(_pool_kernel, seq_len=seq_len, rows=rows),
        out_shape=jax.ShapeDtypeStruct((n_seq, seq_len, POOL_W), BF16),
        grid=(n_seq,),
        in_specs=[
            pl.BlockSpec((None, seq_len, POOL_W), lambda b: (b, 0, POOL_BLK512)),
            pl.BlockSpec((None, POOL_GROUPS, HEAD_W, HEAD_W), lambda b: (layer, 0, 0, 0)),
            pl.BlockSpec((1, POOL_W), lambda b: (0, 0)),
        ],
        out_specs=pl.BlockSpec((None, seq_len, POOL_W), lambda b: (b, 0, 0)),
        scratch_shapes=[pltpu.VMEM((seq_len, HEAD_W), F32)],
        compiler_params=pltpu.CompilerParams(vmem_limit_bytes=V7X_VMEM_LIMIT),
        name="pool",
    )(proj3, pool_w_bf, pool_scale.reshape(1, POOL_W))


def _post_kernel(x_ref, o_ref, p_ref, ada_ref, nf_ref, fin_ref, wo_ref, wgu_ref, wd_ref, y_ref, *, final):
    mix = (jnp.dot(o_ref[...], wo_ref[:DN_W, :], preferred_element_type=F32)
           + jnp.dot(p_ref[...], wo_ref[DN_W:, :], preferred_element_type=F32))
    x = x_ref[...] + ada_ref[2:3, :] * mix
    h = _modulated_norm(x, nf_ref[...], ada_ref[3:4, :], ada_ref[4:5, :])
    gu = jnp.dot(h.astype(BF16), wgu_ref[...], preferred_element_type=F32)
    act = _silu(gu[:, :D_FF]) * gu[:, D_FF:]
    x = x + ada_ref[5:6, :] * jnp.dot(act.astype(BF16), wd_ref[...], preferred_element_type=F32)
    if final:
        x = x * lax.rsqrt(jnp.mean(x * x, axis=-1, keepdims=True) + EPS) * fin_ref[...]
    y_ref[...] = x


def _post_call(x2, o2, p2, ada_l, norm_ffn, final_norm, w_out_bf, w_gu_bf, w_down_bf, layer, seq_len, per_seq,
               final):
    n_tok = x2.shape[0]
    tm = TM_POST
    const = lambda shape: pl.BlockSpec((None,) + shape, lambda i: (layer, 0, 0), pipeline_mode=pl.Buffered(1))
    return pl.pallas_call(
        functools.partial(_post_kernel, final=final),
        out_shape=jax.ShapeDtypeStruct((n_tok, D_MODEL), F32),
        grid=(n_tok // tm,),
        in_specs=[
            pl.BlockSpec((tm, D_MODEL), lambda i: (i, 0)),
            pl.BlockSpec((tm, DN_W), lambda i: (i, 0)),
            pl.BlockSpec((tm, POOL_W), lambda i: (i, 0)),
            pl.BlockSpec((None, 6, D_MODEL), _ada_row_map(seq_len, tm, per_seq)),
            pl.BlockSpec((1, D_MODEL), lambda i: (0, 0)),
            pl.BlockSpec((1, D_MODEL), lambda i: (0, 0)),
            const((D_MODEL, D_MODEL)),
            const((D_MODEL, 2 * D_FF)),
            const((D_FF, D_MODEL)),
        ],
        out_specs=pl.BlockSpec((tm, D_MODEL), lambda i: (i, 0)),
        compiler_params=pltpu.CompilerParams(vmem_limit_bytes=V7X_VMEM_LIMIT),
        name="post",
    )(x2, o2, p2, ada_l, norm_ffn.reshape(1, D_MODEL), final_norm.reshape(1, D_MODEL),
      w_out_bf, w_gu_bf, w_down_bf)


def _gate_row(p):
    one = jnp.concatenate([jnp.zeros((2 * DN_HEADS,), F32), p.reshape(-1)])
    return jnp.tile(one, HEAD_W // GATE_COLS).reshape(1, HEAD_W)


def _stream_layer(x2, seq_len, rows, per_seq, ada_l, lw, state_delta, layer, final):
    n_seq = x2.shape[0] // seq_len
    proj = _inproj_call(x2, ada_l, lw["norm_mix"], lw["w_in"], layer, seq_len, per_seq)
    proj3 = proj.reshape(n_seq, seq_len, PROJ_W)
    o, st = _dn_call(proj3, lw["conv_w"], lw["alog_row"], lw["dtb_row"], lw["dn_norm"],
                     state_delta, layer, out_state=state_delta is None)
    p = _pool_call(proj3, lw["pool_w"], lw["pool_scale"], layer, rows)
    x2 = _post_call(x2, o.reshape(-1, DN_W), p.reshape(-1, POOL_W), ada_l, lw["norm_ffn"], lw["final_norm"],
                    lw["w_out"], lw["w_gu"], lw["w_down"], layer, seq_len, per_seq, final)
    return x2, st


def kernel(x_prompt, x_sample, c, state_delta, c_ctx, w_ada, b_ada, norm_mix, norm_ffn, w_in, conv_w, a_log, dt_bias, dn_norm, pool_w, pool_scale, w_out, w_gu, w_down, final_norm):
    depth = w_ada.shape[0]
    n_ctx, ctx_len, _ = x_prompt.shape
    n_lat, lat_len, _ = x_sample.shape
    assert 1 + n_lat <= ADA_ROWS
    c_all = jnp.concatenate([c_ctx[None, :], c, jnp.zeros((ADA_ROWS - 1 - n_lat, D_MODEL), F32)], axis=0)
    ada = _ada_call(c_all, w_ada, b_ada).reshape(depth, ADA_ROWS, 6, D_MODEL)

    qkvz = 4 * DN_W
    n_gate = 4 * DN_HEADS
    xp = x_prompt.reshape(-1, D_MODEL)
    xs = x_sample.reshape(-1, D_MODEL)
    states = []
    w_in_p = jnp.concatenate(
        [w_in[:, :, :qkvz], w_in[:, :, qkvz + n_gate:], w_in[:, :, qkvz:qkvz + n_gate],
         jnp.zeros((depth, D_MODEL, HEAD_W - n_gate), F32)], axis=2).astype(BF16)
    pool_w_bf, w_out_bf, w_gu_bf, w_down_bf = (w.astype(BF16) for w in (pool_w, w_out, w_gu, w_down))
    for l in range(depth):
        lw = dict(norm_mix=norm_mix[l], norm_ffn=norm_ffn[l], w_in=w_in_p, conv_w=conv_w[l],
                  alog_row=_gate_row(a_log[l]), dtb_row=_gate_row(dt_bias[l]), dn_norm=dn_norm[l],
                  pool_w=pool_w_bf, pool_scale=pool_scale[l], w_out=w_out_bf,
                  w_gu=w_gu_bf, w_down=w_down_bf, final_norm=final_norm)
        final = l == depth - 1
        xp, st = _stream_layer(xp, ctx_len, None, False, ada[l], lw, None, l, final)
        states.append(st)
        xs, _ = _stream_layer(xs, lat_len, lat_len // GRID_W, True, ada[l], lw, state_delta, l, final)
    new_state = jnp.stack(states, axis=1)
    return (xp.reshape(x_prompt.shape), xs.reshape(x_sample.shape), new_state)
```

```python
import functools

import jax
import jax.numpy as jnp
from jax import lax
from jax.experimental import pallas as pl
from jax.experimental.pallas import tpu as pltpu

F32 = jnp.float32
BF16 = jnp.bfloat16

D_MODEL = 1024
DN_HEADS = 4
HEAD_W = 128
DN_W = DN_HEADS * HEAD_W
POOL_GROUPS = 4
POOL_W = POOL_GROUPS * HEAD_W
POOL_WINDOWS = (2, 4, 8, 16)
GRID_W = 64
CONV_K = 5
CHUNK = 64
D_FF = 2816
EPS = 1e-6

PROJ_W = 3 * DN_W + DN_W + POOL_W + HEAD_W
POOL_BLK512 = 4
GATE_BLK = 20
GATE_COLS = 4 * DN_HEADS
ADA_ROWS = 16

V7X_VMEM_LIMIT = 56 * 1024 * 1024

TM_INPROJ = 512
TM_POST = 512
POOL_TB = 256
PREP_UNROLL = 8
DN_HEADS_PER_STEP = 2
YIELD_EVERY = 16
DN_CHUNKS_PER_STEP = 32


def _sigmoid(x):
    return 0.5 * jnp.tanh(0.5 * x) + 0.5


def _silu(x):
    return x * _sigmoid(x)


def _softplus(x):
    return jnp.maximum(x, 0.0) + jnp.log(1.0 + jnp.exp(-jnp.abs(x)))


def _bdot(a, b):
    return jnp.dot(a.astype(BF16), b.astype(BF16), preferred_element_type=F32)


def _iota(shape, dim):
    return lax.broadcasted_iota(jnp.int32, shape, dim)


def _ada_kernel(c_ref, w_ref, b_ref, o_ref):
    o_ref[...] = jnp.dot(_silu(c_ref[...]), w_ref[...], preferred_element_type=F32,
                         precision=lax.Precision.HIGHEST) + b_ref[...]


def _ada_call(c_all, w_ada, b_ada):
    depth, _, n_out = w_ada.shape
    tn = 1536
    return pl.pallas_call(
        _ada_kernel,
        out_shape=jax.ShapeDtypeStruct((depth, ADA_ROWS, n_out), F32),
        grid=(depth, n_out // tn),
        in_specs=[
            pl.BlockSpec((ADA_ROWS, D_MODEL), lambda l, j: (0, 0)),
            pl.BlockSpec((None, D_MODEL, tn), lambda l, j: (l, 0, j)),
            pl.BlockSpec((None, 1, tn), lambda l, j: (l, 0, j)),
        ],
        out_specs=pl.BlockSpec((None, ADA_ROWS, tn), lambda l, j: (l, 0, j)),
        compiler_params=pltpu.CompilerParams(vmem_limit_bytes=V7X_VMEM_LIMIT),
        name="ada",
    )(c_all, w_ada, b_ada.reshape(depth, 1, n_out))


def _modulated_norm(x, gain, shift, scale):
    y = x * lax.rsqrt(jnp.mean(x * x, axis=-1, keepdims=True) + EPS) * gain
    return y * (1.0 + scale) + shift


def _inproj_kernel(x_ref, ada_ref, nw_ref, w_ref, o_ref):
    h = _modulated_norm(x_ref[...], nw_ref[...], ada_ref[0:1, :], ada_ref[1:2, :])
    o_ref[...] = jnp.dot(h.astype(BF16), w_ref[...], preferred_element_type=F32)


def _token_tile(tm, n_tok, seq_len, per_seq):
    tm = min(tm, seq_len) if per_seq else tm
    assert n_tok % tm == 0 and (not per_seq or seq_len % tm == 0)
    return tm


def _ada_row_map(seq_len, tm, per_seq):
    if not per_seq:
        return lambda i: (0, 0, 0)
    return lambda i: (1 + (i * tm) // seq_len, 0, 0)


def _inproj_call(x2, ada_l, norm_w, w_in_p, layer, seq_len, per_seq):
    n_tok = x2.shape[0]
    tm = _token_tile(TM_INPROJ, n_tok, seq_len, per_seq)
    return pl.pallas_call(
        _inproj_kernel,
        out_shape=jax.ShapeDtypeStruct((n_tok, PROJ_W), F32),
        grid=(n_tok // tm,),
        in_specs=[
            pl.BlockSpec((tm, D_MODEL), lambda i: (i, 0)),
            pl.BlockSpec((None, 6, D_MODEL), _ada_row_map(seq_len, tm, per_seq)),
            pl.BlockSpec((1, D_MODEL), lambda i: (0, 0)),
            pl.BlockSpec((None, D_MODEL, PROJ_W), lambda i: (layer, 0, 0), pipeline_mode=pl.Buffered(1)),
        ],
        out_specs=pl.BlockSpec((tm, PROJ_W), lambda i: (i, 0)),
        compiler_params=pltpu.CompilerParams(vmem_limit_bytes=V7X_VMEM_LIMIT),
        name="inproj",
    )(x2, ada_l, norm_w.reshape(1, D_MODEL), w_in_p)


def _col_bcast(x, lane, idx):
    col = jnp.sum(jnp.where(lane == idx, x, 0.0), axis=-1, keepdims=True)
    return jnp.broadcast_to(col, x.shape)


def _each(fn, *lists):
    return [fn(*args) for args in zip(*lists)]


def _each_y(fn, *lists, every=YIELD_EVERY):
    out = []
    for i, args in enumerate(zip(*lists)):
        out.append(fn(*args))
        if i % every == every - 1:
            yield
    return out


def _interleave(*gens):
    gens = list(gens)
    while gens:
        for g in list(gens):
            try:
                next(g)
            except StopIteration:
                gens.remove(g)


def _dn_kernel(*refs, seq_len, ns, hp, has_s0, out_state):
    n_ch = seq_len // CHUNK
    gc = min(PREP_UNROLL, ns * n_ch)
    n_grp = ns * n_ch // gc
    assert gc * GATE_COLS <= HEAD_W
    units = [(t, hh) for t in range(gc) for hh in range(hp)]
    nu = len(units)
    it = iter(refs)
    q_ref, k_ref, v_ref, z_ref, g_ref = (next(it) for _ in range(5))
    cwq_ref, cwk_ref, cwv_ref = (next(it) for _ in range(3))
    alog_ref, dtb_ref, dnn_ref = (next(it) for _ in range(3))
    s0_ref = next(it) if has_s0 else None
    o_ref = next(it)
    st_ref = next(it) if out_state else None
    wq_s, u_s, kdt_s, dl_s, at_s, vn_s, qs_s, s_s, ext_s, lbd_s, rhs_s = (next(it) for _ in range(11))

    head0 = pl.program_id(1) * hp
    row = _iota((CHUNK, 128), 0)
    lane = _iota((CHUNK, 128), 1)
    is_f = lane < CHUNK
    j = lane & (CHUNK - 1)
    incl = (is_f & (row >= j)) | (~is_f & (row <= j))
    strict = (is_f & (row > j)) | (~is_f & (row < j))
    diag = row == j
    eye_p = jnp.where(diag, 1.0, 0.0).astype(F32)
    couple = []
    for lvl in range(CHUNK.bit_length() - 1):
        rb, cb = row >> lvl, j >> lvl
        couple.append((is_f & ((rb & 1) == 1) & (cb == rb - 1)) | (~is_f & ((rb & 1) == 0) & (cb == rb + 1)))
    r128 = _iota((128, 128), 0)
    c128 = _iota((128, 128), 1)
    lane_f128 = c128 < CHUNK
    cum_sel = ((r128 < CHUNK) & (c128 <= r128)) | ((r128 >= CHUNK) & (c128 >= r128 - CHUNK))
    cum_mat = jnp.where(cum_sel, 1.0, 0.0).astype(F32)[:, :CHUNK]

    neg_a = -jnp.exp(alog_ref[...])
    dtb = dtb_ref[...]
    cols = lambda hh: slice(hh * HEAD_W, (hh + 1) * HEAD_W)

    def group_chunks(g):
        ms = [g * gc + t for t in range(gc)]
        return ms, [m // n_ch for m in ms], [m % n_ch for m in ms]

    def conv_silu(ref, cw_ref, hh, s, c, r0, slot):
        p0 = pl.multiple_of(jnp.maximum(r0 - 8, 0), 8)
        n0 = pl.multiple_of(jnp.minimum(r0 + CHUNK, seq_len - 8), 8)
        ext_s[slot, 0:8, :] = jnp.where(c > 0, ref[s, pl.ds(p0, 8), cols(hh)], 0.0)
        ext_s[slot, 8:8 + CHUNK, :] = ref[s, pl.ds(r0, CHUNK), cols(hh)]
        ext_s[slot, 8 + CHUNK:16 + CHUNK, :] = jnp.where(c < n_ch - 1, ref[s, pl.ds(n0, 8), cols(hh)], 0.0)
        acc = ext_s[slot, 6:6 + CHUNK, :] * cw_ref[0:1, cols(hh)]
        for t in range(1, CONV_K):
            acc = acc + ext_s[slot, 6 + t:6 + t + CHUNK, :] * cw_ref[t:t + 1, cols(hh)]
        return _silu(acc)

    def l2n(x):
        return x * lax.rsqrt(jnp.sum(x * x, axis=-1, keepdims=True) + EPS)

    def gates(ss, r0s):
        packed = g_ref[ss[0], pl.ds(r0s[0], CHUNK), :]
        for t in range(1, gc):
            packed = packed + pltpu.roll(g_ref[ss[t], pl.ds(r0s[t], CHUNK), :], GATE_COLS * t, axis=1)
        is_beta = (lane & (GATE_COLS - 1)) < 2 * DN_HEADS
        gact = jnp.where(is_beta, _sigmoid(packed), neg_a * _softplus(packed + dtb))
        cums = jnp.dot(cum_mat, gact, preferred_element_type=F32, precision=lax.Precision.HIGHEST)
        e_cum = jnp.exp(cums)
        e_rest = jnp.concatenate([jnp.exp(cums[CHUNK - 1:CHUNK] - cums[:CHUNK]),
                                  jnp.exp(cums[CHUNK:CHUNK + 1] - cums[CHUNK:])], axis=0)
        return gact, cums, e_cum, e_rest

    def front(g, slot):
        ms, ss, cs = group_chunks(g)
        r0s = [pl.multiple_of(c * CHUNK, CHUNK) for c in cs]
        gact, cums, e_cum, e_rest = gates(ss, r0s)
        yield
        q, k, v = [], [], []
        for n, (t, hh) in enumerate(units):
            q.append(l2n(conv_silu(q_ref, cwq_ref, hh, ss[t], cs[t], r0s[t], n)) * (HEAD_W ** -0.5))
            k.append(l2n(conv_silu(k_ref, cwk_ref, hh, ss[t], cs[t], r0s[t], nu + n)))
            v.append(conv_silu(v_ref, cwv_ref, hh, ss[t], cs[t], r0s[t], 2 * nu + n))
            yield
        col = lambda x, which: (lambda u: _col_bcast(
            x, lane, GATE_COLS * u[0] + which * DN_HEADS + head0 + u[1]))
        beta_f = yield from _each_y(col(gact, 0), units)
        beta_b = yield from _each_y(col(gact, 1), units)
        gc_f = yield from _each_y(col(cums[:CHUNK], 2), units)
        gc_b = yield from _each_y(col(cums[CHUNK:], 3), units)
        e_f = yield from _each_y(col(e_cum[:CHUNK], 2), units)
        e_b = yield from _each_y(col(e_cum[CHUNK:], 3), units)
        rest_f = yield from _each_y(col(e_rest[:CHUNK], 2), units)
        rest_b = yield from _each_y(col(e_rest[CHUNK:], 3), units)

        def decay_of(gf, gb):
            gc_col = jnp.where(is_f, gf, gb)
            gc_row = jnp.sum(jnp.where(diag, gc_col, 0.0), axis=0, keepdims=True)
            return jnp.where(incl, jnp.exp(jnp.where(incl, gc_col - gc_row, 0.0)), 0.0)

        decay = yield from _each_y(decay_of, gc_f, gc_b, every=2)

        def gram_of(qq, kk):
            kb = kk.astype(BF16)
            return lax.dot_general(jnp.concatenate([qq.astype(BF16), kb], axis=0),
                                   jnp.concatenate([kb, kb], axis=0),
                                   (((1,), (1,)), ((), ())), preferred_element_type=F32)

        gram = yield from _each_y(gram_of, q, k, every=2)
        for n, (t, hh) in enumerate(units):
            l2 = jnp.where(strict, jnp.where(is_f, beta_f[n], beta_b[n]) * gram[n][CHUNK:] * decay[n], 0.0)
            lbd_s[slot, n] = l2.astype(BF16)
            at_s[hh, ms[t]] = (gram[n][:CHUNK] * decay[n]).astype(BF16)
            if n % 2 == 1:
                yield
        for n, (t, hh) in enumerate(units):
            rhs_s[slot, n] = jnp.concatenate([
                jnp.concatenate([v[n] * beta_f[n], k[n] * beta_f[n] * e_f[n]], axis=1),
                jnp.concatenate([v[n] * beta_b[n], k[n] * beta_b[n] * e_b[n]], axis=1)],
                axis=0).astype(BF16)
            wq_s[hh, 0, ms[t], CHUNK:, :] = (q[n] * e_f[n]).astype(BF16)
            wq_s[hh, 1, ms[t], CHUNK:, :] = (q[n] * e_b[n]).astype(BF16)
            if n % 2 == 1:
                yield
        for n, (t, hh) in enumerate(units):
            kdec = jnp.concatenate([k[n] * rest_f[n], k[n] * rest_b[n]], axis=0)
            kdt_s[hh, ms[t]] = kdec.T.astype(BF16)
            dl_s[hh, 0, ms[t]] = jnp.broadcast_to(e_f[n][CHUNK - 1:CHUNK, :], (8, 128))
            dl_s[hh, 1, ms[t]] = jnp.broadcast_to(e_b[n][0:1, :], (8, 128))
            if n % 2 == 1:
                yield

    def back(g, slot):
        ms = [g * gc + t for t in range(gc)]
        l2 = [lbd_s[slot, n] for n in range(nu)]
        zero = jnp.zeros((CHUNK, 128), BF16)

        def bd(x):
            x = x.astype(BF16)
            return jnp.concatenate([jnp.where(is_f, x, zero), jnp.where(is_f, zero, x)], axis=0)

        pdot = lambda a, b_bd: jnp.dot(a.astype(BF16), b_bd, preferred_element_type=F32)
        inv = _each(lambda x: eye_p - jnp.where(couple[0], x, zero).astype(F32), l2)
        for lvl in range(1, len(couple)):
            c_bd = _each(lambda x: bd(jnp.where(couple[lvl], x, zero)), l2)
            xc = yield from _each_y(pdot, inv, c_bd)
            inv = yield from _each_y(lambda x, y: x - pdot(y, bd(x)), inv, xc)
        uw = yield from _each_y(lambda tt, n: jnp.dot(bd(tt), rhs_s[slot, n], preferred_element_type=F32),
                                inv, range(nu), every=2)
        for n, (t, hh) in enumerate(units):
            u_s[hh, 0, ms[t]] = uw[n][:CHUNK, :HEAD_W]
            u_s[hh, 1, ms[t]] = uw[n][CHUNK:, :HEAD_W]
            wq_s[hh, 0, ms[t], :CHUNK, :] = uw[n][:CHUNK, HEAD_W:].astype(BF16)
            wq_s[hh, 1, ms[t], :CHUNK, :] = uw[n][CHUNK:, HEAD_W:].astype(BF16)

    chains = [(s, hh) for s in range(ns) for hh in range(hp)]

    dnn = dnn_ref[...]

    def emit_out(c):
        r0 = c * CHUNK if isinstance(c, int) else pl.multiple_of(c * CHUNK, CHUNK)
        for s, hh in chains:
            m = s * n_ch + c
            vst = jnp.concatenate([vn_s[hh, 0, m], vn_s[hh, 1, m]], axis=0)
            o = qs_s[hh, m] + jnp.dot(at_s[hh, m], vst, preferred_element_type=F32)
            y = o * lax.rsqrt(jnp.mean(o * o, axis=-1, keepdims=True) + EPS) * dnn
            o_ref[s, pl.ds(r0, CHUNK), cols(hh)] = (
                y * _silu(z_ref[s, pl.ds(r0, CHUNK), cols(hh)])).astype(o_ref.dtype)

    def scan_body(first, with_out, i, carry):
        if with_out:
            emit_out(i - 1)
            emit_out(n_ch - i)
        todo = [(s, hh, d, s * n_ch + (i if d == 0 else n_ch - 1 - i)) for s, hh in chains for d in range(2)]
        st = [s_s[s * hp + hh, d] for s, hh, d, m in todo]
        r = [jnp.dot(wq_s[hh, d, m], x.astype(BF16), preferred_element_type=F32)
             for (s, hh, d, m), x in zip(todo, st)]
        vn = []
        for (s, hh, d, m), rr in zip(todo, r):
            vn.append((u_s[hh, d, m] - rr[:CHUNK]).astype(BF16))
            vn_s[hh, d, m] = vn[-1]
        upd = []
        for (s, hh, d, m), x in zip(todo, vn):
            kdt = kdt_s[hh, m]
            zero = jnp.zeros_like(kdt)
            kdt = jnp.where(lane_f128, kdt, zero) if d == 0 else jnp.where(lane_f128, zero, kdt)
            upd.append(jnp.dot(kdt, jnp.concatenate([x, x], axis=0), preferred_element_type=F32))
        for (s, hh, d, m), x, rr, up in zip(todo, st, r, upd):
            s_s[s * hp + hh, d] = x * dl_s[hh, d, m][0:1, :] + up
            qs_s[hh, m] = rr[CHUNK:] if first else qs_s[hh, m] + rr[CHUNK:]
        return carry

    half = n_ch // 2
    early = ns == 1 and n_grp >= 3 and gc <= half
    order = (lambda i: (i + n_grp - 1) % n_grp) if early else (lambda i: i)
    _interleave(front(order(0), 0))

    def prep_body(i, carry):
        slot = i & 1
        _interleave(back(order(i), slot), front(order(i + 1), 1 - slot))
        return carry

    lax.fori_loop(0, n_grp - 1, prep_body, 0)
    for s, hh in chains:
        for d in range(2):
            s_s[s * hp + hh, d] = s0_ref[s, d, hh] if has_s0 else jnp.zeros((HEAD_W, HEAD_W), F32)

    def early_scan():
        for i in range(gc):
            scan_body(True, False, i, 0)
            yield

    last = back(order(n_grp - 1), (n_grp - 1) & 1)
    _interleave(last, early_scan()) if early else _interleave(last)
    n_early = gc if early else 0

    lax.fori_loop(n_early, half, functools.partial(scan_body, True, False), 0)
    scan_body(False, False, half, 0)
    lax.fori_loop(half + 1, n_ch, functools.partial(scan_body, False, True), 0)
    emit_out(n_ch - 1)
    emit_out(0)
    if out_state:
        for s, hh in chains:
            for d in range(2):
                st_ref[s, d, hh] = s_s[s * hp + hh, d]


def _dn_call(proj3, conv_w, alog_row, dtb_row, dn_norm, state_delta, layer, out_state):
    n_seq, seq_len, _ = proj3.shape
    n_ch = seq_len // CHUNK
    hp = DN_HEADS_PER_STEP
    ns = max(1, min(n_seq, DN_CHUNKS_PER_STEP // n_ch))
    assert n_seq % ns == 0
    tot = ns * n_ch
    nu = min(PREP_UNROLL, tot) * hp
    wid = hp * HEAD_W
    nblk = DN_HEADS // hp
    has_s0 = state_delta is not None
    act_mode = dict(pipeline_mode=pl.Buffered(1)) if out_state else {}
    tile = lambda off: pl.BlockSpec((ns, seq_len, wid), lambda b, h: (b, 0, off + h), **act_mode)
    cw = lambda off: pl.BlockSpec((CONV_K, wid), lambda b, h: (0, off + h))
    row = pl.BlockSpec((1, HEAD_W), lambda b, h: (0, 0))
    in_specs = [tile(0), tile(nblk), tile(2 * nblk), tile(3 * nblk),
                pl.BlockSpec((ns, seq_len, HEAD_W), lambda b, h: (b, 0, GATE_BLK), **act_mode),
                cw(0), cw(nblk), cw(2 * nblk), row, row, row]
    args = [proj3, proj3, proj3, proj3, proj3, conv_w, conv_w, conv_w, alog_row, dtb_row,
            dn_norm.reshape(1, HEAD_W)]
    if has_s0:
        in_specs.append(pl.BlockSpec((ns, None, 2, hp, HEAD_W, HEAD_W),
                                     lambda b, h: (b, layer, 0, h, 0, 0)))
        args.append(state_delta)
    out_shape = [jax.ShapeDtypeStruct((n_seq, seq_len, DN_W), BF16)]
    out_specs = [pl.BlockSpec((ns, seq_len, wid), lambda b, h: (b, 0, h))]
    if out_state:
        out_shape.append(jax.ShapeDtypeStruct((n_seq, 2, DN_HEADS, HEAD_W, HEAD_W), F32))
        out_specs.append(pl.BlockSpec((ns, 2, hp, HEAD_W, HEAD_W), lambda b, h: (b, 0, h, 0, 0)))
    scratch = [
        pltpu.VMEM((hp, 2, tot, 2 * CHUNK, HEAD_W), BF16),
        pltpu.VMEM((hp, 2, tot, CHUNK, HEAD_W), F32),
        pltpu.VMEM((hp, tot, HEAD_W, 2 * CHUNK), BF16),
        pltpu.VMEM((hp, 2, tot, 8, HEAD_W), F32),
        pltpu.VMEM((hp, tot, CHUNK, 2 * CHUNK), BF16),
        pltpu.VMEM((hp, 2, tot, CHUNK, HEAD_W), BF16),
        pltpu.VMEM((hp, tot, CHUNK, HEAD_W), F32),
        pltpu.VMEM((ns * hp, 2, HEAD_W, HEAD_W), F32),
        pltpu.VMEM((3 * nu, CHUNK + 16, HEAD_W), F32),
        pltpu.VMEM((2, nu, CHUNK, 2 * CHUNK), BF16),
        pltpu.VMEM((2, nu, 2 * CHUNK, 2 * HEAD_W), BF16),
    ]
    outs = pl.pallas_call(
        functools.partial(_dn_kernel, seq_len=seq_len, ns=ns, hp=hp, has_s0=has_s0, out_state=out_state),
        out_shape=out_shape,
        grid=(n_seq // ns, nblk),
        in_specs=in_specs,
        out_specs=out_specs,
        scratch_shapes=scratch,
        compiler_params=pltpu.CompilerParams(vmem_limit_bytes=V7X_VMEM_LIMIT),
        name="deltanet",
    )(*args)
    return (outs[0], outs[1]) if out_state else (outs[0], None)


def _split2(x):
    hi = x.astype(BF16)
    lo = (x - hi.astype(F32)).astype(BF16)
    return hi, lo


def _window(pos, win, n):
    lo = jnp.clip(pos - win // 2, 0, n)
    hi = jnp.clip(pos - win // 2 + win, 0, n)
    return lo, hi


def _pool_kernel(u_ref, pw_ref, ps_ref, o_ref, m_s, *, seq_len, rows):
    period = GRID_W if rows else seq_len
    tb = min(POOL_TB, seq_len)
    r_i = _iota((tb, tb), 0)
    c_i = _iota((tb, tb), 1)
    shift = period.bit_length() - 1
    same_line = (r_i >> shift) == (c_i >> shift)
    pos_r = r_i & (period - 1)
    pos_c = c_i & (period - 1)
    pos_col = _iota((tb, HEAD_W), 0) & (period - 1)
    for g, win in enumerate(POOL_WINDOWS):
        cols = slice(g * HEAD_W, (g + 1) * HEAD_W)
        if rows:
            run = None
            prev_lo = prev_hi = 0
            for r in range(rows):
                lo, hi = max(r - win // 2, 0), min(r - win // 2 + win, rows)
                for a in range(prev_hi, hi):
                    slab = u_ref[a * GRID_W:(a + 1) * GRID_W, cols]
                    run = slab if run is None else run + slab
                for a in range(prev_lo, lo):
                    run = run - u_ref[a * GRID_W:(a + 1) * GRID_W, cols]
                prev_lo, prev_hi = lo, hi
                m_s[r * GRID_W:(r + 1) * GRID_W, :] = run / float(hi - lo)
        lo_r, hi_r = _window(pos_r, win, period)
        band = jnp.where(same_line & (pos_c >= lo_r) & (pos_c < hi_r), 1.0, 0.0).astype(BF16)
        lo_c, hi_c = _window(pos_col, win, period)
        cnt = (hi_c - lo_c).astype(F32)
        pw = pw_ref[g]
        scale = ps_ref[:, cols]
        for t0 in range(0, seq_len, tb):
            ug = u_ref[t0:t0 + tb, cols]
            src = m_s[t0:t0 + tb, :] if rows else ug
            hi, lo = _split2(src)
            box = (jnp.dot(band, hi, preferred_element_type=F32)
                   + jnp.dot(band, lo, preferred_element_type=F32))
            mean = box / cnt
            o_ref[t0:t0 + tb, cols] = (_bdot(mean - ug, pw) * scale).astype(o_ref.dtype)


def _pool_call(proj3, pool_w_bf, pool_scale, layer, rows):
    n_seq, seq_len, _ = proj3.shape
    return pl.pallas_call(
        functools.partial(_pool_kernel, seq_len=seq_len, rows=rows),
        out_shape=jax.ShapeDtypeStruct((n_seq, seq_len, POOL_W), BF16),
        grid=(n_seq,),
        in_specs=[
            pl.BlockSpec((None, seq_len, POOL_W), lambda b: (b, 0, POOL_BLK512)),
            pl.BlockSpec((None, POOL_GROUPS, HEAD_W, HEAD_W), lambda b: (layer, 0, 0, 0)),
            pl.BlockSpec((1, POOL_W), lambda b: (0, 0)),
        ],
        out_specs=pl.BlockSpec((None, seq_len, POOL_W), lambda b: (b, 0, 0)),
        scratch_shapes=[pltpu.VMEM((seq_len, HEAD_W), F32)],
        compiler_params=pltpu.CompilerParams(vmem_limit_bytes=V7X_VMEM_LIMIT),
        name="pool",
    )(proj3, pool_w_bf, pool_scale.reshape(1, POOL_W))


def _post_kernel(x_ref, o_ref, p_ref, ada_ref, nf_ref, fin_ref, wo_ref, wgu_ref, wd_ref, y_ref, *, final):
    mix = (jnp.dot(o_ref[...], wo_ref[:DN_W, :], preferred_element_type=F32)
           + jnp.dot(p_ref[...], wo_ref[DN_W:, :], preferred_element_type=F32))
    x = x_ref[...] + ada_ref[2:3, :] * mix
    h = _modulated_norm(x, nf_ref[...], ada_ref[3:4, :], ada_ref[4:5, :])
    gu = jnp.dot(h.astype(BF16), wgu_ref[...], preferred_element_type=F32)
    act = _silu(gu[:, :D_FF]) * gu[:, D_FF:]
    x = x + ada_ref[5:6, :] * jnp.dot(act.astype(BF16), wd_ref[...], preferred_element_type=F32)
    if final:
        x = x * lax.rsqrt(jnp.mean(x * x, axis=-1, keepdims=True) + EPS) * fin_ref[...]
    y_ref[...] = x


def _post_call(x2, o2, p2, ada_l, norm_ffn, final_norm, w_out_bf, w_gu_bf, w_down_bf, layer, seq_len, per_seq,
               final):
    n_tok = x2.shape[0]
    tm = _token_tile(TM_POST, n_tok, seq_len, per_seq)
    const = lambda shape: pl.BlockSpec((None,) + shape, lambda i: (layer, 0, 0), pipeline_mode=pl.Buffered(1))
    return pl.pallas_call(
        functools.partial(_post_kernel, final=final),
        out_shape=jax.ShapeDtypeStruct((n_tok, D_MODEL), F32),
        grid=(n_tok // tm,),
        in_specs=[
            pl.BlockSpec((tm, D_MODEL), lambda i: (i, 0)),
            pl.BlockSpec((tm, DN_W), lambda i: (i, 0)),
            pl.BlockSpec((tm, POOL_W), lambda i: (i, 0)),
            pl.BlockSpec((None, 6, D_MODEL), _ada_row_map(seq_len, tm, per_seq)),
            pl.BlockSpec((1, D_MODEL), lambda i: (0, 0)),
            pl.BlockSpec((1, D_MODEL), lambda i: (0, 0)),
            const((D_MODEL, D_MODEL)),
            const((D_MODEL, 2 * D_FF)),
            const((D_FF, D_MODEL)),
        ],
        out_specs=pl.BlockSpec((tm, D_MODEL), lambda i: (i, 0)),
        compiler_params=pltpu.CompilerParams(vmem_limit_bytes=V7X_VMEM_LIMIT),
        name="post",
    )(x2, o2, p2, ada_l, norm_ffn.reshape(1, D_MODEL), final_norm.reshape(1, D_MODEL),
      w_out_bf, w_gu_bf, w_down_bf)


def _gate_row(p):
    one = jnp.concatenate([jnp.zeros((2 * DN_HEADS,), F32), p.reshape(-1)])
    return jnp.tile(one, HEAD_W // GATE_COLS).reshape(1, HEAD_W)


def _stream_layer(x2, seq_len, rows, per_seq, ada_l, lw, state_delta, layer, final):
    n_seq = x2.shape[0] // seq_len
    proj = _inproj_call(x2, ada_l, lw["norm_mix"], lw["w_in"], layer, seq_len, per_seq)
    proj3 = proj.reshape(n_seq, seq_len, PROJ_W)
    o, st = _dn_call(proj3, lw["conv_w"], lw["alog_row"], lw["dtb_row"], lw["dn_norm"],
                     state_delta, layer, out_state=state_delta is None)
    p = _pool_call(proj3, lw["pool_w"], lw["pool_scale"], layer, rows)
    x2 = _post_call(x2, o.reshape(-1, DN_W), p.reshape(-1, POOL_W), ada_l, lw["norm_ffn"], lw["final_norm"],
                    lw["w_out"], lw["w_gu"], lw["w_down"], layer, seq_len, per_seq, final)
    return x2, st


def kernel(x_prompt, x_sample, c, state_delta, c_ctx, w_ada, b_ada, norm_mix, norm_ffn, w_in, conv_w, a_log, dt_bias, dn_norm, pool_w, pool_scale, w_out, w_gu, w_down, final_norm):
    depth = w_ada.shape[0]
    n_ctx, ctx_len, _ = x_prompt.shape
    n_lat, lat_len, _ = x_sample.shape
    assert 1 + n_lat <= ADA_ROWS
    c_all = jnp.concatenate([c_ctx[None, :], c, jnp.zeros((ADA_ROWS - 1 - n_lat, D_MODEL), F32)], axis=0)
    ada = _ada_call(c_all, w_ada, b_ada).reshape(depth, ADA_ROWS, 6, D_MODEL)

    qkvz = 4 * DN_W
    n_gate = 4 * DN_HEADS
    xp = x_prompt.reshape(-1, D_MODEL)
    xs = x_sample.reshape(-1, D_MODEL)
    states = []
    w_in_p = jnp.concatenate(
        [w_in[:, :, :qkvz], w_in[:, :, qkvz + n_gate:], w_in[:, :, qkvz:qkvz + n_gate],
         jnp.zeros((depth, D_MODEL, HEAD_W - n_gate), F32)], axis=2).astype(BF16)
    pool_w_bf, w_out_bf, w_gu_bf, w_down_bf = (w.astype(BF16) for w in (pool_w, w_out, w_gu, w_down))
    for l in range(depth):
        lw = dict(norm_mix=norm_mix[l], norm_ffn=norm_ffn[l], w_in=w_in_p, conv_w=conv_w[l],
                  alog_row=_gate_row(a_log[l]), dtb_row=_gate_row(dt_bias[l]), dn_norm=dn_norm[l],
                  pool_w=pool_w_bf, pool_scale=pool_scale[l], w_out=w_out_bf,
                  w_gu=w_gu_bf, w_down=w_down_bf, final_norm=final_norm)
        final = l == depth - 1
        xp, st = _stream_layer(xp, ctx_len, None, False, ada[l], lw, None, l, final)
        states.append(st)
        xs, _ = _stream_layer(xs, lat_len, lat_len // GRID_W, True, ada[l], lw, state_delta, l, final)
    new_state = jnp.stack(states, axis=1)
    return (xp.reshape(x_prompt.shape), xs.reshape(x_sample.shape), new_state)
```

```python
import functools

import jax
import jax.numpy as jnp
from jax import lax
from jax.experimental import pallas as pl
from jax.experimental.pallas import tpu as pltpu

F32 = jnp.float32
BF16 = jnp.bfloat16

D_MODEL = 1024
DN_HEADS = 4
HEAD_W = 128
DN_W = DN_HEADS * HEAD_W
POOL_GROUPS = 4
POOL_W = POOL_GROUPS * HEAD_W
POOL_WINDOWS = (2, 4, 8, 16)
GRID_W = 64
CONV_K = 5
CHUNK = 64
D_FF = 2816
EPS = 1e-6

PROJ_W = 3 * DN_W + DN_W + POOL_W + HEAD_W
POOL_BLK512 = 4
GATE_BLK = 20
GATE_COLS = 4 * DN_HEADS
ADA_ROWS = 16

V7X_VMEM_LIMIT = 56 * 1024 * 1024

TM_INPROJ = 512
TM_POST = 512
POOL_TB = 256
PREP_UNROLL = 8
DN_HEADS_PER_STEP = 2
YIELD_EVERY = 16
DN_CHUNKS_PER_STEP = 32


def _sigmoid(x):
    return 0.5 * jnp.tanh(0.5 * x) + 0.5


def _silu(x):
    return x * _sigmoid(x)


def _softplus(x):
    return jnp.maximum(x, 0.0) + jnp.log(1.0 + jnp.exp(-jnp.abs(x)))


def _bdot(a, b):
    return jnp.dot(a.astype(BF16), b.astype(BF16), preferred_element_type=F32)


def _iota(shape, dim):
    return lax.broadcasted_iota(jnp.int32, shape, dim)


def _ada_kernel(c_ref, w_ref, b_ref, o_ref):
    o_ref[...] = jnp.dot(_silu(c_ref[...]), w_ref[...], preferred_element_type=F32,
                         precision=lax.Precision.HIGHEST) + b_ref[...]


def _ada_call(c_all, w_ada, b_ada):
    depth, _, n_out = w_ada.shape
    tn = 1536
    return pl.pallas_call(
        _ada_kernel,
        out_shape=jax.ShapeDtypeStruct((depth, ADA_ROWS, n_out), F32),
        grid=(depth, n_out // tn),
        in_specs=[
            pl.BlockSpec((ADA_ROWS, D_MODEL), lambda l, j: (0, 0)),
            pl.BlockSpec((None, D_MODEL, tn), lambda l, j: (l, 0, j)),
            pl.BlockSpec((None, 1, tn), lambda l, j: (l, 0, j)),
        ],
        out_specs=pl.BlockSpec((None, ADA_ROWS, tn), lambda l, j: (l, 0, j)),
        compiler_params=pltpu.CompilerParams(vmem_limit_bytes=V7X_VMEM_LIMIT),
        name="ada",
    )(c_all, w_ada, b_ada.reshape(depth, 1, n_out))


def _modulated_norm(x, gain, shift, scale):
    y = x * lax.rsqrt(jnp.mean(x * x, axis=-1, keepdims=True) + EPS) * gain
    return y * (1.0 + scale) + shift


def _inproj_kernel(x_ref, ada_ref, nw_ref, w_ref, o_ref):
    h = _modulated_norm(x_ref[...], nw_ref[...], ada_ref[0:1, :], ada_ref[1:2, :])
    o_ref[...] = jnp.dot(h.astype(BF16), w_ref[...], preferred_element_type=F32)


def _token_tile(tm, n_tok, seq_len, per_seq):
    tm = min(tm, seq_len) if per_seq else tm
    assert n_tok % tm == 0 and (not per_seq or seq_len % tm == 0)
    return tm


def _ada_row_map(seq_len, tm, per_seq):
    if not per_seq:
        return lambda i: (0, 0, 0)
    return lambda i: (1 + (i * tm) // seq_len, 0, 0)


def _inproj_call(x2, ada_l, norm_w, w_in_p, layer, seq_len, per_seq):
    n_tok = x2.shape[0]
    tm = _token_tile(TM_INPROJ, n_tok, seq_len, per_seq)
    return pl.pallas_call(
        _inproj_kernel,
        out_shape=jax.ShapeDtypeStruct((n_tok, PROJ_W), F32),
        grid=(n_tok // tm,),
        in_specs=[
            pl.BlockSpec((tm, D_MODEL), lambda i: (i, 0)),
            pl.BlockSpec((None, 6, D_MODEL), _ada_row_map(seq_len, tm, per_seq)),
            pl.BlockSpec((1, D_MODEL), lambda i: (0, 0)),
            pl.BlockSpec((None, D_MODEL, PROJ_W), lambda i: (layer, 0, 0), pipeline_mode=pl.Buffered(1)),
        ],
        out_specs=pl.BlockSpec((tm, PROJ_W), lambda i: (i, 0)),
        compiler_params=pltpu.CompilerParams(vmem_limit_bytes=V7X_VMEM_LIMIT),
        name="inproj",
    )(x2, ada_l, norm_w.reshape(1, D_MODEL), w_in_p)


def _col_bcast(x, lane, idx):
    col = jnp.sum(jnp.where(lane == idx, x, 0.0), axis=-1, keepdims=True)
    return jnp.broadcast_to(col, x.shape)


def _each(fn, *lists):
    return [fn(*args) for args in zip(*lists)]


def _each_y(fn, *lists, every=YIELD_EVERY):
    out = []
    for i, args in enumerate(zip(*lists)):
        out.append(fn(*args))
        if i % every == every - 1:
            yield
    return out


def _interleave(*gens):
    gens = list(gens)
    while gens:
        for g in list(gens):
            try:
                next(g)
            except StopIteration:
                gens.remove(g)


def _dn_kernel(*refs, seq_len, ns, hp, has_s0, out_state, has_acc):
    n_ch = seq_len // CHUNK
    gc = min(PREP_UNROLL, ns * n_ch)
    n_grp = ns * n_ch // gc
    assert gc * GATE_COLS <= HEAD_W
    units = [(t, hh) for t in range(gc) for hh in range(hp)]
    nu = len(units)
    it = iter(refs)
    q_ref, k_ref, v_ref, z_ref, g_ref = (next(it) for _ in range(5))
    cwq_ref, cwk_ref, cwv_ref = (next(it) for _ in range(3))
    alog_ref, dtb_ref, dnn_ref = (next(it) for _ in range(3))
    s0_ref = next(it) if has_s0 else None
    if has_acc:
        next(it)
    o_ref = next(it)
    st_ref = next(it) if out_state else None
    wq_s, u_s, kdt_s, dl_s, at_s, vn_s, qs_s, s_s, ext_s, lbd_s, rhs_s = (next(it) for _ in range(11))

    head0 = pl.program_id(1) * hp
    row = _iota((CHUNK, 128), 0)
    lane = _iota((CHUNK, 128), 1)
    is_f = lane < CHUNK
    j = lane & (CHUNK - 1)
    incl = (is_f & (row >= j)) | (~is_f & (row <= j))
    strict = (is_f & (row > j)) | (~is_f & (row < j))
    diag = row == j
    eye_p = jnp.where(diag, 1.0, 0.0).astype(F32)
    couple = []
    for lvl in range(CHUNK.bit_length() - 1):
        rb, cb = row >> lvl, j >> lvl
        couple.append((is_f & ((rb & 1) == 1) & (cb == rb - 1)) | (~is_f & ((rb & 1) == 0) & (cb == rb + 1)))
    r128 = _iota((128, 128), 0)
    c128 = _iota((128, 128), 1)
    lane_f128 = c128 < CHUNK
    cum_sel = ((r128 < CHUNK) & (c128 <= r128)) | ((r128 >= CHUNK) & (c128 >= r128 - CHUNK))
    cum_mat = jnp.where(cum_sel, 1.0, 0.0).astype(F32)[:, :CHUNK]

    neg_a = -jnp.exp(alog_ref[...])
    dtb = dtb_ref[...]
    cols = lambda hh: slice(hh * HEAD_W, (hh + 1) * HEAD_W)

    def group_chunks(g):
        ms = [g * gc + t for t in range(gc)]
        return ms, [m // n_ch for m in ms], [m % n_ch for m in ms]

    def conv_silu(ref, cw_ref, hh, s, c, r0, slot):
        p0 = pl.multiple_of(jnp.maximum(r0 - 8, 0), 8)
        n0 = pl.multiple_of(jnp.minimum(r0 + CHUNK, seq_len - 8), 8)
        ext_s[slot, 0:8, :] = jnp.where(c > 0, ref[s, pl.ds(p0, 8), cols(hh)], 0.0)
        ext_s[slot, 8:8 + CHUNK, :] = ref[s, pl.ds(r0, CHUNK), cols(hh)]
        ext_s[slot, 8 + CHUNK:16 + CHUNK, :] = jnp.where(c < n_ch - 1, ref[s, pl.ds(n0, 8), cols(hh)], 0.0)
        acc = ext_s[slot, 6:6 + CHUNK, :] * cw_ref[0:1, cols(hh)]
        for t in range(1, CONV_K):
            acc = acc + ext_s[slot, 6 + t:6 + t + CHUNK, :] * cw_ref[t:t + 1, cols(hh)]
        return _silu(acc)

    def l2n(x):
        return x * lax.rsqrt(jnp.sum(x * x, axis=-1, keepdims=True) + EPS)

    def gates(ss, r0s):
        packed = g_ref[ss[0], pl.ds(r0s[0], CHUNK), :]
        for t in range(1, gc):
            packed = packed + pltpu.roll(g_ref[ss[t], pl.ds(r0s[t], CHUNK), :], GATE_COLS * t, axis=1)
        is_beta = (lane & (GATE_COLS - 1)) < 2 * DN_HEADS
        gact = jnp.where(is_beta, _sigmoid(packed), neg_a * _softplus(packed + dtb))
        cums = jnp.dot(cum_mat, gact, preferred_element_type=F32, precision=lax.Precision.HIGHEST)
        e_cum = jnp.exp(cums)
        e_rest = jnp.concatenate([jnp.exp(cums[CHUNK - 1:CHUNK] - cums[:CHUNK]),
                                  jnp.exp(cums[CHUNK:CHUNK + 1] - cums[CHUNK:])], axis=0)
        return gact, cums, e_cum, e_rest

    def front(g, slot):
        ms, ss, cs = group_chunks(g)
        r0s = [pl.multiple_of(c * CHUNK, CHUNK) for c in cs]
        gact, cums, e_cum, e_rest = gates(ss, r0s)
        yield
        q, k, v = [], [], []
        for n, (t, hh) in enumerate(units):
            q.append(l2n(conv_silu(q_ref, cwq_ref, hh, ss[t], cs[t], r0s[t], n)) * (HEAD_W ** -0.5))
            k.append(l2n(conv_silu(k_ref, cwk_ref, hh, ss[t], cs[t], r0s[t], nu + n)))
            v.append(conv_silu(v_ref, cwv_ref, hh, ss[t], cs[t], r0s[t], 2 * nu + n))
            yield
        col = lambda x, which: (lambda u: _col_bcast(
            x, lane, GATE_COLS * u[0] + which * DN_HEADS + head0 + u[1]))
        beta_f = yield from _each_y(col(gact, 0), units)
        beta_b = yield from _each_y(col(gact, 1), units)
        gc_f = yield from _each_y(col(cums[:CHUNK], 2), units)
        gc_b = yield from _each_y(col(cums[CHUNK:], 3), units)
        e_f = yield from _each_y(col(e_cum[:CHUNK], 2), units)
        e_b = yield from _each_y(col(e_cum[CHUNK:], 3), units)
        rest_f = yield from _each_y(col(e_rest[:CHUNK], 2), units)
        rest_b = yield from _each_y(col(e_rest[CHUNK:], 3), units)

        def decay_of(gf, gb):
            gc_col = jnp.where(is_f, gf, gb)
            gc_row = jnp.sum(jnp.where(diag, gc_col, 0.0), axis=0, keepdims=True)
            return jnp.where(incl, jnp.exp(jnp.where(incl, gc_col - gc_row, 0.0)), 0.0)

        decay = yield from _each_y(decay_of, gc_f, gc_b, every=2)

        def gram_of(qq, kk):
            kb = kk.astype(BF16)
            return lax.dot_general(jnp.concatenate([qq.astype(BF16), kb], axis=0),
                                   jnp.concatenate([kb, kb], axis=0),
                                   (((1,), (1,)), ((), ())), preferred_element_type=F32)

        gram = yield from _each_y(gram_of, q, k, every=2)
        for n, (t, hh) in enumerate(units):
            l2 = jnp.where(strict, jnp.where(is_f, beta_f[n], beta_b[n]) * gram[n][CHUNK:] * decay[n], 0.0)
            lbd_s[slot, n] = l2.astype(BF16)
            at_s[hh, ms[t]] = (gram[n][:CHUNK] * decay[n]).astype(BF16)
            if n % 2 == 1:
                yield
        for n, (t, hh) in enumerate(units):
            rhs_s[slot, n] = jnp.concatenate([
                jnp.concatenate([v[n] * beta_f[n], k[n] * beta_f[n] * e_f[n]], axis=1),
                jnp.concatenate([v[n] * beta_b[n], k[n] * beta_b[n] * e_b[n]], axis=1)],
                axis=0).astype(BF16)
            wq_s[hh, 0, ms[t], CHUNK:, :] = (q[n] * e_f[n]).astype(BF16)
            wq_s[hh, 1, ms[t], CHUNK:, :] = (q[n] * e_b[n]).astype(BF16)
            if n % 2 == 1:
                yield
        for n, (t, hh) in enumerate(units):
            kdec = jnp.concatenate([k[n] * rest_f[n], k[n] * rest_b[n]], axis=0)
            kdt_s[hh, ms[t]] = kdec.T.astype(BF16)
            dl_s[hh, 0, ms[t]] = jnp.broadcast_to(e_f[n][CHUNK - 1:CHUNK, :], (8, 128))
            dl_s[hh, 1, ms[t]] = jnp.broadcast_to(e_b[n][0:1, :], (8, 128))
            if n % 2 == 1:
                yield

    def back(g, slot):
        ms = [g * gc + t for t in range(gc)]
        l2 = [lbd_s[slot, n] for n in range(nu)]
        zero = jnp.zeros((CHUNK, 128), BF16)

        def bd(x):
            x = x.astype(BF16)
            return jnp.concatenate([jnp.where(is_f, x, zero), jnp.where(is_f, zero, x)], axis=0)

        pdot = lambda a, b_bd: jnp.dot(a.astype(BF16), b_bd, preferred_element_type=F32)
        inv = _each(lambda x: eye_p - jnp.where(couple[0], x, zero).astype(F32), l2)
        for lvl in range(1, len(couple)):
            c_bd = _each(lambda x: bd(jnp.where(couple[lvl], x, zero)), l2)
            xc = yield from _each_y(pdot, inv, c_bd)
            inv = yield from _each_y(lambda x, y: x - pdot(y, bd(x)), inv, xc)
        uw = yield from _each_y(lambda tt, n: jnp.dot(bd(tt), rhs_s[slot, n], preferred_element_type=F32),
                                inv, range(nu), every=2)
        for n, (t, hh) in enumerate(units):
            u_s[hh, 0, ms[t]] = uw[n][:CHUNK, :HEAD_W]
            u_s[hh, 1, ms[t]] = uw[n][CHUNK:, :HEAD_W]
            wq_s[hh, 0, ms[t], :CHUNK, :] = uw[n][:CHUNK, HEAD_W:].astype(BF16)
            wq_s[hh, 1, ms[t], :CHUNK, :] = uw[n][CHUNK:, HEAD_W:].astype(BF16)

    chains = [(s, hh) for s in range(ns) for hh in range(hp)]

    dnn = dnn_ref[...]

    def emit_out(c):
        r0 = c * CHUNK if isinstance(c, int) else pl.multiple_of(c * CHUNK, CHUNK)
        for s, hh in chains:
            m = s * n_ch + c
            vst = jnp.concatenate([vn_s[hh, 0, m], vn_s[hh, 1, m]], axis=0)
            o = qs_s[hh, m] + jnp.dot(at_s[hh, m], vst, preferred_element_type=F32)
            y = o * lax.rsqrt(jnp.mean(o * o, axis=-1, keepdims=True) + EPS) * dnn
            o_ref[s, pl.ds(r0, CHUNK), cols(hh)] = (
                y * _silu(z_ref[s, pl.ds(r0, CHUNK), cols(hh)])).astype(o_ref.dtype)

    def scan_body(first, with_out, i, carry):
        if with_out:
            emit_out(i - 1)
            emit_out(n_ch - i)
        todo = [(s, hh, d, s * n_ch + (i if d == 0 else n_ch - 1 - i)) for s, hh in chains for d in range(2)]
        st = [s_s[s * hp + hh, d] for s, hh, d, m in todo]
        r = [jnp.dot(wq_s[hh, d, m], x.astype(BF16), preferred_element_type=F32)
             for (s, hh, d, m), x in zip(todo, st)]
        vn = []
        for (s, hh, d, m), rr in zip(todo, r):
            vn.append((u_s[hh, d, m] - rr[:CHUNK]).astype(BF16))
            vn_s[hh, d, m] = vn[-1]
        upd = []
        for (s, hh, d, m), x in zip(todo, vn):
            kdt = kdt_s[hh, m]
            zero = jnp.zeros_like(kdt)
            kdt = jnp.where(lane_f128, kdt, zero) if d == 0 else jnp.where(lane_f128, zero, kdt)
            upd.append(jnp.dot(kdt, jnp.concatenate([x, x], axis=0), preferred_element_type=F32))
        for (s, hh, d, m), x, rr, up in zip(todo, st, r, upd):
            s_s[s * hp + hh, d] = x * dl_s[hh, d, m][0:1, :] + up
            qs_s[hh, m] = rr[CHUNK:] if first else qs_s[hh, m] + rr[CHUNK:]
        return carry

    half = n_ch // 2
    early = ns == 1 and n_grp >= 3 and gc <= half
    order = (lambda i: (i + n_grp - 1) % n_grp) if early else (lambda i: i)
    _interleave(front(order(0), 0))

    def prep_body(i, carry):
        slot = i & 1
        _interleave(back(order(i), slot), front(order(i + 1), 1 - slot))
        return carry

    lax.fori_loop(0, n_grp - 1, prep_body, 0)
    for s, hh in chains:
        for d in range(2):
            s_s[s * hp + hh, d] = s0_ref[s, d, hh] if has_s0 else jnp.zeros((HEAD_W, HEAD_W), F32)

    def early_scan():
        for i in range(gc):
            scan_body(True, False, i, 0)
            yield

    last = back(order(n_grp - 1), (n_grp - 1) & 1)
    _interleave(last, early_scan()) if early else _interleave(last)
    n_early = gc if early else 0

    lax.fori_loop(n_early, half, functools.partial(scan_body, True, False), 0)
    scan_body(False, False, half, 0)
    lax.fori_loop(half + 1, n_ch, functools.partial(scan_body, False, True), 0)
    emit_out(n_ch - 1)
    emit_out(0)
    if out_state:
        for s, hh in chains:
            for d in range(2):
                st_ref[s, d, hh] = s_s[s * hp + hh, d]


def _dn_call(proj3, conv_w, alog_row, dtb_row, dn_norm, state_delta, layer, depth, out_state, states_so_far):
    n_seq, seq_len, _ = proj3.shape
    n_ch = seq_len // CHUNK
    hp = DN_HEADS_PER_STEP
    ns = max(1, min(n_seq, DN_CHUNKS_PER_STEP // n_ch))
    assert n_seq % ns == 0
    tot = ns * n_ch
    nu = min(PREP_UNROLL, tot) * hp
    wid = hp * HEAD_W
    nblk = DN_HEADS // hp
    has_s0 = state_delta is not None
    act_mode = dict(pipeline_mode=pl.Buffered(1)) if out_state else {}
    tile = lambda off: pl.BlockSpec((ns, seq_len, wid), lambda b, h: (b, 0, off + h), **act_mode)
    cw = lambda off: pl.BlockSpec((CONV_K, wid), lambda b, h: (0, off + h))
    row = pl.BlockSpec((1, HEAD_W), lambda b, h: (0, 0))
    in_specs = [tile(0), tile(nblk), tile(2 * nblk), tile(3 * nblk),
                pl.BlockSpec((ns, seq_len, HEAD_W), lambda b, h: (b, 0, GATE_BLK), **act_mode),
                cw(0), cw(nblk), cw(2 * nblk), row, row, row]
    args = [proj3, proj3, proj3, proj3, proj3, conv_w, conv_w, conv_w, alog_row, dtb_row,
            dn_norm.reshape(1, HEAD_W)]
    if has_s0:
        in_specs.append(pl.BlockSpec((ns, None, 2, hp, HEAD_W, HEAD_W),
                                     lambda b, h: (b, layer, 0, h, 0, 0)))
        args.append(state_delta)
    out_shape = [jax.ShapeDtypeStruct((n_seq, seq_len, DN_W), BF16)]
    out_specs = [pl.BlockSpec((ns, seq_len, wid), lambda b, h: (b, 0, h))]
    aliases = {}
    if out_state:
        out_shape.append(jax.ShapeDtypeStruct((n_seq, depth, 2, DN_HEADS, HEAD_W, HEAD_W), F32))
        out_specs.append(pl.BlockSpec((ns, None, 2, hp, HEAD_W, HEAD_W), lambda b, h: (b, layer, 0, h, 0, 0)))
        if states_so_far is not None:
            in_specs.append(pl.BlockSpec(memory_space=pl.ANY))
            args.append(states_so_far)
            aliases = {len(args) - 1: 1}
    scratch = [
        pltpu.VMEM((hp, 2, tot, 2 * CHUNK, HEAD_W), BF16),
        pltpu.VMEM((hp, 2, tot, CHUNK, HEAD_W), F32),
        pltpu.VMEM((hp, tot, HEAD_W, 2 * CHUNK), BF16),
        pltpu.VMEM((hp, 2, tot, 8, HEAD_W), F32),
        pltpu.VMEM((hp, tot, CHUNK, 2 * CHUNK), BF16),
        pltpu.VMEM((hp, 2, tot, CHUNK, HEAD_W), BF16),
        pltpu.VMEM((hp, tot, CHUNK, HEAD_W), F32),
        pltpu.VMEM((ns * hp, 2, HEAD_W, HEAD_W), F32),
        pltpu.VMEM((3 * nu, CHUNK + 16, HEAD_W), F32),
        pltpu.VMEM((2, nu, CHUNK, 2 * CHUNK), BF16),
        pltpu.VMEM((2, nu, 2 * CHUNK, 2 * HEAD_W), BF16),
    ]
    outs = pl.pallas_call(
        functools.partial(_dn_kernel, seq_len=seq_len, ns=ns, hp=hp, has_s0=has_s0, out_state=out_state,
                          has_acc=bool(aliases)),
        out_shape=out_shape,
        grid=(n_seq // ns, nblk),
        in_specs=in_specs,
        out_specs=out_specs,
        scratch_shapes=scratch,
        input_output_aliases=aliases,
        compiler_params=pltpu.CompilerParams(vmem_limit_bytes=V7X_VMEM_LIMIT),
        name="deltanet",
    )(*args)
    return (outs[0], outs[1]) if out_state else (outs[0], None)


def _split2(x):
    hi = x.astype(BF16)
    lo = (x - hi.astype(F32)).astype(BF16)
    return hi, lo


def _window(pos, win, n):
    lo = jnp.clip(pos - win // 2, 0, n)
    hi = jnp.clip(pos - win // 2 + win, 0, n)
    return lo, hi


def _pool_kernel(u_ref, pw_ref, ps_ref, o_ref, m_s, *, seq_len, rows):
    period = GRID_W if rows else seq_len
    tb = min(POOL_TB, seq_len)
    r_i = _iota((tb, tb), 0)
    c_i = _iota((tb, tb), 1)
    shift = period.bit_length() - 1
    same_line = (r_i >> shift) == (c_i >> shift)
    pos_r = r_i & (period - 1)
    pos_c = c_i & (period - 1)
    pos_col = _iota((tb, HEAD_W), 0) & (period - 1)
    for g, win in enumerate(POOL_WINDOWS):
        cols = slice(g * HEAD_W, (g + 1) * HEAD_W)
        if rows:
            run = None
            prev_lo = prev_hi = 0
            for r in range(rows):
                lo, hi = max(r - win // 2, 0), min(r - win // 2 + win, rows)
                for a in range(prev_hi, hi):
                    slab = u_ref[a * GRID_W:(a + 1) * GRID_W, cols]
                    run = slab if run is None else run + slab
                for a in range(prev_lo, lo):
                    run = run - u_ref[a * GRID_W:(a + 1) * GRID_W, cols]
                prev_lo, prev_hi = lo, hi
                m_s[r * GRID_W:(r + 1) * GRID_W, :] = run / float(hi - lo)
        lo_r, hi_r = _window(pos_r, win, period)
        band = jnp.where(same_line & (pos_c >= lo_r) & (pos_c < hi_r), 1.0, 0.0).astype(BF16)
        lo_c, hi_c = _window(pos_col, win, period)
        inv_cnt = 1.0 / (hi_c - lo_c).astype(F32)
        pw = pw_ref[g]
        scale = ps_ref[:, cols]
        for t0 in range(0, seq_len, tb):
            ug = u_ref[t0:t0 + tb, cols]
            src = m_s[t0:t0 + tb, :] if rows else ug
            hi, lo = _split2(src)
            box = (jnp.dot(band, hi, preferred_element_type=F32)
                   + jnp.dot(band, lo, preferred_element_type=F32))
            mean = box * inv_cnt
            o_ref[t0:t0 + tb, cols] = (_bdot(mean - ug, pw) * scale).astype(o_ref.dtype)


def _pool_call(proj3, pool_w_bf, pool_scale, layer, rows):
    n_seq, seq_len, _ = proj3.shape
    return pl.pallas_call(
        functools.partial(_pool_kernel, seq_len=seq_len, rows=rows),
        out_shape=jax.ShapeDtypeStruct((n_seq, seq_len, POOL_W), BF16),
        grid=(n_seq,),
        in_specs=[
            pl.BlockSpec((None, seq_len, POOL_W), lambda b: (b, 0, POOL_BLK512)),
            pl.BlockSpec((None, POOL_GROUPS, HEAD_W, HEAD_W), lambda b: (layer, 0, 0, 0)),
            pl.BlockSpec((1, POOL_W), lambda b: (0, 0)),
        ],
        out_specs=pl.BlockSpec((None, seq_len, POOL_W), lambda b: (b, 0, 0)),
        scratch_shapes=[pltpu.VMEM((seq_len, HEAD_W), F32)],
        compiler_params=pltpu.CompilerParams(vmem_limit_bytes=V7X_VMEM_LIMIT),
        name="pool",
    )(proj3, pool_w_bf, pool_scale.reshape(1, POOL_W))


def _post_kernel(x_ref, o_ref, p_ref, ada_ref, nf_ref, fin_ref, wo_ref, wgu_ref, wd_ref, y_ref, *, final):
    mix = (jnp.dot(o_ref[...], wo_ref[:DN_W, :], preferred_element_type=F32)
           + jnp.dot(p_ref[...], wo_ref[DN_W:, :], preferred_element_type=F32))
    x = x_ref[...] + ada_ref[2:3, :] * mix
    h = _modulated_norm(x, nf_ref[...], ada_ref[3:4, :], ada_ref[4:5, :])
    gu = jnp.dot(h.astype(BF16), wgu_ref[...], preferred_element_type=F32)
    act = _silu(gu[:, :D_FF]) * gu[:, D_FF:]
    x = x + ada_ref[5:6, :] * jnp.dot(act.astype(BF16), wd_ref[...], preferred_element_type=F32)
    if final:
        x = x * lax.rsqrt(jnp.mean(x * x, axis=-1, keepdims=True) + EPS) * fin_ref[...]
    y_ref[...] = x


def _post_call(x2, o2, p2, ada_l, norm_ffn, final_norm, w_out_bf, w_gu_bf, w_down_bf, layer, seq_len, per_seq,
               final):
    n_tok = x2.shape[0]
    tm = _token_tile(TM_POST, n_tok, seq_len, per_seq)
    const = lambda shape: pl.BlockSpec((None,) + shape, lambda i: (layer, 0, 0), pipeline_mode=pl.Buffered(1))
    return pl.pallas_call(
        functools.partial(_post_kernel, final=final),
        out_shape=jax.ShapeDtypeStruct((n_tok, D_MODEL), F32),
        grid=(n_tok // tm,),
        in_specs=[
            pl.BlockSpec((tm, D_MODEL), lambda i: (i, 0)),
            pl.BlockSpec((tm, DN_W), lambda i: (i, 0)),
            pl.BlockSpec((tm, POOL_W), lambda i: (i, 0)),
            pl.BlockSpec((None, 6, D_MODEL), _ada_row_map(seq_len, tm, per_seq)),
            pl.BlockSpec((1, D_MODEL), lambda i: (0, 0)),
            pl.BlockSpec((1, D_MODEL), lambda i: (0, 0)),
            const((D_MODEL, D_MODEL)),
            const((D_MODEL, 2 * D_FF)),
            const((D_FF, D_MODEL)),
        ],
        out_specs=pl.BlockSpec((tm, D_MODEL), lambda i: (i, 0)),
        compiler_params=pltpu.CompilerParams(vmem_limit_bytes=V7X_VMEM_LIMIT),
        name="post",
    )(x2, o2, p2, ada_l, norm_ffn.reshape(1, D_MODEL), final_norm.reshape(1, D_MODEL),
      w_out_bf, w_gu_bf, w_down_bf)


def _gate_row(p):
    one = jnp.concatenate([jnp.zeros((2 * DN_HEADS,), F32), p.reshape(-1)])
    return jnp.tile(one, HEAD_W // GATE_COLS).reshape(1, HEAD_W)


def _stream_layer(x2, seq_len, rows, per_seq, ada_l, lw, state_delta, states_so_far, layer, depth, final):
    n_seq = x2.shape[0] // seq_len
    proj = _inproj_call(x2, ada_l, lw["norm_mix"], lw["w_in"], layer, seq_len, per_seq)
    proj3 = proj.reshape(n_seq, seq_len, PROJ_W)
    o, st = _dn_call(proj3, lw["conv_w"], lw["alog_row"], lw["dtb_row"], lw["dn_norm"],
                     state_delta, layer, depth, state_delta is None, states_so_far)
    p = _pool_call(proj3, lw["pool_w"], lw["pool_scale"], layer, rows)
    x2 = _post_call(x2, o.reshape(-1, DN_W), p.reshape(-1, POOL_W), ada_l, lw["norm_ffn"], lw["final_norm"],
                    lw["w_out"], lw["w_gu"], lw["w_down"], layer, seq_len, per_seq, final)
    return x2, st


def kernel(x_prompt, x_sample, c, state_delta, c_ctx, w_ada, b_ada, norm_mix, norm_ffn, w_in, conv_w, a_log, dt_bias, dn_norm, pool_w, pool_scale, w_out, w_gu, w_down, final_norm):
    depth = w_ada.shape[0]
    _, ctx_len, _ = x_prompt.shape
    n_lat, lat_len, _ = x_sample.shape
    assert 1 + n_lat <= ADA_ROWS
    c_all = jnp.concatenate([c_ctx[None, :], c, jnp.zeros((ADA_ROWS - 1 - n_lat, D_MODEL), F32)], axis=0)
    ada = _ada_call(c_all, w_ada, b_ada).reshape(depth, ADA_ROWS, 6, D_MODEL)

    qkvz = 4 * DN_W
    n_gate = 4 * DN_HEADS
    xp = x_prompt.reshape(-1, D_MODEL)
    xs = x_sample.reshape(-1, D_MODEL)
    new_state = None
    w_in_p = jnp.concatenate(
        [w_in[:, :, :qkvz], w_in[:, :, qkvz + n_gate:], w_in[:, :, qkvz:qkvz + n_gate],
         jnp.zeros((depth, D_MODEL, HEAD_W - n_gate), F32)], axis=2).astype(BF16)
    pool_w_bf, w_out_bf, w_gu_bf, w_down_bf = (w.astype(BF16) for w in (pool_w, w_out, w_gu, w_down))
    for l in range(depth):
        lw = dict(norm_mix=norm_mix[l], norm_ffn=norm_ffn[l], w_in=w_in_p, conv_w=conv_w[l],
                  alog_row=_gate_row(a_log[l]), dtb_row=_gate_row(dt_bias[l]), dn_norm=dn_norm[l],
                  pool_w=pool_w_bf, pool_scale=pool_scale[l], w_out=w_out_bf,
                  w_gu=w_gu_bf, w_down=w_down_bf, final_norm=final_norm)
        final = l == depth - 1
        xp, new_state = _stream_layer(xp, ctx_len, None, False, ada[l], lw, None, new_state, l, depth, final)
        xs, _ = _stream_layer(xs, lat_len, lat_len // GRID_W, True, ada[l], lw, state_delta, None, l, depth, final)
    return (xp.reshape(x_prompt.shape), xs.reshape(x_sample.shape), new_state)
```

```python
import functools

import jax
import jax.numpy as jnp
from jax import lax
from jax.experimental import pallas as pl
from jax.experimental.pallas import tpu as pltpu

F32 = jnp.float32
BF16 = jnp.bfloat16

D_MODEL = 1024
DN_HEADS = 4
HEAD_W = 128
DN_W = DN_HEADS * HEAD_W
POOL_GROUPS = 4
POOL_W = POOL_GROUPS * HEAD_W
POOL_WINDOWS = (2, 4, 8, 16)
GRID_W = 64
CONV_K = 5
CHUNK = 64
D_FF = 2816
EPS = 1e-6

PROJ_W = 3 * DN_W + DN_W + POOL_W + HEAD_W
POOL_BLK512 = 4
GATE_BLK = 20
GATE_COLS = 4 * DN_HEADS
ADA_ROWS = 16

V7X_VMEM_LIMIT = 56 * 1024 * 1024

TM_INPROJ = 512
TM_POST = 512
POOL_TB = 256
PREP_UNROLL = 8
DN_HEADS_PER_STEP = 2
YIELD_EVERY = 16
DN_CHUNKS_PER_STEP = 32


def _sigmoid(x):
    return 0.5 * jnp.tanh(0.5 * x) + 0.5


def _silu(x):
    return x * _sigmoid(x)


def _softplus(x):
    return jnp.maximum(x, 0.0) + jnp.log(1.0 + jnp.exp(-jnp.abs(x)))


def _bdot(a, b):
    return jnp.dot(a.astype(BF16), b.astype(BF16), preferred_element_type=F32)


def _iota(shape, dim):
    return lax.broadcasted_iota(jnp.int32, shape, dim)


def _ada_kernel(c_ref, w_ref, b_ref, o_ref):
    o_ref[...] = jnp.dot(_silu(c_ref[...]), w_ref[...], preferred_element_type=F32,
                         precision=lax.Precision.HIGHEST) + b_ref[...]


def _ada_call(c_all, w_ada, b_ada):
    depth, _, n_out = w_ada.shape
    tn = 1536
    return pl.pallas_call(
        _ada_kernel,
        out_shape=jax.ShapeDtypeStruct((depth, ADA_ROWS, n_out), F32),
        grid=(depth, n_out // tn),
        in_specs=[
            pl.BlockSpec((ADA_ROWS, D_MODEL), lambda l, j: (0, 0)),
            pl.BlockSpec((None, D_MODEL, tn), lambda l, j: (l, 0, j)),
            pl.BlockSpec((None, 1, tn), lambda l, j: (l, 0, j)),
        ],
        out_specs=pl.BlockSpec((None, ADA_ROWS, tn), lambda l, j: (l, 0, j)),
        compiler_params=pltpu.CompilerParams(vmem_limit_bytes=V7X_VMEM_LIMIT),
        name="ada",
    )(c_all, w_ada, b_ada.reshape(depth, 1, n_out))


def _modulated_norm(x, gain, shift, scale):
    y = x * lax.rsqrt(jnp.mean(x * x, axis=-1, keepdims=True) + EPS) * gain
    return y * (1.0 + scale) + shift


def _inproj_kernel(x_ref, ada_ref, nw_ref, w_ref, o_ref):
    h = _modulated_norm(x_ref[...], nw_ref[...], ada_ref[0:1, :], ada_ref[1:2, :])
    o_ref[...] = jnp.dot(h.astype(BF16), w_ref[...], preferred_element_type=F32)


def _token_tile(tm, n_tok, seq_len, per_seq):
    tm = min(tm, seq_len) if per_seq else tm
    assert n_tok % tm == 0 and (not per_seq or seq_len % tm == 0)
    return tm


def _ada_row_map(seq_len, tm, per_seq):
    if not per_seq:
        return lambda i: (0, 0, 0)
    return lambda i: (1 + (i * tm) // seq_len, 0, 0)


def _inproj_call(x2, ada_l, norm_w, w_in_p, layer, seq_len, per_seq):
    n_tok = x2.shape[0]
    tm = _token_tile(TM_INPROJ, n_tok, seq_len, per_seq)
    return pl.pallas_call(
        _inproj_kernel,
        out_shape=jax.ShapeDtypeStruct((n_tok, PROJ_W), F32),
        grid=(n_tok // tm,),
        in_specs=[
            pl.BlockSpec((tm, D_MODEL), lambda i: (i, 0)),
            pl.BlockSpec((None, 6, D_MODEL), _ada_row_map(seq_len, tm, per_seq)),
            pl.BlockSpec((1, D_MODEL), lambda i: (0, 0)),
            pl.BlockSpec((None, D_MODEL, PROJ_W), lambda i: (layer, 0, 0), pipeline_mode=pl.Buffered(1)),
        ],
        out_specs=pl.BlockSpec((tm, PROJ_W), lambda i: (i, 0)),
        compiler_params=pltpu.CompilerParams(vmem_limit_bytes=V7X_VMEM_LIMIT),
        name="inproj",
    )(x2, ada_l, norm_w.reshape(1, D_MODEL), w_in_p)


def _col_bcast(x, lane, idx):
    col = jnp.sum(jnp.where(lane == idx, x, 0.0), axis=-1, keepdims=True)
    return jnp.broadcast_to(col, x.shape)


def _each(fn, *lists):
    return [fn(*args) for args in zip(*lists)]


def _each_y(fn, *lists, every=YIELD_EVERY):
    out = []
    for i, args in enumerate(zip(*lists)):
        out.append(fn(*args))
        if i % every == every - 1:
            yield
    return out


def _interleave(*gens):
    gens = list(gens)
    while gens:
        for g in list(gens):
            try:
                next(g)
            except StopIteration:
                gens.remove(g)


def _dn_kernel(*refs, seq_len, ns, hp, has_s0, out_state, has_acc):
    n_ch = seq_len // CHUNK
    gc = min(PREP_UNROLL, ns * n_ch)
    n_grp = ns * n_ch // gc
    assert gc * GATE_COLS <= HEAD_W
    units = [(t, hh) for t in range(gc) for hh in range(hp)]
    nu = len(units)
    it = iter(refs)
    q_ref, k_ref, v_ref, z_ref, g_ref = (next(it) for _ in range(5))
    cwq_ref, cwk_ref, cwv_ref = (next(it) for _ in range(3))
    alog_ref, dtb_ref, dnn_ref = (next(it) for _ in range(3))
    s0_ref = next(it) if has_s0 else None
    if has_acc:
        next(it)
    o_ref = next(it)
    st_ref = next(it) if out_state else None
    wq_s, u_s, kdt_s, dl_s, at_s, vn_s, qs_s, s_s, ext_s, lbd_s, rhs_s = (next(it) for _ in range(11))

    head0 = pl.program_id(1) * hp
    row = _iota((CHUNK, 128), 0)
    lane = _iota((CHUNK, 128), 1)
    is_f = lane < CHUNK
    j = lane & (CHUNK - 1)
    incl = (is_f & (row >= j)) | (~is_f & (row <= j))
    strict = (is_f & (row > j)) | (~is_f & (row < j))
    diag = row == j
    eye_p = jnp.where(diag, 1.0, 0.0).astype(F32)
    couple = []
    for lvl in range(CHUNK.bit_length() - 1):
        rb, cb = row >> lvl, j >> lvl
        couple.append((is_f & ((rb & 1) == 1) & (cb == rb - 1)) | (~is_f & ((rb & 1) == 0) & (cb == rb + 1)))
    r128 = _iota((128, 128), 0)
    c128 = _iota((128, 128), 1)
    lane_f128 = c128 < CHUNK
    cum_sel = ((r128 < CHUNK) & (c128 <= r128)) | ((r128 >= CHUNK) & (c128 >= r128 - CHUNK))
    cum_mat = jnp.where(cum_sel, 1.0, 0.0).astype(F32)[:, :CHUNK]

    neg_a = -jnp.exp(alog_ref[...])
    dtb = dtb_ref[...]
    cols = lambda hh: slice(hh * HEAD_W, (hh + 1) * HEAD_W)

    def group_chunks(g):
        ms = [g * gc + t for t in range(gc)]
        return ms, [m // n_ch for m in ms], [m % n_ch for m in ms]

    def conv_silu(ref, cw_ref, hh, s, c, r0, slot):
        p0 = pl.multiple_of(jnp.maximum(r0 - 8, 0), 8)
        n0 = pl.multiple_of(jnp.minimum(r0 + CHUNK, seq_len - 8), 8)
        ext_s[slot, 0:8, :] = jnp.where(c > 0, ref[s, pl.ds(p0, 8), cols(hh)], 0.0)
        ext_s[slot, 8:8 + CHUNK, :] = ref[s, pl.ds(r0, CHUNK), cols(hh)]
        ext_s[slot, 8 + CHUNK:16 + CHUNK, :] = jnp.where(c < n_ch - 1, ref[s, pl.ds(n0, 8), cols(hh)], 0.0)
        acc = ext_s[slot, 6:6 + CHUNK, :] * cw_ref[0:1, cols(hh)]
        for t in range(1, CONV_K):
            acc = acc + ext_s[slot, 6 + t:6 + t + CHUNK, :] * cw_ref[t:t + 1, cols(hh)]
        return _silu(acc)

    def l2n(x):
        return x * lax.rsqrt(jnp.sum(x * x, axis=-1, keepdims=True) + EPS)

    def gates(ss, r0s):
        packed = g_ref[ss[0], pl.ds(r0s[0], CHUNK), :]
        for t in range(1, gc):
            packed = packed + pltpu.roll(g_ref[ss[t], pl.ds(r0s[t], CHUNK), :], GATE_COLS * t, axis=1)
        is_beta = (lane & (GATE_COLS - 1)) < 2 * DN_HEADS
        gact = jnp.where(is_beta, _sigmoid(packed), neg_a * _softplus(packed + dtb))
        cums = jnp.dot(cum_mat, gact, preferred_element_type=F32, precision=lax.Precision.HIGHEST)
        e_cum = jnp.exp(cums)
        e_rest = jnp.concatenate([jnp.exp(cums[CHUNK - 1:CHUNK] - cums[:CHUNK]),
                                  jnp.exp(cums[CHUNK:CHUNK + 1] - cums[CHUNK:])], axis=0)
        return gact, cums, e_cum, e_rest

    def front(g, slot):
        ms, ss, cs = group_chunks(g)
        r0s = [pl.multiple_of(c * CHUNK, CHUNK) for c in cs]
        gact, cums, e_cum, e_rest = gates(ss, r0s)
        yield
        qk, kk, decay, bdecay = [], [], [], []
        for n, (t, hh) in enumerate(units):
            col = lambda x, which: _col_bcast(x, lane, GATE_COLS * t + which * DN_HEADS + head0 + hh)
            beta_f, beta_b = col(gact, 0), col(gact, 1)
            gc_col = jnp.where(is_f, col(cums[:CHUNK], 2), col(cums[CHUNK:], 3))
            e_f, e_b = col(e_cum[:CHUNK], 2), col(e_cum[CHUNK:], 3)
            q = l2n(conv_silu(q_ref, cwq_ref, hh, ss[t], cs[t], r0s[t], n)) * (HEAD_W ** -0.5)
            k = l2n(conv_silu(k_ref, cwk_ref, hh, ss[t], cs[t], r0s[t], nu + n))
            v = conv_silu(v_ref, cwv_ref, hh, ss[t], cs[t], r0s[t], 2 * nu + n)
            rhs_s[slot, n] = jnp.concatenate([
                jnp.concatenate([v * beta_f, k * beta_f * e_f], axis=1),
                jnp.concatenate([v * beta_b, k * beta_b * e_b], axis=1)], axis=0).astype(BF16)
            wq_s[hh, 0, ms[t], CHUNK:, :] = (q * e_f).astype(BF16)
            wq_s[hh, 1, ms[t], CHUNK:, :] = (q * e_b).astype(BF16)
            kdec = jnp.concatenate([k * col(e_rest[:CHUNK], 2), k * col(e_rest[CHUNK:], 3)], axis=0)
            kdt_s[hh, ms[t]] = kdec.T.astype(BF16)
            dl_s[hh, 0, ms[t]] = jnp.broadcast_to(e_f[CHUNK - 1:CHUNK, :], (8, 128))
            dl_s[hh, 1, ms[t]] = jnp.broadcast_to(e_b[0:1, :], (8, 128))
            kb = k.astype(BF16)
            qk.append(jnp.concatenate([q.astype(BF16), kb], axis=0))
            kk.append(jnp.concatenate([kb, kb], axis=0))
            gc_row = jnp.sum(jnp.where(diag, gc_col, 0.0), axis=0, keepdims=True)
            decay.append(jnp.where(incl, jnp.exp(jnp.where(incl, gc_col - gc_row, 0.0)), 0.0))
            bdecay.append(jnp.where(strict, jnp.where(is_f, beta_f, beta_b) * decay[-1], 0.0))
            yield
        gram = yield from _each_y(
            lambda a, b: lax.dot_general(a, b, (((1,), (1,)), ((), ())), preferred_element_type=F32),
            qk, kk, every=2)
        for n, (t, hh) in enumerate(units):
            lbd_s[slot, n] = (bdecay[n] * gram[n][CHUNK:]).astype(BF16)
            at_s[hh, ms[t]] = (gram[n][:CHUNK] * decay[n]).astype(BF16)
            if n % 2 == 1:
                yield

    def back(g, slot):
        ms = [g * gc + t for t in range(gc)]
        l2 = [lbd_s[slot, n] for n in range(nu)]
        zero = jnp.zeros((CHUNK, 128), BF16)

        def bd(x):
            x = x.astype(BF16)
            return jnp.concatenate([jnp.where(is_f, x, zero), jnp.where(is_f, zero, x)], axis=0)

        pdot = lambda a, b_bd: jnp.dot(a.astype(BF16), b_bd, preferred_element_type=F32)
        inv = _each(lambda x: eye_p - jnp.where(couple[0], x, zero).astype(F32), l2)
        for lvl in range(1, len(couple)):
            c_bd = _each(lambda x: bd(jnp.where(couple[lvl], x, zero)), l2)
            xc = yield from _each_y(lambda a, b: pdot(a, b).astype(BF16), inv, c_bd)
            inv = yield from _each_y(lambda x, y: x - pdot(y, bd(x)), inv, xc)
        uw = yield from _each_y(lambda tt, n: jnp.dot(bd(tt), rhs_s[slot, n], preferred_element_type=F32),
                                inv, range(nu), every=2)
        for n, (t, hh) in enumerate(units):
            u_s[hh, 0, ms[t]] = uw[n][:CHUNK, :HEAD_W]
            u_s[hh, 1, ms[t]] = uw[n][CHUNK:, :HEAD_W]
            wq_s[hh, 0, ms[t], :CHUNK, :] = uw[n][:CHUNK, HEAD_W:].astype(BF16)
            wq_s[hh, 1, ms[t], :CHUNK, :] = uw[n][CHUNK:, HEAD_W:].astype(BF16)

    chains = [(s, hh) for s in range(ns) for hh in range(hp)]

    dnn = dnn_ref[...]

    def emit_out(c):
        r0 = c * CHUNK if isinstance(c, int) else pl.multiple_of(c * CHUNK, CHUNK)
        for s, hh in chains:
            m = s * n_ch + c
            vst = jnp.concatenate([vn_s[hh, 0, m], vn_s[hh, 1, m]], axis=0)
            o = qs_s[hh, m] + jnp.dot(at_s[hh, m], vst, preferred_element_type=F32)
            y = o * lax.rsqrt(jnp.mean(o * o, axis=-1, keepdims=True) + EPS) * dnn
            o_ref[s, pl.ds(r0, CHUNK), cols(hh)] = (
                y * _silu(z_ref[s, pl.ds(r0, CHUNK), cols(hh)])).astype(o_ref.dtype)

    def scan_body(first, with_out, i, carry):
        if with_out:
            emit_out(i - 1)
            emit_out(n_ch - i)
        todo = [(s, hh, d, s * n_ch + (i if d == 0 else n_ch - 1 - i)) for s, hh in chains for d in range(2)]
        st = [s_s[s * hp + hh, d] for s, hh, d, m in todo]
        r = [jnp.dot(wq_s[hh, d, m], x.astype(BF16), preferred_element_type=F32)
             for (s, hh, d, m), x in zip(todo, st)]
        vn = []
        for (s, hh, d, m), rr in zip(todo, r):
            vn.append((u_s[hh, d, m] - rr[:CHUNK]).astype(BF16))
            vn_s[hh, d, m] = vn[-1]
        upd = []
        for (s, hh, d, m), x in zip(todo, vn):
            kdt = kdt_s[hh, m]
            zero = jnp.zeros_like(kdt)
            kdt = jnp.where(lane_f128, kdt, zero) if d == 0 else jnp.where(lane_f128, zero, kdt)
            upd.append(jnp.dot(kdt, jnp.concatenate([x, x], axis=0), preferred_element_type=F32))
        for (s, hh, d, m), x, rr, up in zip(todo, st, r, upd):
            s_s[s * hp + hh, d] = x * dl_s[hh, d, m][0:1, :] + up
            qs_s[hh, m] = rr[CHUNK:] if first else qs_s[hh, m] + rr[CHUNK:]
        return carry

    half = n_ch // 2
    early = ns == 1 and n_grp >= 3 and gc <= half
    order = (lambda i: (i + n_grp - 1) % n_grp) if early else (lambda i: i)
    _interleave(front(order(0), 0))

    def prep_body(i, carry):
        slot = i & 1
        _interleave(back(order(i), slot), front(order(i + 1), 1 - slot))
        return carry

    lax.fori_loop(0, n_grp - 1, prep_body, 0)
    for s, hh in chains:
        for d in range(2):
            s_s[s * hp + hh, d] = s0_ref[s, d, hh] if has_s0 else jnp.zeros((HEAD_W, HEAD_W), F32)

    def early_scan():
        for i in range(gc):
            scan_body(True, False, i, 0)
            yield

    last = back(order(n_grp - 1), (n_grp - 1) & 1)
    _interleave(last, early_scan()) if early else _interleave(last)
    n_early = gc if early else 0

    lax.fori_loop(n_early, half, functools.partial(scan_body, True, False), 0)
    scan_body(False, False, half, 0)
    lax.fori_loop(half + 1, n_ch, functools.partial(scan_body, False, True), 0)
    emit_out(n_ch - 1)
    emit_out(0)
    if out_state:
        for s, hh in chains:
            for d in range(2):
                st_ref[s, d, hh] = s_s[s * hp + hh, d]


def _dn_call(proj3, conv_w, alog_row, dtb_row, dn_norm, state_delta, layer, depth, out_state, states_so_far):
    n_seq, seq_len, _ = proj3.shape
    n_ch = seq_len // CHUNK
    hp = DN_HEADS_PER_STEP
    ns = max(1, min(n_seq, DN_CHUNKS_PER_STEP // n_ch))
    assert n_seq % ns == 0
    tot = ns * n_ch
    nu = min(PREP_UNROLL, tot) * hp
    wid = hp * HEAD_W
    nblk = DN_HEADS // hp
    has_s0 = state_delta is not None
    act_mode = dict(pipeline_mode=pl.Buffered(1)) if out_state else {}
    tile = lambda off: pl.BlockSpec((ns, seq_len, wid), lambda b, h: (b, 0, off + h), **act_mode)
    cw = lambda off: pl.BlockSpec((CONV_K, wid), lambda b, h: (0, off + h))
    row = pl.BlockSpec((1, HEAD_W), lambda b, h: (0, 0))
    in_specs = [tile(0), tile(nblk), tile(2 * nblk), tile(3 * nblk),
                pl.BlockSpec((ns, seq_len, HEAD_W), lambda b, h: (b, 0, GATE_BLK), **act_mode),
                cw(0), cw(nblk), cw(2 * nblk), row, row, row]
    args = [proj3, proj3, proj3, proj3, proj3, conv_w, conv_w, conv_w, alog_row, dtb_row,
            dn_norm.reshape(1, HEAD_W)]
    if has_s0:
        in_specs.append(pl.BlockSpec((ns, None, 2, hp, HEAD_W, HEAD_W),
                                     lambda b, h: (b, layer, 0, h, 0, 0)))
        args.append(state_delta)
    out_shape = [jax.ShapeDtypeStruct((n_seq, seq_len, DN_W), BF16)]
    out_specs = [pl.BlockSpec((ns, seq_len, wid), lambda b, h: (b, 0, h))]
    aliases = {}
    if out_state:
        out_shape.append(jax.ShapeDtypeStruct((n_seq, depth, 2, DN_HEADS, HEAD_W, HEAD_W), F32))
        out_specs.append(pl.BlockSpec((ns, None, 2, hp, HEAD_W, HEAD_W), lambda b, h: (b, layer, 0, h, 0, 0)))
        if states_so_far is not None:
            in_specs.append(pl.BlockSpec(memory_space=pl.ANY))
            args.append(states_so_far)
            aliases = {len(args) - 1: 1}
    scratch = [
        pltpu.VMEM((hp, 2, tot, 2 * CHUNK, HEAD_W), BF16),
        pltpu.VMEM((hp, 2, tot, CHUNK, HEAD_W), F32),
        pltpu.VMEM((hp, tot, HEAD_W, 2 * CHUNK), BF16),
        pltpu.VMEM((hp, 2, tot, 8, HEAD_W), F32),
        pltpu.VMEM((hp, tot, CHUNK, 2 * CHUNK), BF16),
        pltpu.VMEM((hp, 2, tot, CHUNK, HEAD_W), BF16),
        pltpu.VMEM((hp, tot, CHUNK, HEAD_W), F32),
        pltpu.VMEM((ns * hp, 2, HEAD_W, HEAD_W), F32),
        pltpu.VMEM((3 * nu, CHUNK + 16, HEAD_W), F32),
        pltpu.VMEM((2, nu, CHUNK, 2 * CHUNK), BF16),
        pltpu.VMEM((2, nu, 2 * CHUNK, 2 * HEAD_W), BF16),
    ]
    outs = pl.pallas_call(
        functools.partial(_dn_kernel, seq_len=seq_len, ns=ns, hp=hp, has_s0=has_s0, out_state=out_state,
                          has_acc=bool(aliases)),
        out_shape=out_shape,
        grid=(n_seq // ns, nblk),
        in_specs=in_specs,
        out_specs=out_specs,
        scratch_shapes=scratch,
        input_output_aliases=aliases,
        compiler_params=pltpu.CompilerParams(vmem_limit_bytes=V7X_VMEM_LIMIT),
        name="deltanet",
    )(*args)
    return (outs[0], outs[1]) if out_state else (outs[0], None)


def _split2(x):
    hi = x.astype(BF16)
    lo = (x - hi.astype(F32)).astype(BF16)
    return hi, lo


def _window(pos, win, n):
    lo = jnp.clip(pos - win // 2, 0, n)
    hi = jnp.clip(pos - win // 2 + win, 0, n)
    return lo, hi


def _pool_kernel(u_ref, pw_ref, ps_ref, o_ref, m_s, *, seq_len, rows):
    period = GRID_W if rows else seq_len
    tb = min(POOL_TB, seq_len)
    r_i = _iota((tb, tb), 0)
    c_i = _iota((tb, tb), 1)
    shift = period.bit_length() - 1
    same_line = (r_i >> shift) == (c_i >> shift)
    pos_r = r_i & (period - 1)
    pos_c = c_i & (period - 1)
    pos_col = _iota((tb, HEAD_W), 0) & (period - 1)
    for g, win in enumerate(POOL_WINDOWS):
        cols = slice(g * HEAD_W, (g + 1) * HEAD_W)
        if rows:
            run = None
            prev_lo = prev_hi = 0
            for r in range(rows):
                lo, hi = max(r - win // 2, 0), min(r - win // 2 + win, rows)
                for a in range(prev_hi, hi):
                    slab = u_ref[a * GRID_W:(a + 1) * GRID_W, cols]
                    run = slab if run is None else run + slab
                for a in range(prev_lo, lo):
                    run = run - u_ref[a * GRID_W:(a + 1) * GRID_W, cols]
                prev_lo, prev_hi = lo, hi
                m_s[r * GRID_W:(r + 1) * GRID_W, :] = run / float(hi - lo)
        lo_r, hi_r = _window(pos_r, win, period)
        band = jnp.where(same_line & (pos_c >= lo_r) & (pos_c < hi_r), 1.0, 0.0).astype(BF16)
        lo_c, hi_c = _window(pos_col, win, period)
        inv_cnt = 1.0 / (hi_c - lo_c).astype(F32)
        pw = pw_ref[g]
        scale = ps_ref[:, cols]
        for t0 in range(0, seq_len, tb):
            ug = u_ref[t0:t0 + tb, cols]
            src = m_s[t0:t0 + tb, :] if rows else ug
            hi, lo = _split2(src)
            box = (jnp.dot(band, hi, preferred_element_type=F32)
                   + jnp.dot(band, lo, preferred_element_type=F32))
            mean = box * inv_cnt
            o_ref[t0:t0 + tb, cols] = (_bdot(mean - ug, pw) * scale).astype(o_ref.dtype)


def _pool_call(proj3, pool_w_bf, pool_scale, layer, rows):
    n_seq, seq_len, _ = proj3.shape
    return pl.pallas_call(
        functools.partial(_pool_kernel, seq_len=seq_len, rows=rows),
        out_shape=jax.ShapeDtypeStruct((n_seq, seq_len, POOL_W), BF16),
        grid=(n_seq,),
        in_specs=[
            pl.BlockSpec((None, seq_len, POOL_W), lambda b: (b, 0, POOL_BLK512)),
            pl.BlockSpec((None, POOL_GROUPS, HEAD_W, HEAD_W), lambda b: (layer, 0, 0, 0)),
            pl.BlockSpec((1, POOL_W), lambda b: (0, 0)),
        ],
        out_specs=pl.BlockSpec((None, seq_len, POOL_W), lambda b: (b, 0, 0)),
        scratch_shapes=[pltpu.VMEM((seq_len, HEAD_W), F32)],
        compiler_params=pltpu.CompilerParams(vmem_limit_bytes=V7X_VMEM_LIMIT),
        name="pool",
    )(proj3, pool_w_bf, pool_scale.reshape(1, POOL_W))


def _post_kernel(x_ref, o_ref, p_ref, ada_ref, nf_ref, fin_ref, wo_ref, wgu_ref, wd_ref, y_ref, *, final):
    mix = (jnp.dot(o_ref[...], wo_ref[:DN_W, :], preferred_element_type=F32)
           + jnp.dot(p_ref[...], wo_ref[DN_W:, :], preferred_element_type=F32))
    x = x_ref[...] + ada_ref[2:3, :] * mix
    h = _modulated_norm(x, nf_ref[...], ada_ref[3:4, :], ada_ref[4:5, :])
    gu = jnp.dot(h.astype(BF16), wgu_ref[...], preferred_element_type=F32)
    act = _silu(gu[:, :D_FF]) * gu[:, D_FF:]
    x = x + ada_ref[5:6, :] * jnp.dot(act.astype(BF16), wd_ref[...], preferred_element_type=F32)
    if final:
        x = x * lax.rsqrt(jnp.mean(x * x, axis=-1, keepdims=True) + EPS) * fin_ref[...]
    y_ref[...] = x


def _post_call(x2, o2, p2, ada_l, norm_ffn, final_norm, w_out_bf, w_gu_bf, w_down_bf, layer, seq_len, per_seq,
               final):
    n_tok = x2.shape[0]
    tm = _token_tile(TM_POST, n_tok, seq_len, per_seq)
    const = lambda shape: pl.BlockSpec((None,) + shape, lambda i: (layer, 0, 0), pipeline_mode=pl.Buffered(1))
    return pl.pallas_call(
        functools.partial(_post_kernel, final=final),
        out_shape=jax.ShapeDtypeStruct((n_tok, D_MODEL), F32),
        grid=(n_tok // tm,),
        in_specs=[
            pl.BlockSpec((tm, D_MODEL), lambda i: (i, 0)),
            pl.BlockSpec((tm, DN_W), lambda i: (i, 0)),
            pl.BlockSpec((tm, POOL_W), lambda i: (i, 0)),
            pl.BlockSpec((None, 6, D_MODEL), _ada_row_map(seq_len, tm, per_seq)),
            pl.BlockSpec((1, D_MODEL), lambda i: (0, 0)),
            pl.BlockSpec((1, D_MODEL), lambda i: (0, 0)),
            const((D_MODEL, D_MODEL)),
            const((D_MODEL, 2 * D_FF)),
            const((D_FF, D_MODEL)),
        ],
        out_specs=pl.BlockSpec((tm, D_MODEL), lambda i: (i, 0)),
        compiler_params=pltpu.CompilerParams(vmem_limit_bytes=V7X_VMEM_LIMIT),
        name="post",
    )(x2, o2, p2, ada_l, norm_ffn.reshape(1, D_MODEL), final_norm.reshape(1, D_MODEL),
      w_out_bf, w_gu_bf, w_down_bf)


def _gate_row(p):
    one = jnp.concatenate([jnp.zeros((2 * DN_HEADS,), F32), p.reshape(-1)])
    return jnp.tile(one, HEAD_W // GATE_COLS).reshape(1, HEAD_W)


def _stream_layer(x2, seq_len, rows, per_seq, ada_l, lw, state_delta, states_so_far, layer, depth, final):
    n_seq = x2.shape[0] // seq_len
    proj = _inproj_call(x2, ada_l, lw["norm_mix"], lw["w_in"], layer, seq_len, per_seq)
    proj3 = proj.reshape(n_seq, seq_len, PROJ_W)
    o, st = _dn_call(proj3, lw["conv_w"], lw["alog_row"], lw["dtb_row"], lw["dn_norm"],
                     state_delta, layer, depth, state_delta is None, states_so_far)
    p = _pool_call(proj3, lw["pool_w"], lw["pool_scale"], layer, rows)
    x2 = _post_call(x2, o.reshape(-1, DN_W), p.reshape(-1, POOL_W), ada_l, lw["norm_ffn"], lw["final_norm"],
                    lw["w_out"], lw["w_gu"], lw["w_down"], layer, seq_len, per_seq, final)
    return x2, st


def kernel(x_prompt, x_sample, c, state_delta, c_ctx, w_ada, b_ada, norm_mix, norm_ffn, w_in, conv_w, a_log, dt_bias, dn_norm, pool_w, pool_scale, w_out, w_gu, w_down, final_norm):
    depth = w_ada.shape[0]
    _, ctx_len, _ = x_prompt.shape
    n_lat, lat_len, _ = x_sample.shape
    assert 1 + n_lat <= ADA_ROWS
    c_all = jnp.concatenate([c_ctx[None, :], c, jnp.zeros((ADA_ROWS - 1 - n_lat, D_MODEL), F32)], axis=0)
    ada = _ada_call(c_all, w_ada, b_ada).reshape(depth, ADA_ROWS, 6, D_MODEL)

    qkvz = 4 * DN_W
    n_gate = 4 * DN_HEADS
    xp = x_prompt.reshape(-1, D_MODEL)
    xs = x_sample.reshape(-1, D_MODEL)
    new_state = None
    w_in_bf = w_in.astype(BF16)
    w_in_p = jnp.concatenate(
        [w_in_bf[:, :, :qkvz], w_in_bf[:, :, qkvz + n_gate:], w_in_bf[:, :, qkvz:qkvz + n_gate],
         jnp.zeros((depth, D_MODEL, HEAD_W - n_gate), BF16)], axis=2)
    pool_w_bf, w_out_bf, w_gu_bf, w_down_bf = (w.astype(BF16) for w in (pool_w, w_out, w_gu, w_down))
    for l in range(depth):
        lw = dict(norm_mix=norm_mix[l], norm_ffn=norm_ffn[l], w_in=w_in_p, conv_w=conv_w[l],
                  alog_row=_gate_row(a_log[l]), dtb_row=_gate_row(dt_bias[l]), dn_norm=dn_norm[l],
                  pool_w=pool_w_bf, pool_scale=pool_scale[l], w_out=w_out_bf,
                  w_gu=w_gu_bf, w_down=w_down_bf, final_norm=final_norm)
        final = l == depth - 1
        xp, new_state = _stream_layer(xp, ctx_len, None, False, ada[l], lw, None, new_state, l, depth, final)
        xs, _ = _stream_layer(xs, lat_len, lat_len // GRID_W, True, ada[l], lw, state_delta, None, l, depth, final)
    return (xp.reshape(x_prompt.shape), xs.reshape(x_sample.shape), new_state)
```

```python
import functools

import jax
import jax.numpy as jnp
from jax import lax
from jax.experimental import pallas as pl
from jax.experimental.pallas import tpu as pltpu

F32 = jnp.float32
BF16 = jnp.bfloat16

D_MODEL = 1024
DN_HEADS = 4
HEAD_W = 128
DN_W = DN_HEADS * HEAD_W
POOL_GROUPS = 4
POOL_W = POOL_GROUPS * HEAD_W
POOL_WINDOWS = (2, 4, 8, 16)
GRID_W = 64
CONV_K = 5
CHUNK = 64
D_FF = 2816
EPS = 1e-6

PROJ_W = 3 * DN_W + DN_W + POOL_W + HEAD_W
POOL_BLK512 = 4
GATE_BLK = 20
GATE_COLS = 4 * DN_HEADS
ADA_ROWS = 16

V7X_VMEM_LIMIT = 56 * 1024 * 1024

TM_INPROJ = 512
TM_POST = 512
POOL_TB = 256
PREP_UNROLL = 8
DN_HEADS_PER_STEP = 2
YIELD_EVERY = 16
DN_CHUNKS_PER_STEP = 32


def _sigmoid(x):
    return 0.5 * jnp.tanh(0.5 * x) + 0.5


def _silu(x):
    return x * _sigmoid(x)


def _softplus(x):
    return jnp.maximum(x, 0.0) + jnp.log(1.0 + jnp.exp(-jnp.abs(x)))


def _bdot(a, b):
    return jnp.dot(a.astype(BF16), b.astype(BF16), preferred_element_type=F32)


def _iota(shape, dim):
    return lax.broadcasted_iota(jnp.int32, shape, dim)


def _ada_kernel(c_ref, w_ref, b_ref, o_ref):
    o_ref[...] = jnp.dot(_silu(c_ref[...]), w_ref[...], preferred_element_type=F32,
                         precision=lax.Precision.HIGHEST) + b_ref[...]


def _ada_call(c_all, w_ada, b_ada):
    depth, _, n_out = w_ada.shape
    tn = 1536
    return pl.pallas_call(
        _ada_kernel,
        out_shape=jax.ShapeDtypeStruct((depth, ADA_ROWS, n_out), F32),
        grid=(depth, n_out // tn),
        in_specs=[
            pl.BlockSpec((ADA_ROWS, D_MODEL), lambda l, j: (0, 0)),
            pl.BlockSpec((None, D_MODEL, tn), lambda l, j: (l, 0, j)),
            pl.BlockSpec((None, 1, tn), lambda l, j: (l, 0, j)),
        ],
        out_specs=pl.BlockSpec((None, ADA_ROWS, tn), lambda l, j: (l, 0, j)),
        compiler_params=pltpu.CompilerParams(vmem_limit_bytes=V7X_VMEM_LIMIT),
        name="ada",
    )(c_all, w_ada, b_ada.reshape(depth, 1, n_out))


def _modulated_norm(x, gain, shift, scale):
    y = x * lax.rsqrt(jnp.mean(x * x, axis=-1, keepdims=True) + EPS) * gain
    return y * (1.0 + scale) + shift


def _inproj_kernel(x_ref, ada_ref, nw_ref, w_ref, o_ref):
    h = _modulated_norm(x_ref[...], nw_ref[...], ada_ref[0:1, :], ada_ref[1:2, :])
    o_ref[...] = jnp.dot(h.astype(BF16), w_ref[...], preferred_element_type=F32)


def _token_tile(tm, n_tok, seq_len, per_seq):
    tm = min(tm, seq_len) if per_seq else tm
    assert n_tok % tm == 0 and (not per_seq or seq_len % tm == 0)
    return tm


def _ada_row_map(seq_len, tm, per_seq):
    if not per_seq:
        return lambda i: (0, 0, 0)
    return lambda i: (1 + (i * tm) // seq_len, 0, 0)


def _inproj_call(x2, ada_l, norm_w, w_in_p, layer, seq_len, per_seq):
    n_tok = x2.shape[0]
    tm = _token_tile(TM_INPROJ, n_tok, seq_len, per_seq)
    return pl.pallas_call(
        _inproj_kernel,
        out_shape=jax.ShapeDtypeStruct((n_tok, PROJ_W), F32),
        grid=(n_tok // tm,),
        in_specs=[
            pl.BlockSpec((tm, D_MODEL), lambda i: (i, 0)),
            pl.BlockSpec((None, 6, D_MODEL), _ada_row_map(seq_len, tm, per_seq)),
            pl.BlockSpec((1, D_MODEL), lambda i: (0, 0)),
            pl.BlockSpec((None, D_MODEL, PROJ_W), lambda i: (layer, 0, 0), pipeline_mode=pl.Buffered(1)),
        ],
        out_specs=pl.BlockSpec((tm, PROJ_W), lambda i: (i, 0)),
        compiler_params=pltpu.CompilerParams(vmem_limit_bytes=V7X_VMEM_LIMIT),
        name="inproj",
    )(x2, ada_l, norm_w.reshape(1, D_MODEL), w_in_p)


def _col_bcast(x, lane, idx):
    col = jnp.sum(jnp.where(lane == idx, x, 0.0), axis=-1, keepdims=True)
    return jnp.broadcast_to(col, x.shape)


def _each(fn, *lists):
    return [fn(*args) for args in zip(*lists)]


def _each_y(fn, *lists, every=YIELD_EVERY):
    out = []
    for i, args in enumerate(zip(*lists)):
        out.append(fn(*args))
        if i % every == every - 1:
            yield
    return out


def _interleave(*gens):
    gens = list(gens)
    while gens:
        for g in list(gens):
            try:
                next(g)
            except StopIteration:
                gens.remove(g)


def _dn_kernel(*refs, seq_len, ns, hp, has_s0, out_state, has_acc):
    n_ch = seq_len // CHUNK
    gc = min(PREP_UNROLL, ns * n_ch)
    n_grp = ns * n_ch // gc
    assert gc * GATE_COLS <= HEAD_W
    units = [(t, hh) for t in range(gc) for hh in range(hp)]
    nu = len(units)
    it = iter(refs)
    q_ref, k_ref, v_ref, z_ref, g_ref = (next(it) for _ in range(5))
    cwq_ref, cwk_ref, cwv_ref = (next(it) for _ in range(3))
    alog_ref, dtb_ref, dnn_ref = (next(it) for _ in range(3))
    s0_ref = next(it) if has_s0 else None
    if has_acc:
        next(it)
    o_ref = next(it)
    st_ref = next(it) if out_state else None
    wq_s, u_s, kdt_s, dl_s, at_s, vn_s, qs_s, s_s, ext_s, lbd_s, rhs_s = (next(it) for _ in range(11))

    head0 = pl.program_id(1) * hp
    row = _iota((CHUNK, 128), 0)
    lane = _iota((CHUNK, 128), 1)
    is_f = lane < CHUNK
    j = lane & (CHUNK - 1)
    incl = (is_f & (row >= j)) | (~is_f & (row <= j))
    strict = (is_f & (row > j)) | (~is_f & (row < j))
    diag = row == j
    eye_p = jnp.where(diag, 1.0, 0.0).astype(F32)
    couple = []
    for lvl in range(CHUNK.bit_length() - 1):
        rb, cb = row >> lvl, j >> lvl
        couple.append((is_f & ((rb & 1) == 1) & (cb == rb - 1)) | (~is_f & ((rb & 1) == 0) & (cb == rb + 1)))
    r128 = _iota((128, 128), 0)
    c128 = _iota((128, 128), 1)
    lane_f128 = c128 < CHUNK
    cum_sel = ((r128 < CHUNK) & (c128 <= r128)) | ((r128 >= CHUNK) & (c128 >= r128 - CHUNK))
    cum_mat = jnp.where(cum_sel, 1.0, 0.0).astype(F32)[:, :CHUNK]

    neg_a = -jnp.exp(alog_ref[...])
    dtb = dtb_ref[...]
    cols = lambda hh: slice(hh * HEAD_W, (hh + 1) * HEAD_W)

    def group_chunks(g):
        ms = [g * gc + t for t in range(gc)]
        return ms, [m // n_ch for m in ms], [m % n_ch for m in ms]

    def conv_silu(ref, cw_ref, hh, s, c, r0, slot):
        p0 = pl.multiple_of(jnp.maximum(r0 - 8, 0), 8)
        n0 = pl.multiple_of(jnp.minimum(r0 + CHUNK, seq_len - 8), 8)
        ext_s[slot, 0:8, :] = jnp.where(c > 0, ref[s, pl.ds(p0, 8), cols(hh)], 0.0)
        ext_s[slot, 8:8 + CHUNK, :] = ref[s, pl.ds(r0, CHUNK), cols(hh)]
        ext_s[slot, 8 + CHUNK:16 + CHUNK, :] = jnp.where(c < n_ch - 1, ref[s, pl.ds(n0, 8), cols(hh)], 0.0)
        acc = ext_s[slot, 6:6 + CHUNK, :] * cw_ref[0:1, cols(hh)]
        for t in range(1, CONV_K):
            acc = acc + ext_s[slot, 6 + t:6 + t + CHUNK, :] * cw_ref[t:t + 1, cols(hh)]
        return _silu(acc)

    def l2n(x):
        return x * lax.rsqrt(jnp.sum(x * x, axis=-1, keepdims=True) + EPS)

    def gates(ss, r0s):
        packed = g_ref[ss[0], pl.ds(r0s[0], CHUNK), :]
        for t in range(1, gc):
            packed = packed + pltpu.roll(g_ref[ss[t], pl.ds(r0s[t], CHUNK), :], GATE_COLS * t, axis=1)
        is_beta = (lane & (GATE_COLS - 1)) < 2 * DN_HEADS
        gact = jnp.where(is_beta, _sigmoid(packed), neg_a * _softplus(packed + dtb))
        cums = jnp.dot(cum_mat, gact, preferred_element_type=F32, precision=lax.Precision.HIGHEST)
        e_cum = jnp.exp(cums)
        e_rest = jnp.concatenate([jnp.exp(cums[CHUNK - 1:CHUNK] - cums[:CHUNK]),
                                  jnp.exp(cums[CHUNK:CHUNK + 1] - cums[CHUNK:])], axis=0)
        return gact, cums, e_cum, e_rest

    def front(g, slot):
        ms, ss, cs = group_chunks(g)
        r0s = [pl.multiple_of(c * CHUNK, CHUNK) for c in cs]
        gact, cums, e_cum, e_rest = gates(ss, r0s)
        yield
        qk, kk, decay, bdecay = [], [], [], []
        for n, (t, hh) in enumerate(units):
            col = lambda x, which: _col_bcast(x, lane, GATE_COLS * t + which * DN_HEADS + head0 + hh)
            beta_f, beta_b = col(gact, 0), col(gact, 1)
            gc_col = jnp.where(is_f, col(cums[:CHUNK], 2), col(cums[CHUNK:], 3))
            e_f, e_b = col(e_cum[:CHUNK], 2), col(e_cum[CHUNK:], 3)
            q = l2n(conv_silu(q_ref, cwq_ref, hh, ss[t], cs[t], r0s[t], n)) * (HEAD_W ** -0.5)
            k = l2n(conv_silu(k_ref, cwk_ref, hh, ss[t], cs[t], r0s[t], nu + n))
            v = conv_silu(v_ref, cwv_ref, hh, ss[t], cs[t], r0s[t], 2 * nu + n)
            rhs_s[slot, n] = jnp.concatenate([
                jnp.concatenate([v * beta_f, k * beta_f * e_f], axis=1),
                jnp.concatenate([v * beta_b, k * beta_b * e_b], axis=1)], axis=0).astype(BF16)
            wq_s[hh, 0, ms[t], CHUNK:, :] = (q * e_f).astype(BF16)
            wq_s[hh, 1, ms[t], CHUNK:, :] = (q * e_b).astype(BF16)
            kdec = jnp.concatenate([k * col(e_rest[:CHUNK], 2), k * col(e_rest[CHUNK:], 3)], axis=0)
            kdt_s[hh, ms[t]] = kdec.T.astype(BF16)
            dl_s[hh, 0, ms[t]] = jnp.broadcast_to(e_f[CHUNK - 1:CHUNK, :], (8, 128))
            dl_s[hh, 1, ms[t]] = jnp.broadcast_to(e_b[0:1, :], (8, 128))
            kb = k.astype(BF16)
            qk.append(jnp.concatenate([q.astype(BF16), kb], axis=0))
            kk.append(jnp.concatenate([kb, kb], axis=0))
            gc_row = jnp.sum(jnp.where(diag, gc_col, 0.0), axis=0, keepdims=True)
            decay.append(jnp.where(incl, jnp.exp(jnp.where(incl, gc_col - gc_row, 0.0)), 0.0))
            bdecay.append(jnp.where(strict, jnp.where(is_f, beta_f, beta_b) * decay[-1], 0.0))
            yield
        gram = yield from _each_y(
            lambda a, b: lax.dot_general(a, b, (((1,), (1,)), ((), ())), preferred_element_type=F32),
            qk, kk, every=2)
        for n, (t, hh) in enumerate(units):
            lbd_s[slot, n] = (bdecay[n] * gram[n][CHUNK:]).astype(BF16)
            at_s[hh, ms[t]] = (gram[n][:CHUNK] * decay[n]).astype(BF16)
            if n % 2 == 1:
                yield

    def back(g, slot):
        ms = [g * gc + t for t in range(gc)]
        l2 = [lbd_s[slot, n] for n in range(nu)]
        zero = jnp.zeros((CHUNK, 128), BF16)

        def bd(x):
            x = x.astype(BF16)
            return jnp.concatenate([jnp.where(is_f, x, zero), jnp.where(is_f, zero, x)], axis=0)

        pdot = lambda a, b_bd: jnp.dot(a.astype(BF16), b_bd, preferred_element_type=F32)
        inv = _each(lambda x: eye_p - jnp.where(couple[0], x, zero).astype(F32), l2)
        for lvl in range(1, len(couple)):
            c_bd = _each(lambda x: bd(jnp.where(couple[lvl], x, zero)), l2)
            xc = yield from _each_y(lambda a, b: pdot(a, b).astype(BF16), inv, c_bd)
            inv = yield from _each_y(lambda x, y: x - pdot(y, bd(x)), inv, xc)
        uw = yield from _each_y(lambda tt, n: jnp.dot(bd(tt), rhs_s[slot, n], preferred_element_type=F32),
                                inv, range(nu), every=2)
        for n, (t, hh) in enumerate(units):
            u_s[hh, 0, ms[t]] = uw[n][:CHUNK, :HEAD_W]
            u_s[hh, 1, ms[t]] = uw[n][CHUNK:, :HEAD_W]
            wq_s[hh, 0, ms[t], :CHUNK, :] = uw[n][:CHUNK, HEAD_W:].astype(BF16)
            wq_s[hh, 1, ms[t], :CHUNK, :] = uw[n][CHUNK:, HEAD_W:].astype(BF16)

    chains = [(s, hh) for s in range(ns) for hh in range(hp)]

    dnn = dnn_ref[...]

    def emit_out(c):
        r0 = c * CHUNK if isinstance(c, int) else pl.multiple_of(c * CHUNK, CHUNK)
        for s, hh in chains:
            m = s * n_ch + c
            vst = jnp.concatenate([vn_s[hh, 0, m], vn_s[hh, 1, m]], axis=0)
            o = qs_s[hh, m] + jnp.dot(at_s[hh, m], vst, preferred_element_type=F32)
            y = o * lax.rsqrt(jnp.mean(o * o, axis=-1, keepdims=True) + EPS) * dnn
            o_ref[s, pl.ds(r0, CHUNK), cols(hh)] = (
                y * _silu(z_ref[s, pl.ds(r0, CHUNK), cols(hh)])).astype(o_ref.dtype)

    def scan_body(first, with_out, i, carry):
        if with_out:
            emit_out(i - 1)
            emit_out(n_ch - i)
        todo = [(s, hh, d, s * n_ch + (i if d == 0 else n_ch - 1 - i)) for s, hh in chains for d in range(2)]
        st = [s_s[s * hp + hh, d] for s, hh, d, m in todo]
        r = [jnp.dot(wq_s[hh, d, m], x.astype(BF16), preferred_element_type=F32)
             for (s, hh, d, m), x in zip(todo, st)]
        vn = []
        for (s, hh, d, m), rr in zip(todo, r):
            vn.append((u_s[hh, d, m] - rr[:CHUNK]).astype(BF16))
            vn_s[hh, d, m] = vn[-1]
        upd = []
        for (s, hh, d, m), x in zip(todo, vn):
            kdt = kdt_s[hh, m]
            zero = jnp.zeros_like(kdt)
            kdt = jnp.where(lane_f128, kdt, zero) if d == 0 else jnp.where(lane_f128, zero, kdt)
            upd.append(jnp.dot(kdt, jnp.concatenate([x, x], axis=0), preferred_element_type=F32))
        for (s, hh, d, m), x, rr, up in zip(todo, st, r, upd):
            s_s[s * hp + hh, d] = x * dl_s[hh, d, m][0:1, :] + up
            qs_s[hh, m] = rr[CHUNK:] if first else qs_s[hh, m] + rr[CHUNK:]
        return carry

    half = n_ch // 2
    early = ns == 1 and n_grp >= 3 and gc <= half
    order = (lambda i: (i + n_grp - 1) % n_grp) if early else (lambda i: i)
    _interleave(front(order(0), 0))

    def prep_body(i, carry):
        slot = i & 1
        _interleave(back(order(i), slot), front(order(i + 1), 1 - slot))
        return carry

    lax.fori_loop(0, n_grp - 1, prep_body, 0)
    for s, hh in chains:
        for d in range(2):
            s_s[s * hp + hh, d] = s0_ref[s, d, hh] if has_s0 else jnp.zeros((HEAD_W, HEAD_W), F32)

    def early_scan():
        for i in range(gc):
            scan_body(True, False, i, 0)
            yield

    last = back(order(n_grp - 1), (n_grp - 1) & 1)
    _interleave(last, early_scan()) if early else _interleave(last)
    n_early = gc if early else 0

    lax.fori_loop(n_early, half, functools.partial(scan_body, True, False), 0)
    scan_body(False, False, half, 0)
    lax.fori_loop(half + 1, n_ch, functools.partial(scan_body, False, True), 0)
    emit_out(n_ch - 1)
    emit_out(0)
    if out_state:
        for s, hh in chains:
            for d in range(2):
                st_ref[s, d, hh] = s_s[s * hp + hh, d]


def _dn_call(proj3, conv_w, alog_row, dtb_row, dn_norm, state_delta, layer, depth, out_state, states_so_far):
    n_seq, seq_len, _ = proj3.shape
    n_ch = seq_len // CHUNK
    hp = DN_HEADS_PER_STEP
    ns = max(1, min(n_seq, DN_CHUNKS_PER_STEP // n_ch))
    assert n_seq % ns == 0
    tot = ns * n_ch
    nu = min(PREP_UNROLL, tot) * hp
    wid = hp * HEAD_W
    nblk = DN_HEADS // hp
    has_s0 = state_delta is not None
    tile = lambda off: pl.BlockSpec((ns, seq_len, wid), lambda b, h: (b, 0, off + h))
    cw = lambda off: pl.BlockSpec((CONV_K, wid), lambda b, h: (0, off + h))
    row = pl.BlockSpec((1, HEAD_W), lambda b, h: (0, 0))
    in_specs = [tile(0), tile(nblk), tile(2 * nblk), tile(3 * nblk),
                pl.BlockSpec((ns, seq_len, HEAD_W), lambda b, h: (b, 0, GATE_BLK)),
                cw(0), cw(nblk), cw(2 * nblk), row, row, row]
    args = [proj3, proj3, proj3, proj3, proj3, conv_w, conv_w, conv_w, alog_row, dtb_row,
            dn_norm.reshape(1, HEAD_W)]
    if has_s0:
        in_specs.append(pl.BlockSpec((ns, None, 2, hp, HEAD_W, HEAD_W),
                                     lambda b, h: (b, layer, 0, h, 0, 0)))
        args.append(state_delta)
    out_shape = [jax.ShapeDtypeStruct((n_seq, seq_len, DN_W), BF16)]
    out_specs = [pl.BlockSpec((ns, seq_len, wid), lambda b, h: (b, 0, h))]
    aliases = {}
    if out_state:
        out_shape.append(jax.ShapeDtypeStruct((n_seq, depth, 2, DN_HEADS, HEAD_W, HEAD_W), F32))
        out_specs.append(pl.BlockSpec((ns, None, 2, hp, HEAD_W, HEAD_W), lambda b, h: (b, layer, 0, h, 0, 0)))
        if states_so_far is not None:
            in_specs.append(pl.BlockSpec(memory_space=pl.ANY))
            args.append(states_so_far)
            aliases = {len(args) - 1: 1}
    scratch = [
        pltpu.VMEM((hp, 2, tot, 2 * CHUNK, HEAD_W), BF16),
        pltpu.VMEM((hp, 2, tot, CHUNK, HEAD_W), F32),
        pltpu.VMEM((hp, tot, HEAD_W, 2 * CHUNK), BF16),
        pltpu.VMEM((hp, 2, tot, 8, HEAD_W), F32),
        pltpu.VMEM((hp, tot, CHUNK, 2 * CHUNK), BF16),
        pltpu.VMEM((hp, 2, tot, CHUNK, HEAD_W), BF16),
        pltpu.VMEM((hp, tot, CHUNK, HEAD_W), F32),
        pltpu.VMEM((ns * hp, 2, HEAD_W, HEAD_W), F32),
        pltpu.VMEM((3 * nu, CHUNK + 16, HEAD_W), F32),
        pltpu.VMEM((2, nu, CHUNK, 2 * CHUNK), BF16),
        pltpu.VMEM((2, nu, 2 * CHUNK, 2 * HEAD_W), BF16),
    ]
    outs = pl.pallas_call(
        functools.partial(_dn_kernel, seq_len=seq_len, ns=ns, hp=hp, has_s0=has_s0, out_state=out_state,
                          has_acc=bool(aliases)),
        out_shape=out_shape,
        grid=(n_seq // ns, nblk),
        in_specs=in_specs,
        out_specs=out_specs,
        scratch_shapes=scratch,
        input_output_aliases=aliases,
        compiler_params=pltpu.CompilerParams(vmem_limit_bytes=V7X_VMEM_LIMIT),
        name="deltanet",
    )(*args)
    return (outs[0], outs[1]) if out_state else (outs[0], None)


def _split2(x):
    hi = x.astype(BF16)
    lo = (x - hi.astype(F32)).astype(BF16)
    return hi, lo


def _window(pos, win, n):
    lo = jnp.clip(pos - win // 2, 0, n)
    hi = jnp.clip(pos - win // 2 + win, 0, n)
    return lo, hi


def _pool_kernel(u_ref, pw_ref, ps_ref, o_ref, m_s, *, seq_len, rows):
    period = GRID_W if rows else seq_len
    tb = min(POOL_TB, seq_len)
    r_i = _iota((tb, tb), 0)
    c_i = _iota((tb, tb), 1)
    shift = period.bit_length() - 1
    same_line = (r_i >> shift) == (c_i >> shift)
    pos_r = r_i & (period - 1)
    pos_c = c_i & (period - 1)
    pos_col = _iota((tb, HEAD_W), 0) & (period - 1)
    for g, win in enumerate(POOL_WINDOWS):
        cols = slice(g * HEAD_W, (g + 1) * HEAD_W)
        if rows:
            run = None
            prev_lo = prev_hi = 0
            for r in range(rows):
                lo, hi = max(r - win // 2, 0), min(r - win // 2 + win, rows)
                for a in range(prev_hi, hi):
                    slab = u_ref[a * GRID_W:(a + 1) * GRID_W, cols]
                    run = slab if run is None else run + slab
                for a in range(prev_lo, lo):
                    run = run - u_ref[a * GRID_W:(a + 1) * GRID_W, cols]
                prev_lo, prev_hi = lo, hi
                m_s[r * GRID_W:(r + 1) * GRID_W, :] = run / float(hi - lo)
        lo_r, hi_r = _window(pos_r, win, period)
        band = jnp.where(same_line & (pos_c >= lo_r) & (pos_c < hi_r), 1.0, 0.0).astype(BF16)
        lo_c, hi_c = _window(pos_col, win, period)
        inv_cnt = 1.0 / (hi_c - lo_c).astype(F32)
        pw = pw_ref[g]
        scale = ps_ref[:, cols]
        for t0 in range(0, seq_len, tb):
            ug = u_ref[t0:t0 + tb, cols]
            src = m_s[t0:t0 + tb, :] if rows else ug
            hi, lo = _split2(src)
            box = (jnp.dot(band, hi, preferred_element_type=F32)
                   + jnp.dot(band, lo, preferred_element_type=F32))
            mean = box * inv_cnt
            o_ref[t0:t0 + tb, cols] = (_bdot(mean - ug, pw) * scale).astype(o_ref.dtype)


def _pool_call(proj3, pool_w_bf, pool_scale, layer, rows):
    n_seq, seq_len, _ = proj3.shape
    return pl.pallas_call(
        functools.partial(_pool_kernel, seq_len=seq_len, rows=rows),
        out_shape=jax.ShapeDtypeStruct((n_seq, seq_len, POOL_W), BF16),
        grid=(n_seq,),
        in_specs=[
            pl.BlockSpec((None, seq_len, POOL_W), lambda b: (b, 0, POOL_BLK512)),
            pl.BlockSpec((None, POOL_GROUPS, HEAD_W, HEAD_W), lambda b: (layer, 0, 0, 0)),
            pl.BlockSpec((1, POOL_W), lambda b: (0, 0)),
        ],
        out_specs=pl.BlockSpec((None, seq_len, POOL_W), lambda b: (b, 0, 0)),
        scratch_shapes=[pltpu.VMEM((seq_len, HEAD_W), F32)],
        compiler_params=pltpu.CompilerParams(vmem_limit_bytes=V7X_VMEM_LIMIT),
        name="pool",
    )(proj3, pool_w_bf, pool_scale.reshape(1, POOL_W))


def _post_kernel(x_ref, o_ref, p_ref, ada_ref, nf_ref, fin_ref, wo_ref, wgu_ref, wd_ref, y_ref, *, final):
    mix = (jnp.dot(o_ref[...], wo_ref[:DN_W, :], preferred_element_type=F32)
           + jnp.dot(p_ref[...], wo_ref[DN_W:, :], preferred_element_type=F32))
    x = x_ref[...] + ada_ref[2:3, :] * mix
    h = _modulated_norm(x, nf_ref[...], ada_ref[3:4, :], ada_ref[4:5, :])
    gu = jnp.dot(h.astype(BF16), wgu_ref[...], preferred_element_type=F32)
    act = _silu(gu[:, :D_FF]) * gu[:, D_FF:]
    x = x + ada_ref[5:6, :] * jnp.dot(act.astype(BF16), wd_ref[...], preferred_element_type=F32)
    if final:
        x = x * lax.rsqrt(jnp.mean(x * x, axis=-1, keepdims=True) + EPS) * fin_ref[...]
    y_ref[...] = x


def _post_call(x2, o2, p2, ada_l, norm_ffn, final_norm, w_out_bf, w_gu_bf, w_down_bf, layer, seq_len, per_seq,
               final):
    n_tok = x2.shape[0]
    tm = _token_tile(TM_POST, n_tok, seq_len, per_seq)
    const = lambda shape: pl.BlockSpec((None,) + shape, lambda i: (layer, 0, 0), pipeline_mode=pl.Buffered(1))
    return pl.pallas_call(
        functools.partial(_post_kernel, final=final),
        out_shape=jax.ShapeDtypeStruct((n_tok, D_MODEL), F32),
        grid=(n_tok // tm,),
        in_specs=[
            pl.BlockSpec((tm, D_MODEL), lambda i: (i, 0)),
            pl.BlockSpec((tm, DN_W), lambda i: (i, 0)),
            pl.BlockSpec((tm, POOL_W), lambda i: (i, 0)),
            pl.BlockSpec((None, 6, D_MODEL), _ada_row_map(seq_len, tm, per_seq)),
            pl.BlockSpec((1, D_MODEL), lambda i: (0, 0)),
            pl.BlockSpec((1, D_MODEL), lambda i: (0, 0)),
            const((D_MODEL, D_MODEL)),
            const((D_MODEL, 2 * D_FF)),
            const((D_FF, D_MODEL)),
        ],
        out_specs=pl.BlockSpec((tm, D_MODEL), lambda i: (i, 0)),
        compiler_params=pltpu.CompilerParams(vmem_limit_bytes=V7X_VMEM_LIMIT),
        name="post",
    )(x2, o2, p2, ada_l, norm_ffn.reshape(1, D_MODEL), final_norm.reshape(1, D_MODEL),
      w_out_bf, w_gu_bf, w_down_bf)


def _gate_row(p):
    one = jnp.concatenate([jnp.zeros((2 * DN_HEADS,), F32), p.reshape(-1)])
    return jnp.tile(one, HEAD_W // GATE_COLS).reshape(1, HEAD_W)


def _stream_layer(x2, seq_len, rows, per_seq, ada_l, lw, state_delta, states_so_far, layer, depth, final):
    n_seq = x2.shape[0] // seq_len
    proj = _inproj_call(x2, ada_l, lw["norm_mix"], lw["w_in"], layer, seq_len, per_seq)
    proj3 = proj.reshape(n_seq, seq_len, PROJ_W)
    o, st = _dn_call(proj3, lw["conv_w"], lw["alog_row"], lw["dtb_row"], lw["dn_norm"],
                     state_delta, layer, depth, state_delta is None, states_so_far)
    p = _pool_call(proj3, lw["pool_w"], lw["pool_scale"], layer, rows)
    x2 = _post_call(x2, o.reshape(-1, DN_W), p.reshape(-1, POOL_W), ada_l, lw["norm_ffn"], lw["final_norm"],
                    lw["w_out"], lw["w_gu"], lw["w_down"], layer, seq_len, per_seq, final)
    return x2, st


def kernel(x_prompt, x_sample, c, state_delta, c_ctx, w_ada, b_ada, norm_mix, norm_ffn, w_in, conv_w, a_log, dt_bias, dn_norm, pool_w, pool_scale, w_out, w_gu, w_down, final_norm):
    depth = w_ada.shape[0]
    _, ctx_len, _ = x_prompt.shape
    n_lat, lat_len, _ = x_sample.shape
    assert 1 + n_lat <= ADA_ROWS
    c_all = jnp.concatenate([c_ctx[None, :], c, jnp.zeros((ADA_ROWS - 1 - n_lat, D_MODEL), F32)], axis=0)
    ada = _ada_call(c_all, w_ada, b_ada).reshape(depth, ADA_ROWS, 6, D_MODEL)

    qkvz = 4 * DN_W
    n_gate = 4 * DN_HEADS
    xp = x_prompt.reshape(-1, D_MODEL)
    xs = x_sample.reshape(-1, D_MODEL)
    new_state = None
    w_in_bf = w_in.astype(BF16)
    w_in_p = jnp.concatenate(
        [w_in_bf[:, :, :qkvz], w_in_bf[:, :, qkvz + n_gate:], w_in_bf[:, :, qkvz:qkvz + n_gate],
         jnp.zeros((depth, D_MODEL, HEAD_W - n_gate), BF16)], axis=2)
    pool_w_bf, w_out_bf, w_gu_bf, w_down_bf = (w.astype(BF16) for w in (pool_w, w_out, w_gu, w_down))
    for l in range(depth):
        lw = dict(norm_mix=norm_mix[l], norm_ffn=norm_ffn[l], w_in=w_in_p, conv_w=conv_w[l],
                  alog_row=_gate_row(a_log[l]), dtb_row=_gate_row(dt_bias[l]), dn_norm=dn_norm[l],
                  pool_w=pool_w_bf, pool_scale=pool_scale[l], w_out=w_out_bf,
                  w_gu=w_gu_bf, w_down=w_down_bf, final_norm=final_norm)
        final = l == depth - 1
        xp, new_state = _stream_layer(xp, ctx_len, None, False, ada[l], lw, None, new_state, l, depth, final)
        xs, _ = _stream_layer(xs, lat_len, lat_len // GRID_W, True, ada[l], lw, state_delta, None, l, depth, final)
    return (xp.reshape(x_prompt.shape), xs.reshape(x_sample.shape), new_state)
```

```python
import functools

import jax
import jax.numpy as jnp
from jax import lax
from jax.experimental import pallas as pl
from jax.experimental.pallas import tpu as pltpu

F32 = jnp.float32
BF16 = jnp.bfloat16

D_MODEL = 1024
DN_HEADS = 4
HEAD_W = 128
DN_W = DN_HEADS * HEAD_W
POOL_GROUPS = 4
POOL_W = POOL_GROUPS * HEAD_W
POOL_WINDOWS = (2, 4, 8, 16)
GRID_W = 64
CONV_K = 5
CHUNK = 64
D_FF = 2816
EPS = 1e-6

PROJ_W = 3 * DN_W + DN_W + POOL_W + HEAD_W
POOL_BLK512 = 4
GATE_BLK = 20
GATE_COLS = 4 * DN_HEADS
ADA_ROWS = 16

V7X_VMEM_LIMIT = 56 * 1024 * 1024

TM_INPROJ = 512
TM_POST = 512
POOL_TB = 256
POOL_BATCH = 8
PREP_UNROLL = 8
DN_HEADS_PER_STEP = 2
YIELD_EVERY = 16
DN_CHUNKS_PER_STEP = 32


def _sigmoid(x):
    return 0.5 * jnp.tanh(0.5 * x) + 0.5


def _silu(x):
    return x * _sigmoid(x)


def _softplus(x):
    return jnp.maximum(x, 0.0) + jnp.log(1.0 + jnp.exp(-jnp.abs(x)))


def _bdot(a, b):
    return jnp.dot(a.astype(BF16), b.astype(BF16), preferred_element_type=F32)


def _iota(shape, dim):
    return lax.broadcasted_iota(jnp.int32, shape, dim)


def _ada_kernel(c_ref, w_ref, b_ref, o_ref):
    o_ref[...] = jnp.dot(_silu(c_ref[...]), w_ref[...], preferred_element_type=F32,
                         precision=lax.Precision.HIGHEST) + b_ref[...]


def _ada_call(c_all, w_ada, b_ada):
    depth, _, n_out = w_ada.shape
    tn = 1536
    return pl.pallas_call(
        _ada_kernel,
        out_shape=jax.ShapeDtypeStruct((depth, ADA_ROWS, n_out), F32),
        grid=(depth, n_out // tn),
        in_specs=[
            pl.BlockSpec((ADA_ROWS, D_MODEL), lambda l, j: (0, 0)),
            pl.BlockSpec((None, D_MODEL, tn), lambda l, j: (l, 0, j)),
            pl.BlockSpec((None, 1, tn), lambda l, j: (l, 0, j)),
        ],
        out_specs=pl.BlockSpec((None, ADA_ROWS, tn), lambda l, j: (l, 0, j)),
        compiler_params=pltpu.CompilerParams(vmem_limit_bytes=V7X_VMEM_LIMIT),
        name="ada",
    )(c_all, w_ada, b_ada.reshape(depth, 1, n_out))


def _modulated_norm(x, gain, shift, scale):
    y = x * lax.rsqrt(jnp.mean(x * x, axis=-1, keepdims=True) + EPS) * gain
    return y * (1.0 + scale) + shift


def _inproj_kernel(x_ref, ada_ref, nw_ref, w_ref, o_ref):
    h = _modulated_norm(x_ref[...], nw_ref[...], ada_ref[0:1, :], ada_ref[1:2, :])
    o_ref[...] = jnp.dot(h.astype(BF16), w_ref[...], preferred_element_type=F32)


def _token_tile(tm, n_tok, seq_len, per_seq):
    tm = min(tm, seq_len) if per_seq else tm
    assert n_tok % tm == 0 and (not per_seq or seq_len % tm == 0)
    return tm


def _ada_row_map(seq_len, tm, per_seq):
    if not per_seq:
        return lambda i: (0, 0, 0)
    return lambda i: (1 + (i * tm) // seq_len, 0, 0)


def _inproj_call(x2, ada_l, norm_w, w_in_p, layer, seq_len, per_seq):
    n_tok = x2.shape[0]
    tm = _token_tile(TM_INPROJ, n_tok, seq_len, per_seq)
    return pl.pallas_call(
        _inproj_kernel,
        out_shape=jax.ShapeDtypeStruct((n_tok, PROJ_W), F32),
        grid=(n_tok // tm,),
        in_specs=[
            pl.BlockSpec((tm, D_MODEL), lambda i: (i, 0)),
            pl.BlockSpec((None, 6, D_MODEL), _ada_row_map(seq_len, tm, per_seq)),
            pl.BlockSpec((1, D_MODEL), lambda i: (0, 0)),
            pl.BlockSpec((None, D_MODEL, PROJ_W), lambda i: (layer, 0, 0), pipeline_mode=pl.Buffered(1)),
        ],
        out_specs=pl.BlockSpec((tm, PROJ_W), lambda i: (i, 0)),
        compiler_params=pltpu.CompilerParams(vmem_limit_bytes=V7X_VMEM_LIMIT),
        name="inproj",
    )(x2, ada_l, norm_w.reshape(1, D_MODEL), w_in_p)


def _col_bcast(x, lane, idx):
    col = jnp.sum(jnp.where(lane == idx, x, 0.0), axis=-1, keepdims=True)
    return jnp.broadcast_to(col, x.shape)


def _each(fn, *lists):
    return [fn(*args) for args in zip(*lists)]


def _each_y(fn, *lists, every=YIELD_EVERY):
    out = []
    for i, args in enumerate(zip(*lists)):
        out.append(fn(*args))
        if i % every == every - 1:
            yield
    return out


def _interleave(*gens):
    gens = list(gens)
    while gens:
        for g in list(gens):
            try:
                next(g)
            except StopIteration:
                gens.remove(g)


def _dn_kernel(*refs, seq_len, ns, hp, has_s0, out_state, has_acc):
    n_ch = seq_len // CHUNK
    gc = min(PREP_UNROLL, ns * n_ch)
    n_grp = ns * n_ch // gc
    assert gc * GATE_COLS <= HEAD_W
    units = [(t, hh) for t in range(gc) for hh in range(hp)]
    nu = len(units)
    it = iter(refs)
    q_ref, k_ref, v_ref, z_ref, g_ref = (next(it) for _ in range(5))
    cwq_ref, cwk_ref, cwv_ref = (next(it) for _ in range(3))
    alog_ref, dtb_ref, dnn_ref = (next(it) for _ in range(3))
    s0_ref = next(it) if has_s0 else None
    if has_acc:
        next(it)
    o_ref = next(it)
    st_ref = next(it) if out_state else None
    wq_s, u_s, kdt_s, dl_s, at_s, vn_s, qs_s, s_s, ext_s, lbd_s, rhs_s = (next(it) for _ in range(11))

    head0 = pl.program_id(1) * hp
    row = _iota((CHUNK, 128), 0)
    lane = _iota((CHUNK, 128), 1)
    is_f = lane < CHUNK
    j = lane & (CHUNK - 1)
    incl = (is_f & (row >= j)) | (~is_f & (row <= j))
    strict = (is_f & (row > j)) | (~is_f & (row < j))
    diag = row == j
    eye_p = jnp.where(diag, 1.0, 0.0).astype(F32)
    couple = []
    for lvl in range(CHUNK.bit_length() - 1):
        rb, cb = row >> lvl, j >> lvl
        couple.append((is_f & ((rb & 1) == 1) & (cb == rb - 1)) | (~is_f & ((rb & 1) == 0) & (cb == rb + 1)))
    r128 = _iota((128, 128), 0)
    c128 = _iota((128, 128), 1)
    lane_f128 = c128 < CHUNK
    cum_sel = ((r128 < CHUNK) & (c128 <= r128)) | ((r128 >= CHUNK) & (c128 >= r128 - CHUNK))
    cum_mat = jnp.where(cum_sel, 1.0, 0.0).astype(F32)[:, :CHUNK]

    neg_a = -jnp.exp(alog_ref[...])
    dtb = dtb_ref[...]
    cols = lambda hh: slice(hh * HEAD_W, (hh + 1) * HEAD_W)

    def group_chunks(g):
        ms = [g * gc + t for t in range(gc)]
        return ms, [m // n_ch for m in ms], [m % n_ch for m in ms]

    def conv_silu(ref, cw_ref, hh, s, c, r0, slot):
        p0 = pl.multiple_of(jnp.maximum(r0 - 8, 0), 8)
        n0 = pl.multiple_of(jnp.minimum(r0 + CHUNK, seq_len - 8), 8)
        ext_s[slot, 0:8, :] = jnp.where(c > 0, ref[s, pl.ds(p0, 8), cols(hh)], 0.0)
        ext_s[slot, 8:8 + CHUNK, :] = ref[s, pl.ds(r0, CHUNK), cols(hh)]
        ext_s[slot, 8 + CHUNK:16 + CHUNK, :] = jnp.where(c < n_ch - 1, ref[s, pl.ds(n0, 8), cols(hh)], 0.0)
        acc = ext_s[slot, 6:6 + CHUNK, :] * cw_ref[0:1, cols(hh)]
        for t in range(1, CONV_K):
            acc = acc + ext_s[slot, 6 + t:6 + t + CHUNK, :] * cw_ref[t:t + 1, cols(hh)]
        return _silu(acc)

    def l2n(x):
        return x * lax.rsqrt(jnp.sum(x * x, axis=-1, keepdims=True) + EPS)

    def gates(ss, r0s):
        packed = g_ref[ss[0], pl.ds(r0s[0], CHUNK), :]
        for t in range(1, gc):
            packed = packed + pltpu.roll(g_ref[ss[t], pl.ds(r0s[t], CHUNK), :], GATE_COLS * t, axis=1)
        is_beta = (lane & (GATE_COLS - 1)) < 2 * DN_HEADS
        gact = jnp.where(is_beta, _sigmoid(packed), neg_a * _softplus(packed + dtb))
        cums = jnp.dot(cum_mat, gact, preferred_element_type=F32, precision=lax.Precision.HIGHEST)
        e_cum = jnp.exp(cums)
        e_rest = jnp.concatenate([jnp.exp(cums[CHUNK - 1:CHUNK] - cums[:CHUNK]),
                                  jnp.exp(cums[CHUNK:CHUNK + 1] - cums[CHUNK:])], axis=0)
        return gact, cums, e_cum, e_rest

    def front(g, slot):
        ms, ss, cs = group_chunks(g)
        r0s = [pl.multiple_of(c * CHUNK, CHUNK) for c in cs]
        gact, cums, e_cum, e_rest = gates(ss, r0s)
        yield
        qk, kk, decay, bdecay = [], [], [], []
        for n, (t, hh) in enumerate(units):
            col = lambda x, which: _col_bcast(x, lane, GATE_COLS * t + which * DN_HEADS + head0 + hh)
            beta_f, beta_b = col(gact, 0), col(gact, 1)
            gc_col = jnp.where(is_f, col(cums[:CHUNK], 2), col(cums[CHUNK:], 3))
            e_f, e_b = col(e_cum[:CHUNK], 2), col(e_cum[CHUNK:], 3)
            q = l2n(conv_silu(q_ref, cwq_ref, hh, ss[t], cs[t], r0s[t], n)) * (HEAD_W ** -0.5)
            k = l2n(conv_silu(k_ref, cwk_ref, hh, ss[t], cs[t], r0s[t], nu + n))
            v = conv_silu(v_ref, cwv_ref, hh, ss[t], cs[t], r0s[t], 2 * nu + n)
            rhs_s[slot, n] = jnp.concatenate([
                jnp.concatenate([v * beta_f, k * beta_f * e_f], axis=1),
                jnp.concatenate([v * beta_b, k * beta_b * e_b], axis=1)], axis=0).astype(BF16)
            wq_s[hh, 0, ms[t], CHUNK:, :] = (q * e_f).astype(BF16)
            wq_s[hh, 1, ms[t], CHUNK:, :] = (q * e_b).astype(BF16)
            kdec = jnp.concatenate([k * col(e_rest[:CHUNK], 2), k * col(e_rest[CHUNK:], 3)], axis=0)
            kdt_s[hh, ms[t]] = kdec.T.astype(BF16)
            dl_s[hh, 0, ms[t]] = jnp.broadcast_to(e_f[CHUNK - 1:CHUNK, :], (8, 128))
            dl_s[hh, 1, ms[t]] = jnp.broadcast_to(e_b[0:1, :], (8, 128))
            kb = k.astype(BF16)
            qk.append(jnp.concatenate([q.astype(BF16), kb], axis=0))
            kk.append(jnp.concatenate([kb, kb], axis=0))
            gc_row = jnp.sum(jnp.where(diag, gc_col, 0.0), axis=0, keepdims=True)
            decay.append(jnp.where(incl, jnp.exp(jnp.where(incl, gc_col - gc_row, 0.0)), 0.0))
            bdecay.append(jnp.where(strict, jnp.where(is_f, beta_f, beta_b) * decay[-1], 0.0))
            yield
        gram = yield from _each_y(
            lambda a, b: lax.dot_general(a, b, (((1,), (1,)), ((), ())), preferred_element_type=F32),
            qk, kk, every=2)
        for n, (t, hh) in enumerate(units):
            lbd_s[slot, n] = (bdecay[n] * gram[n][CHUNK:]).astype(BF16)
            at_s[hh, ms[t]] = (gram[n][:CHUNK] * decay[n]).astype(BF16)
            if n % 2 == 1:
                yield

    def back(g, slot):
        ms = [g * gc + t for t in range(gc)]
        l2 = [lbd_s[slot, n] for n in range(nu)]
        zero = jnp.zeros((CHUNK, 128), BF16)

        def bd(x):
            x = x.astype(BF16)
            return jnp.concatenate([jnp.where(is_f, x, zero), jnp.where(is_f, zero, x)], axis=0)

        pdot = lambda a, b_bd: jnp.dot(a.astype(BF16), b_bd, preferred_element_type=F32)
        inv = _each(lambda x: eye_p - jnp.where(couple[0], x, zero).astype(F32), l2)
        for lvl in range(1, len(couple)):
            c_bd = _each(lambda x: bd(jnp.where(couple[lvl], x, zero)), l2)
            xc = yield from _each_y(lambda a, b: pdot(a, b).astype(BF16), inv, c_bd)
            inv = yield from _each_y(lambda x, y: x - pdot(y, bd(x)), inv, xc)
        uw = yield from _each_y(lambda tt, n: jnp.dot(bd(tt), rhs_s[slot, n], preferred_element_type=F32),
                                inv, range(nu), every=2)
        for n, (t, hh) in enumerate(units):
            u_s[hh, 0, ms[t]] = uw[n][:CHUNK, :HEAD_W]
            u_s[hh, 1, ms[t]] = uw[n][CHUNK:, :HEAD_W]
            wq_s[hh, 0, ms[t], :CHUNK, :] = uw[n][:CHUNK, HEAD_W:].astype(BF16)
            wq_s[hh, 1, ms[t], :CHUNK, :] = uw[n][CHUNK:, HEAD_W:].astype(BF16)

    chains = [(s, hh) for s in range(ns) for hh in range(hp)]

    dnn = dnn_ref[...]

    def emit_out(c):
        r0 = c * CHUNK if isinstance(c, int) else pl.multiple_of(c * CHUNK, CHUNK)
        for s, hh in chains:
            m = s * n_ch + c
            vst = jnp.concatenate([vn_s[hh, 0, m], vn_s[hh, 1, m]], axis=0)
            o = qs_s[hh, m] + jnp.dot(at_s[hh, m], vst, preferred_element_type=F32)
            y = o * lax.rsqrt(jnp.mean(o * o, axis=-1, keepdims=True) + EPS) * dnn
            o_ref[s, pl.ds(r0, CHUNK), cols(hh)] = (
                y * _silu(z_ref[s, pl.ds(r0, CHUNK), cols(hh)])).astype(o_ref.dtype)

    def scan_body(first, with_out, i, carry):
        if with_out:
            emit_out(i - 1)
            emit_out(n_ch - i)
        todo = [(s, hh, d, s * n_ch + (i if d == 0 else n_ch - 1 - i)) for s, hh in chains for d in range(2)]
        st = [s_s[s * hp + hh, d] for s, hh, d, m in todo]
        r = [jnp.dot(wq_s[hh, d, m], x.astype(BF16), preferred_element_type=F32)
             for (s, hh, d, m), x in zip(todo, st)]
        vn = []
        for (s, hh, d, m), rr in zip(todo, r):
            vn.append((u_s[hh, d, m] - rr[:CHUNK]).astype(BF16))
            vn_s[hh, d, m] = vn[-1]
        upd = []
        for (s, hh, d, m), x in zip(todo, vn):
            kdt = kdt_s[hh, m]
            zero = jnp.zeros_like(kdt)
            kdt = jnp.where(lane_f128, kdt, zero) if d == 0 else jnp.where(lane_f128, zero, kdt)
            upd.append(jnp.dot(kdt, jnp.concatenate([x, x], axis=0), preferred_element_type=F32))
        for (s, hh, d, m), x, rr, up in zip(todo, st, r, upd):
            s_s[s * hp + hh, d] = x * dl_s[hh, d, m][0:1, :] + up
            qs_s[hh, m] = rr[CHUNK:] if first else qs_s[hh, m] + rr[CHUNK:]
        return carry

    half = n_ch // 2
    early = ns == 1 and n_grp >= 3 and gc <= half
    order = (lambda i: (i + n_grp - 1) % n_grp) if early else (lambda i: i)
    _interleave(front(order(0), 0))

    def prep_body(i, carry):
        slot = i & 1
        _interleave(back(order(i), slot), front(order(i + 1), 1 - slot))
        return carry

    lax.fori_loop(0, n_grp - 1, prep_body, 0)
    for s, hh in chains:
        for d in range(2):
            s_s[s * hp + hh, d] = s0_ref[s, d, hh] if has_s0 else jnp.zeros((HEAD_W, HEAD_W), F32)

    def early_scan():
        for i in range(gc):
            scan_body(True, False, i, 0)
            yield

    last = back(order(n_grp - 1), (n_grp - 1) & 1)
    _interleave(last, early_scan()) if early else _interleave(last)
    n_early = gc if early else 0

    lax.fori_loop(n_early, half, functools.partial(scan_body, True, False), 0)
    scan_body(False, False, half, 0)
    lax.fori_loop(half + 1, n_ch, functools.partial(scan_body, False, True), 0)
    emit_out(n_ch - 1)
    emit_out(0)
    if out_state:
        for s, hh in chains:
            for d in range(2):
                st_ref[s, d, hh] = s_s[s * hp + hh, d]


def _dn_call(proj3, conv_w, alog_row, dtb_row, dn_norm, state_delta, layer, depth, out_state, states_so_far):
    n_seq, seq_len, _ = proj3.shape
    n_ch = seq_len // CHUNK
    hp = DN_HEADS_PER_STEP
    ns = max(1, min(n_seq, DN_CHUNKS_PER_STEP // n_ch))
    assert n_seq % ns == 0
    tot = ns * n_ch
    nu = min(PREP_UNROLL, tot) * hp
    wid = hp * HEAD_W
    nblk = DN_HEADS // hp
    has_s0 = state_delta is not None
    tile = lambda off: pl.BlockSpec((ns, seq_len, wid), lambda b, h: (b, 0, off + h))
    cw = lambda off: pl.BlockSpec((CONV_K, wid), lambda b, h: (0, off + h))
    row = pl.BlockSpec((1, HEAD_W), lambda b, h: (0, 0))
    in_specs = [tile(0), tile(nblk), tile(2 * nblk), tile(3 * nblk),
                pl.BlockSpec((ns, seq_len, HEAD_W), lambda b, h: (b, 0, GATE_BLK)),
                cw(0), cw(nblk), cw(2 * nblk), row, row, row]
    args = [proj3, proj3, proj3, proj3, proj3, conv_w, conv_w, conv_w, alog_row, dtb_row,
            dn_norm.reshape(1, HEAD_W)]
    if has_s0:
        in_specs.append(pl.BlockSpec((ns, None, 2, hp, HEAD_W, HEAD_W),
                                     lambda b, h: (b, layer, 0, h, 0, 0)))
        args.append(state_delta)
    out_shape = [jax.ShapeDtypeStruct((n_seq, seq_len, DN_W), BF16)]
    out_specs = [pl.BlockSpec((ns, seq_len, wid), lambda b, h: (b, 0, h))]
    aliases = {}
    if out_state:
        out_shape.append(jax.ShapeDtypeStruct((n_seq, depth, 2, DN_HEADS, HEAD_W, HEAD_W), F32))
        out_specs.append(pl.BlockSpec((ns, None, 2, hp, HEAD_W, HEAD_W), lambda b, h: (b, layer, 0, h, 0, 0)))
        if states_so_far is not None:
            in_specs.append(pl.BlockSpec(memory_space=pl.ANY))
            args.append(states_so_far)
            aliases = {len(args) - 1: 1}
    scratch = [
        pltpu.VMEM((hp, 2, tot, 2 * CHUNK, HEAD_W), BF16),
        pltpu.VMEM((hp, 2, tot, CHUNK, HEAD_W), F32),
        pltpu.VMEM((hp, tot, HEAD_W, 2 * CHUNK), BF16),
        pltpu.VMEM((hp, 2, tot, 8, HEAD_W), F32),
        pltpu.VMEM((hp, tot, CHUNK, 2 * CHUNK), BF16),
        pltpu.VMEM((hp, 2, tot, CHUNK, HEAD_W), BF16),
        pltpu.VMEM((hp, tot, CHUNK, HEAD_W), F32),
        pltpu.VMEM((ns * hp, 2, HEAD_W, HEAD_W), F32),
        pltpu.VMEM((3 * nu, CHUNK + 16, HEAD_W), F32),
        pltpu.VMEM((2, nu, CHUNK, 2 * CHUNK), BF16),
        pltpu.VMEM((2, nu, 2 * CHUNK, 2 * HEAD_W), BF16),
    ]
    outs = pl.pallas_call(
        functools.partial(_dn_kernel, seq_len=seq_len, ns=ns, hp=hp, has_s0=has_s0, out_state=out_state,
                          has_acc=bool(aliases)),
        out_shape=out_shape,
        grid=(n_seq // ns, nblk),
        in_specs=in_specs,
        out_specs=out_specs,
        scratch_shapes=scratch,
        input_output_aliases=aliases,
        compiler_params=pltpu.CompilerParams(vmem_limit_bytes=V7X_VMEM_LIMIT),
        name="deltanet",
    )(*args)
    return (outs[0], outs[1]) if out_state else (outs[0], None)


def _split2(x):
    hi = x.astype(BF16)
    lo = (x - hi.astype(F32)).astype(BF16)
    return hi, lo


def _window(pos, win, n):
    lo = jnp.clip(pos - win // 2, 0, n)
    hi = jnp.clip(pos - win // 2 + win, 0, n)
    return lo, hi


def _pool_kernel(u_ref, pw_ref, ps_ref, o_ref, m_s, *, seq_len, rows):
    period = GRID_W if rows else seq_len
    tb = min(POOL_TB, seq_len)
    r_i = _iota((tb, tb), 0)
    c_i = _iota((tb, tb), 1)
    shift = period.bit_length() - 1
    same_line = (r_i >> shift) == (c_i >> shift)
    pos_r = r_i & (period - 1)
    pos_c = c_i & (period - 1)
    pos_col = _iota((tb, HEAD_W), 0) & (period - 1)
    band, inv_cnt = {}, {}
    for g, win in enumerate(POOL_WINDOWS):
        cols = slice(g * HEAD_W, (g + 1) * HEAD_W)
        if rows:
            run = None
            prev_lo = prev_hi = 0
            for r in range(rows):
                lo, hi = max(r - win // 2, 0), min(r - win // 2 + win, rows)
                for a in range(prev_hi, hi):
                    slab = u_ref[a * GRID_W:(a + 1) * GRID_W, cols]
                    run = slab if run is None else run + slab
                for a in range(prev_lo, lo):
                    run = run - u_ref[a * GRID_W:(a + 1) * GRID_W, cols]
                prev_lo, prev_hi = lo, hi
                m_s[g, r * GRID_W:(r + 1) * GRID_W, :] = run / float(hi - lo)
        lo_r, hi_r = _window(pos_r, win, period)
        band[g] = jnp.where(same_line & (pos_c >= lo_r) & (pos_c < hi_r), 1.0, 0.0).astype(BF16)
        lo_c, hi_c = _window(pos_col, win, period)
        inv_cnt[g] = 1.0 / (hi_c - lo_c).astype(F32)

    items = [(g, t0) for g in range(POOL_GROUPS) for t0 in range(0, seq_len, tb)]
    for b0 in range(0, len(items), POOL_BATCH):
        batch = items[b0:b0 + POOL_BATCH]
        gcols = lambda g: slice(g * HEAD_W, (g + 1) * HEAD_W)
        ug = [u_ref[t0:t0 + tb, gcols(g)] for g, t0 in batch]
        src = [m_s[g, t0:t0 + tb, :] for g, t0 in batch] if rows else ug
        parts = _each(_split2, src)
        box = [jnp.dot(band[g], hi, preferred_element_type=F32) + jnp.dot(band[g], lo, preferred_element_type=F32)
               for (g, _), (hi, lo) in zip(batch, parts)]
        dev = [(bx * inv_cnt[g] - u).astype(BF16) for (g, _), bx, u in zip(batch, box, ug)]
        mix = [jnp.dot(d, pw_ref[g], preferred_element_type=F32) for (g, _), d in zip(batch, dev)]
        for (g, t0), y in zip(batch, mix):
            o_ref[t0:t0 + tb, gcols(g)] = (y * ps_ref[:, gcols(g)]).astype(o_ref.dtype)


def _pool_call(proj3, pool_w_bf, pool_scale, layer, rows):
    n_seq, seq_len, _ = proj3.shape
    return pl.pallas_call(
        functools.partial(_pool_kernel, seq_len=seq_len, rows=rows),
        out_shape=jax.ShapeDtypeStruct((n_seq, seq_len, POOL_W), BF16),
        grid=(n_seq,),
        in_specs=[
            pl.BlockSpec((None, seq_len, POOL_W), lambda b: (b, 0, POOL_BLK512)),
            pl.BlockSpec((None, POOL_GROUPS, HEAD_W, HEAD_W), lambda b: (layer, 0, 0, 0)),
            pl.BlockSpec((1, POOL_W), lambda b: (0, 0)),
        ],
        out_specs=pl.BlockSpec((None, seq_len, POOL_W), lambda b: (b, 0, 0)),
        scratch_shapes=[pltpu.VMEM((POOL_GROUPS, seq_len, HEAD_W), F32)],
        compiler_params=pltpu.CompilerParams(vmem_limit_bytes=V7X_VMEM_LIMIT),
        name="pool",
    )(proj3, pool_w_bf, pool_scale.reshape(1, POOL_W))


def _post_kernel(x_ref, o_ref, p_ref, ada_ref, nf_ref, fin_ref, wo_ref, wgu_ref, wd_ref, y_ref, *, final):
    mix = (jnp.dot(o_ref[...], wo_ref[:DN_W, :], preferred_element_type=F32)
           + jnp.dot(p_ref[...], wo_ref[DN_W:, :], preferred_element_type=F32))
    x = x_ref[...] + ada_ref[2:3, :] * mix
    h = _modulated_norm(x, nf_ref[...], ada_ref[3:4, :], ada_ref[4:5, :])
    gu = jnp.dot(h.astype(BF16), wgu_ref[...], preferred_element_type=F32)
    act = _silu(gu[:, :D_FF]) * gu[:, D_FF:]
    x = x + ada_ref[5:6, :] * jnp.dot(act.astype(BF16), wd_ref[...], preferred_element_type=F32)
    if final:
        x = x * lax.rsqrt(jnp.mean(x * x, axis=-1, keepdims=True) + EPS) * fin_ref[...]
    y_ref[...] = x


def _post_call(x2, o2, p2, ada_l, norm_ffn, final_norm, w_out_bf, w_gu_bf, w_down_bf, layer, seq_len, per_seq,
               final):
    n_tok = x2.shape[0]
    tm = _token_tile(TM_POST, n_tok, seq_len, per_seq)
    const = lambda shape: pl.BlockSpec((None,) + shape, lambda i: (layer, 0, 0), pipeline_mode=pl.Buffered(1))
    return pl.pallas_call(
        functools.partial(_post_kernel, final=final),
        out_shape=jax.ShapeDtypeStruct((n_tok, D_MODEL), F32),
        grid=(n_tok // tm,),
        in_specs=[
            pl.BlockSpec((tm, D_MODEL), lambda i: (i, 0)),
            pl.BlockSpec((tm, DN_W), lambda i: (i, 0)),
            pl.BlockSpec((tm, POOL_W), lambda i: (i, 0)),
            pl.BlockSpec((None, 6, D_MODEL), _ada_row_map(seq_len, tm, per_seq)),
            pl.BlockSpec((1, D_MODEL), lambda i: (0, 0)),
            pl.BlockSpec((1, D_MODEL), lambda i: (0, 0)),
            const((D_MODEL, D_MODEL)),
            const((D_MODEL, 2 * D_FF)),
            const((D_FF, D_MODEL)),
        ],
        out_specs=pl.BlockSpec((tm, D_MODEL), lambda i: (i, 0)),
        compiler_params=pltpu.CompilerParams(vmem_limit_bytes=V7X_VMEM_LIMIT),
        name="post",
    )(x2, o2, p2, ada_l, norm_ffn.reshape(1, D_MODEL), final_norm.reshape(1, D_MODEL),
      w_out_bf, w_gu_bf, w_down_bf)


def _gate_row(p):
    one = jnp.concatenate([jnp.zeros((2 * DN_HEADS,), F32), p.reshape(-1)])
    return jnp.tile(one, HEAD_W // GATE_COLS).reshape(1, HEAD_W)


def _stream_layer(x2, seq_len, rows, per_seq, ada_l, lw, state_delta, states_so_far, layer, depth, final):
    n_seq = x2.shape[0] // seq_len
    proj = _inproj_call(x2, ada_l, lw["norm_mix"], lw["w_in"], layer, seq_len, per_seq)
    proj3 = proj.reshape(n_seq, seq_len, PROJ_W)
    o, st = _dn_call(proj3, lw["conv_w"], lw["alog_row"], lw["dtb_row"], lw["dn_norm"],
                     state_delta, layer, depth, state_delta is None, states_so_far)
    p = _pool_call(proj3, lw["pool_w"], lw["pool_scale"], layer, rows)
    x2 = _post_call(x2, o.reshape(-1, DN_W), p.reshape(-1, POOL_W), ada_l, lw["norm_ffn"], lw["final_norm"],
                    lw["w_out"], lw["w_gu"], lw["w_down"], layer, seq_len, per_seq, final)
    return x2, st


def kernel(x_prompt, x_sample, c, state_delta, c_ctx, w_ada, b_ada, norm_mix, norm_ffn, w_in, conv_w, a_log, dt_bias, dn_norm, pool_w, pool_scale, w_out, w_gu, w_down, final_norm):
    depth = w_ada.shape[0]
    _, ctx_len, _ = x_prompt.shape
    n_lat, lat_len, _ = x_sample.shape
    assert 1 + n_lat <= ADA_ROWS
    c_all = jnp.concatenate([c_ctx[None, :], c, jnp.zeros((ADA_ROWS - 1 - n_lat, D_MODEL), F32)], axis=0)
    ada = _ada_call(c_all, w_ada, b_ada).reshape(depth, ADA_ROWS, 6, D_MODEL)

    qkvz = 4 * DN_W
    n_gate = 4 * DN_HEADS
    xp = x_prompt.reshape(-1, D_MODEL)
    xs = x_sample.reshape(-1, D_MODEL)
    new_state = None
    w_in_bf = w_in.astype(BF16)
    w_in_p = jnp.concatenate(
        [w_in_bf[:, :, :qkvz], w_in_bf[:, :, qkvz + n_gate:], w_in_bf[:, :, qkvz:qkvz + n_gate],
         jnp.zeros((depth, D_MODEL, HEAD_W - n_gate), BF16)], axis=2)
    pool_w_bf, w_out_bf, w_gu_bf, w_down_bf = (w.astype(BF16) for w in (pool_w, w_out, w_gu, w_down))
    for l in range(depth):
        lw = dict(norm_mix=norm_mix[l], norm_ffn=norm_ffn[l], w_in=w_in_p, conv_w=conv_w[l],
                  alog_row=_gate_row(a_log[l]), dtb_row=_gate_row(dt_bias[l]), dn_norm=dn_norm[l],
                  pool_w=pool_w_bf, pool_scale=pool_scale[l], w_out=w_out_bf,
                  w_gu=w_gu_bf, w_down=w_down_bf, final_norm=final_norm)
        final = l == depth - 1
        xp, new_state = _stream_layer(xp, ctx_len, None, False, ada[l], lw, None, new_state, l, depth, final)
        xs, _ = _stream_layer(xs, lat_len, lat_len // GRID_W, True, ada[l], lw, state_delta, None, l, depth, final)
    return (xp.reshape(x_prompt.shape), xs.reshape(x_sample.shape), new_state)
```

```python
import functools

import jax
import jax.numpy as jnp
from jax import lax
from jax.experimental import pallas as pl
from jax.experimental.pallas import tpu as pltpu

F32 = jnp.float32
BF16 = jnp.bfloat16

D_MODEL = 1024
DN_HEADS = 4
HEAD_W = 128
DN_W = DN_HEADS * HEAD_W
POOL_GROUPS = 4
POOL_W = POOL_GROUPS * HEAD_W
POOL_WINDOWS = (2, 4, 8, 16)
GRID_W = 64
CONV_K = 5
CHUNK = 64
D_FF = 2816
EPS = 1e-6

PROJ_W = 3 * DN_W + DN_W + POOL_W + HEAD_W
POOL_BLK512 = 4
GATE_BLK = 20
GATE_COLS = 4 * DN_HEADS
ADA_ROWS = 16

V7X_VMEM_LIMIT = 56 * 1024 * 1024

TM_INPROJ = 512
TM_POST = 512
POOL_TB = 256
POOL_BATCH = 8
POOL_TOKENS_PER_STEP = 2048
PREP_UNROLL = 8
DN_HEADS_PER_STEP = 2
YIELD_EVERY = 16
DN_CHUNKS_PER_STEP = 32


def _sigmoid(x):
    return 0.5 * jnp.tanh(0.5 * x) + 0.5


def _silu(x):
    return x * _sigmoid(x)


def _softplus(x):
    return jnp.maximum(x, 0.0) + jnp.log(1.0 + jnp.exp(-jnp.abs(x)))


def _bdot(a, b):
    return jnp.dot(a.astype(BF16), b.astype(BF16), preferred_element_type=F32)


def _iota(shape, dim):
    return lax.broadcasted_iota(jnp.int32, shape, dim)


def _ada_kernel(c_ref, w_ref, b_ref, o_ref):
    o_ref[...] = jnp.dot(_silu(c_ref[...]), w_ref[...], preferred_element_type=F32,
                         precision=lax.Precision.HIGHEST) + b_ref[...]


def _ada_call(c_all, w_ada, b_ada):
    depth, _, n_out = w_ada.shape
    tn = 1536
    return pl.pallas_call(
        _ada_kernel,
        out_shape=jax.ShapeDtypeStruct((depth, ADA_ROWS, n_out), F32),
        grid=(depth, n_out // tn),
        in_specs=[
            pl.BlockSpec((ADA_ROWS, D_MODEL), lambda l, j: (0, 0)),
            pl.BlockSpec((None, D_MODEL, tn), lambda l, j: (l, 0, j)),
            pl.BlockSpec((None, 1, tn), lambda l, j: (l, 0, j)),
        ],
        out_specs=pl.BlockSpec((None, ADA_ROWS, tn), lambda l, j: (l, 0, j)),
        compiler_params=pltpu.CompilerParams(vmem_limit_bytes=V7X_VMEM_LIMIT),
        name="ada",
    )(c_all, w_ada, b_ada.reshape(depth, 1, n_out))


def _modulated_norm(x, gain, shift, scale):
    y = x * lax.rsqrt(jnp.mean(x * x, axis=-1, keepdims=True) + EPS) * gain
    return y * (1.0 + scale) + shift


def _inproj_kernel(x_ref, ada_ref, nw_ref, w_ref, o_ref):
    h = _modulated_norm(x_ref[...], nw_ref[...], ada_ref[0:1, :], ada_ref[1:2, :])
    o_ref[...] = jnp.dot(h.astype(BF16), w_ref[...], preferred_element_type=F32)


def _token_tile(tm, n_tok, seq_len, per_seq):
    tm = min(tm, seq_len) if per_seq else tm
    assert n_tok % tm == 0 and (not per_seq or seq_len % tm == 0)
    return tm


def _ada_row_map(seq_len, tm, per_seq):
    if not per_seq:
        return lambda i: (0, 0, 0)
    return lambda i: (1 + (i * tm) // seq_len, 0, 0)


def _inproj_call(x2, ada_l, norm_w, w_in_p, layer, seq_len, per_seq):
    n_tok = x2.shape[0]
    tm = _token_tile(TM_INPROJ, n_tok, seq_len, per_seq)
    return pl.pallas_call(
        _inproj_kernel,
        out_shape=jax.ShapeDtypeStruct((n_tok, PROJ_W), F32),
        grid=(n_tok // tm,),
        in_specs=[
            pl.BlockSpec((tm, D_MODEL), lambda i: (i, 0)),
            pl.BlockSpec((None, 6, D_MODEL), _ada_row_map(seq_len, tm, per_seq)),
            pl.BlockSpec((1, D_MODEL), lambda i: (0, 0)),
            pl.BlockSpec((None, D_MODEL, PROJ_W), lambda i: (layer, 0, 0), pipeline_mode=pl.Buffered(1)),
        ],
        out_specs=pl.BlockSpec((tm, PROJ_W), lambda i: (i, 0)),
        compiler_params=pltpu.CompilerParams(vmem_limit_bytes=V7X_VMEM_LIMIT),
        name="inproj",
    )(x2, ada_l, norm_w.reshape(1, D_MODEL), w_in_p)


def _col_bcast(x, lane, idx):
    col = jnp.sum(jnp.where(lane == idx, x, 0.0), axis=-1, keepdims=True)
    return jnp.broadcast_to(col, x.shape)


def _each(fn, *lists):
    return [fn(*args) for args in zip(*lists)]


def _each_y(fn, *lists, every=YIELD_EVERY):
    out = []
    for i, args in enumerate(zip(*lists)):
        out.append(fn(*args))
        if i % every == every - 1:
            yield
    return out


def _interleave(*gens):
    gens = list(gens)
    while gens:
        for g in list(gens):
            try:
                next(g)
            except StopIteration:
                gens.remove(g)


def _dn_kernel(*refs, seq_len, ns, hp, has_s0, out_state, has_acc):
    n_ch = seq_len // CHUNK
    gc = min(PREP_UNROLL, ns * n_ch)
    n_grp = ns * n_ch // gc
    assert gc * GATE_COLS <= HEAD_W
    units = [(t, hh) for t in range(gc) for hh in range(hp)]
    nu = len(units)
    it = iter(refs)
    q_ref, k_ref, v_ref, z_ref, g_ref = (next(it) for _ in range(5))
    cwq_ref, cwk_ref, cwv_ref = (next(it) for _ in range(3))
    alog_ref, dtb_ref, dnn_ref = (next(it) for _ in range(3))
    s0_ref = next(it) if has_s0 else None
    if has_acc:
        next(it)
    o_ref = next(it)
    st_ref = next(it) if out_state else None
    wq_s, u_s, kdt_s, dl_s, at_s, vn_s, qs_s, s_s, ext_s, lbd_s, rhs_s = (next(it) for _ in range(11))

    head0 = pl.program_id(1) * hp
    row = _iota((CHUNK, 128), 0)
    lane = _iota((CHUNK, 128), 1)
    is_f = lane < CHUNK
    j = lane & (CHUNK - 1)
    incl = (is_f & (row >= j)) | (~is_f & (row <= j))
    strict = (is_f & (row > j)) | (~is_f & (row < j))
    diag = row == j
    eye_p = jnp.where(diag, 1.0, 0.0).astype(F32)
    couple = []
    for lvl in range(CHUNK.bit_length() - 1):
        rb, cb = row >> lvl, j >> lvl
        couple.append((is_f & ((rb & 1) == 1) & (cb == rb - 1)) | (~is_f & ((rb & 1) == 0) & (cb == rb + 1)))
    r128 = _iota((128, 128), 0)
    c128 = _iota((128, 128), 1)
    lane_f128 = c128 < CHUNK
    cum_sel = ((r128 < CHUNK) & (c128 <= r128)) | ((r128 >= CHUNK) & (c128 >= r128 - CHUNK))
    cum_mat = jnp.where(cum_sel, 1.0, 0.0).astype(F32)[:, :CHUNK]

    neg_a = -jnp.exp(alog_ref[...])
    dtb = dtb_ref[...]
    cols = lambda hh: slice(hh * HEAD_W, (hh + 1) * HEAD_W)

    def group_chunks(g):
        ms = [g * gc + t for t in range(gc)]
        return ms, [m // n_ch for m in ms], [m % n_ch for m in ms]

    def conv_silu(ref, cw_ref, hh, s, c, r0, slot):
        p0 = pl.multiple_of(jnp.maximum(r0 - 8, 0), 8)
        n0 = pl.multiple_of(jnp.minimum(r0 + CHUNK, seq_len - 8), 8)
        ext_s[slot, 0:8, :] = jnp.where(c > 0, ref[s, pl.ds(p0, 8), cols(hh)], 0.0)
        ext_s[slot, 8:8 + CHUNK, :] = ref[s, pl.ds(r0, CHUNK), cols(hh)]
        ext_s[slot, 8 + CHUNK:16 + CHUNK, :] = jnp.where(c < n_ch - 1, ref[s, pl.ds(n0, 8), cols(hh)], 0.0)
        acc = ext_s[slot, 6:6 + CHUNK, :] * cw_ref[0:1, cols(hh)]
        for t in range(1, CONV_K):
            acc = acc + ext_s[slot, 6 + t:6 + t + CHUNK, :] * cw_ref[t:t + 1, cols(hh)]
        return _silu(acc)

    def l2n(x):
        return x * lax.rsqrt(jnp.sum(x * x, axis=-1, keepdims=True) + EPS)

    def gates(ss, r0s):
        packed = g_ref[ss[0], pl.ds(r0s[0], CHUNK), :]
        for t in range(1, gc):
            packed = packed + pltpu.roll(g_ref[ss[t], pl.ds(r0s[t], CHUNK), :], GATE_COLS * t, axis=1)
        is_beta = (lane & (GATE_COLS - 1)) < 2 * DN_HEADS
        gact = jnp.where(is_beta, _sigmoid(packed), neg_a * _softplus(packed + dtb))
        cums = jnp.dot(cum_mat, gact, preferred_element_type=F32, precision=lax.Precision.HIGHEST)
        e_cum = jnp.exp(cums)
        e_rest = jnp.concatenate([jnp.exp(cums[CHUNK - 1:CHUNK] - cums[:CHUNK]),
                                  jnp.exp(cums[CHUNK:CHUNK + 1] - cums[CHUNK:])], axis=0)
        return gact, cums, e_cum, e_rest

    def front(g, slot):
        ms, ss, cs = group_chunks(g)
        r0s = [pl.multiple_of(c * CHUNK, CHUNK) for c in cs]
        gact, cums, e_cum, e_rest = gates(ss, r0s)
        yield
        qk, kk, decay, bdecay = [], [], [], []
        for n, (t, hh) in enumerate(units):
            col = lambda x, which: _col_bcast(x, lane, GATE_COLS * t + which * DN_HEADS + head0 + hh)
            beta_f, beta_b = col(gact, 0), col(gact, 1)
            gc_col = jnp.where(is_f, col(cums[:CHUNK], 2), col(cums[CHUNK:], 3))
            e_f, e_b = col(e_cum[:CHUNK], 2), col(e_cum[CHUNK:], 3)
            q = l2n(conv_silu(q_ref, cwq_ref, hh, ss[t], cs[t], r0s[t], n)) * (HEAD_W ** -0.5)
            k = l2n(conv_silu(k_ref, cwk_ref, hh, ss[t], cs[t], r0s[t], nu + n))
            v = conv_silu(v_ref, cwv_ref, hh, ss[t], cs[t], r0s[t], 2 * nu + n)
            rhs_s[slot, n] = jnp.concatenate([
                jnp.concatenate([v * beta_f, k * beta_f * e_f], axis=1),
                jnp.concatenate([v * beta_b, k * beta_b * e_b], axis=1)], axis=0).astype(BF16)
            wq_s[hh, 0, ms[t], CHUNK:, :] = (q * e_f).astype(BF16)
            wq_s[hh, 1, ms[t], CHUNK:, :] = (q * e_b).astype(BF16)
            kdec = jnp.concatenate([k * col(e_rest[:CHUNK], 2), k * col(e_rest[CHUNK:], 3)], axis=0)
            kdt_s[hh, ms[t]] = kdec.T.astype(BF16)
            dl_s[hh, 0, ms[t]] = jnp.broadcast_to(e_f[CHUNK - 1:CHUNK, :], (8, 128))
            dl_s[hh, 1, ms[t]] = jnp.broadcast_to(e_b[0:1, :], (8, 128))
            kb = k.astype(BF16)
            qk.append(jnp.concatenate([q.astype(BF16), kb], axis=0))
            kk.append(jnp.concatenate([kb, kb], axis=0))
            gc_row = jnp.sum(jnp.where(diag, gc_col, 0.0), axis=0, keepdims=True)
            decay.append(jnp.where(incl, jnp.exp(jnp.where(incl, gc_col - gc_row, 0.0)), 0.0))
            bdecay.append(jnp.where(strict, jnp.where(is_f, beta_f, beta_b) * decay[-1], 0.0))
            yield
        gram = yield from _each_y(
            lambda a, b: lax.dot_general(a, b, (((1,), (1,)), ((), ())), preferred_element_type=F32),
            qk, kk, every=2)
        for n, (t, hh) in enumerate(units):
            lbd_s[slot, n] = (bdecay[n] * gram[n][CHUNK:]).astype(BF16)
            at_s[hh, ms[t]] = (gram[n][:CHUNK] * decay[n]).astype(BF16)
            if n % 2 == 1:
                yield

    def back(g, slot):
        ms = [g * gc + t for t in range(gc)]
        l2 = [lbd_s[slot, n] for n in range(nu)]
        zero = jnp.zeros((CHUNK, 128), BF16)

        def bd(x):
            x = x.astype(BF16)
            return jnp.concatenate([jnp.where(is_f, x, zero), jnp.where(is_f, zero, x)], axis=0)

        pdot = lambda a, b_bd: jnp.dot(a.astype(BF16), b_bd, preferred_element_type=F32)
        inv = _each(lambda x: eye_p - jnp.where(couple[0], x, zero).astype(F32), l2)
        for lvl in range(1, len(couple)):
            c_bd = _each(lambda x: bd(jnp.where(couple[lvl], x, zero)), l2)
            xc = yield from _each_y(lambda a, b: pdot(a, b).astype(BF16), inv, c_bd)
            inv = yield from _each_y(lambda x, y: x - pdot(y, bd(x)), inv, xc)
        uw = yield from _each_y(lambda tt, n: jnp.dot(bd(tt), rhs_s[slot, n], preferred_element_type=F32),
                                inv, range(nu), every=2)
        for n, (t, hh) in enumerate(units):
            u_s[hh, 0, ms[t]] = uw[n][:CHUNK, :HEAD_W]
            u_s[hh, 1, ms[t]] = uw[n][CHUNK:, :HEAD_W]
            wq_s[hh, 0, ms[t], :CHUNK, :] = uw[n][:CHUNK, HEAD_W:].astype(BF16)
            wq_s[hh, 1, ms[t], :CHUNK, :] = uw[n][CHUNK:, HEAD_W:].astype(BF16)

    chains = [(s, hh) for s in range(ns) for hh in range(hp)]

    dnn = dnn_ref[...]

    def emit_out(c):
        r0 = c * CHUNK if isinstance(c, int) else pl.multiple_of(c * CHUNK, CHUNK)
        for s, hh in chains:
            m = s * n_ch + c
            vst = jnp.concatenate([vn_s[hh, 0, m], vn_s[hh, 1, m]], axis=0)
            o = qs_s[hh, m] + jnp.dot(at_s[hh, m], vst, preferred_element_type=F32)
            y = o * lax.rsqrt(jnp.mean(o * o, axis=-1, keepdims=True) + EPS) * dnn
            o_ref[s, pl.ds(r0, CHUNK), cols(hh)] = (
                y * _silu(z_ref[s, pl.ds(r0, CHUNK), cols(hh)])).astype(o_ref.dtype)

    def scan_body(first, with_out, i, carry):
        if with_out:
            emit_out(i - 1)
            emit_out(n_ch - i)
        todo = [(s, hh, d, s * n_ch + (i if d == 0 else n_ch - 1 - i)) for s, hh in chains for d in range(2)]
        st = [s_s[s * hp + hh, d] for s, hh, d, m in todo]
        r = [jnp.dot(wq_s[hh, d, m], x.astype(BF16), preferred_element_type=F32)
             for (s, hh, d, m), x in zip(todo, st)]
        vn = []
        for (s, hh, d, m), rr in zip(todo, r):
            vn.append((u_s[hh, d, m] - rr[:CHUNK]).astype(BF16))
            vn_s[hh, d, m] = vn[-1]
        upd = []
        for (s, hh, d, m), x in zip(todo, vn):
            kdt = kdt_s[hh, m]
            zero = jnp.zeros_like(kdt)
            kdt = jnp.where(lane_f128, kdt, zero) if d == 0 else jnp.where(lane_f128, zero, kdt)
            upd.append(jnp.dot(kdt, jnp.concatenate([x, x], axis=0), preferred_element_type=F32))
        for (s, hh, d, m), x, rr, up in zip(todo, st, r, upd):
            s_s[s * hp + hh, d] = x * dl_s[hh, d, m][0:1, :] + up
            qs_s[hh, m] = rr[CHUNK:] if first else qs_s[hh, m] + rr[CHUNK:]
        return carry

    half = n_ch // 2
    early = ns == 1 and n_grp >= 3 and gc <= half
    order = (lambda i: (i + n_grp - 1) % n_grp) if early else (lambda i: i)
    _interleave(front(order(0), 0))

    def prep_body(i, carry):
        slot = i & 1
        _interleave(back(order(i), slot), front(order(i + 1), 1 - slot))
        return carry

    lax.fori_loop(0, n_grp - 1, prep_body, 0)
    for s, hh in chains:
        for d in range(2):
            s_s[s * hp + hh, d] = s0_ref[s, d, hh] if has_s0 else jnp.zeros((HEAD_W, HEAD_W), F32)

    def early_scan():
        for i in range(gc):
            scan_body(True, False, i, 0)
            yield

    last = back(order(n_grp - 1), (n_grp - 1) & 1)
    _interleave(last, early_scan()) if early else _interleave(last)
    n_early = gc if early else 0

    lax.fori_loop(n_early, half, functools.partial(scan_body, True, False), 0)
    scan_body(False, False, half, 0)
    lax.fori_loop(half + 1, n_ch, functools.partial(scan_body, False, True), 0)
    emit_out(n_ch - 1)
    emit_out(0)
    if out_state:
        for s, hh in chains:
            for d in range(2):
                st_ref[s, d, hh] = s_s[s * hp + hh, d]


def _dn_call(proj3, conv_w, alog_row, dtb_row, dn_norm, state_delta, layer, depth, out_state, states_so_far):
    n_seq, seq_len, _ = proj3.shape
    n_ch = seq_len // CHUNK
    hp = DN_HEADS_PER_STEP
    ns = max(1, min(n_seq, DN_CHUNKS_PER_STEP // n_ch))
    assert n_seq % ns == 0
    tot = ns * n_ch
    nu = min(PREP_UNROLL, tot) * hp
    wid = hp * HEAD_W
    nblk = DN_HEADS // hp
    has_s0 = state_delta is not None
    tile = lambda off: pl.BlockSpec((ns, seq_len, wid), lambda b, h: (b, 0, off + h))
    cw = lambda off: pl.BlockSpec((CONV_K, wid), lambda b, h: (0, off + h))
    row = pl.BlockSpec((1, HEAD_W), lambda b, h: (0, 0))
    in_specs = [tile(0), tile(nblk), tile(2 * nblk), tile(3 * nblk),
                pl.BlockSpec((ns, seq_len, HEAD_W), lambda b, h: (b, 0, GATE_BLK)),
                cw(0), cw(nblk), cw(2 * nblk), row, row, row]
    args = [proj3, proj3, proj3, proj3, proj3, conv_w, conv_w, conv_w, alog_row, dtb_row,
            dn_norm.reshape(1, HEAD_W)]
    if has_s0:
        in_specs.append(pl.BlockSpec((ns, None, 2, hp, HEAD_W, HEAD_W),
                                     lambda b, h: (b, layer, 0, h, 0, 0)))
        args.append(state_delta)
    out_shape = [jax.ShapeDtypeStruct((n_seq, seq_len, DN_W), BF16)]
    out_specs = [pl.BlockSpec((ns, seq_len, wid), lambda b, h: (b, 0, h))]
    aliases = {}
    if out_state:
        out_shape.append(jax.ShapeDtypeStruct((n_seq, depth, 2, DN_HEADS, HEAD_W, HEAD_W), F32))
        out_specs.append(pl.BlockSpec((ns, None, 2, hp, HEAD_W, HEAD_W), lambda b, h: (b, layer, 0, h, 0, 0)))
        if states_so_far is not None:
            in_specs.append(pl.BlockSpec(memory_space=pl.ANY))
            args.append(states_so_far)
            aliases = {len(args) - 1: 1}
    scratch = [
        pltpu.VMEM((hp, 2, tot, 2 * CHUNK, HEAD_W), BF16),
        pltpu.VMEM((hp, 2, tot, CHUNK, HEAD_W), F32),
        pltpu.VMEM((hp, tot, HEAD_W, 2 * CHUNK), BF16),
        pltpu.VMEM((hp, 2, tot, 8, HEAD_W), F32),
        pltpu.VMEM((hp, tot, CHUNK, 2 * CHUNK), BF16),
        pltpu.VMEM((hp, 2, tot, CHUNK, HEAD_W), BF16),
        pltpu.VMEM((hp, tot, CHUNK, HEAD_W), F32),
        pltpu.VMEM((ns * hp, 2, HEAD_W, HEAD_W), F32),
        pltpu.VMEM((3 * nu, CHUNK + 16, HEAD_W), F32),
        pltpu.VMEM((2, nu, CHUNK, 2 * CHUNK), BF16),
        pltpu.VMEM((2, nu, 2 * CHUNK, 2 * HEAD_W), BF16),
    ]
    outs = pl.pallas_call(
        functools.partial(_dn_kernel, seq_len=seq_len, ns=ns, hp=hp, has_s0=has_s0, out_state=out_state,
                          has_acc=bool(aliases)),
        out_shape=out_shape,
        grid=(n_seq // ns, nblk),
        in_specs=in_specs,
        out_specs=out_specs,
        scratch_shapes=scratch,
        input_output_aliases=aliases,
        compiler_params=pltpu.CompilerParams(vmem_limit_bytes=V7X_VMEM_LIMIT),
        name="deltanet",
    )(*args)
    return (outs[0], outs[1]) if out_state else (outs[0], None)


def _split2(x):
    hi = x.astype(BF16)
    lo = (x - hi.astype(F32)).astype(BF16)
    return hi, lo


def _window(pos, win, n):
    lo = jnp.clip(pos - win // 2, 0, n)
    hi = jnp.clip(pos - win // 2 + win, 0, n)
    return lo, hi


def _pool_kernel(u_ref, pw_ref, ps_ref, o_ref, m_s, *, seq_len, rows, ns):
    period = GRID_W if rows else seq_len
    tb = min(POOL_TB, seq_len)
    r_i = _iota((tb, tb), 0)
    c_i = _iota((tb, tb), 1)
    shift = period.bit_length() - 1
    same_line = (r_i >> shift) == (c_i >> shift)
    pos_r = r_i & (period - 1)
    pos_c = c_i & (period - 1)
    pos_col = _iota((tb, HEAD_W), 0) & (period - 1)
    band, inv_cnt = {}, {}
    for g, win in enumerate(POOL_WINDOWS):
        cols = slice(g * HEAD_W, (g + 1) * HEAD_W)
        if rows:
            run = None
            prev_lo = prev_hi = 0
            for r in range(rows):
                lo, hi = max(r - win // 2, 0), min(r - win // 2 + win, rows)
                for a in range(prev_hi, hi):
                    slab = u_ref[0, a * GRID_W:(a + 1) * GRID_W, cols]
                    run = slab if run is None else run + slab
                for a in range(prev_lo, lo):
                    run = run - u_ref[0, a * GRID_W:(a + 1) * GRID_W, cols]
                prev_lo, prev_hi = lo, hi
                m_s[g, r * GRID_W:(r + 1) * GRID_W, :] = run / float(hi - lo)
        lo_r, hi_r = _window(pos_r, win, period)
        band[g] = jnp.where(same_line & (pos_c >= lo_r) & (pos_c < hi_r), 1.0, 0.0).astype(BF16)
        lo_c, hi_c = _window(pos_col, win, period)
        inv_cnt[g] = 1.0 / (hi_c - lo_c).astype(F32)

    items = [(s, g, t0) for s in range(ns) for g in range(POOL_GROUPS) for t0 in range(0, seq_len, tb)]
    gcols = lambda g: slice(g * HEAD_W, (g + 1) * HEAD_W)
    for b0 in range(0, len(items), POOL_BATCH):
        batch = items[b0:b0 + POOL_BATCH]
        ug = [u_ref[s, t0:t0 + tb, gcols(g)] for s, g, t0 in batch]
        src = [m_s[g, t0:t0 + tb, :] for _, g, t0 in batch] if rows else ug
        parts = _each(_split2, src)
        box = [jnp.dot(band[g], hi, preferred_element_type=F32) + jnp.dot(band[g], lo, preferred_element_type=F32)
               for (_, g, _), (hi, lo) in zip(batch, parts)]
        dev = [(bx * inv_cnt[g] - u).astype(BF16) for (_, g, _), bx, u in zip(batch, box, ug)]
        mix = [jnp.dot(d, pw_ref[g], preferred_element_type=F32) for (_, g, _), d in zip(batch, dev)]
        for (s, g, t0), y in zip(batch, mix):
            o_ref[s, t0:t0 + tb, gcols(g)] = (y * ps_ref[:, gcols(g)]).astype(o_ref.dtype)


def _pool_call(proj3, pool_w_bf, pool_scale, layer, rows):
    n_seq, seq_len, _ = proj3.shape
    ns = 1 if rows else max(1, min(n_seq, POOL_TOKENS_PER_STEP // seq_len))
    assert n_seq % ns == 0
    return pl.pallas_call(
        functools.partial(_pool_kernel, seq_len=seq_len, rows=rows, ns=ns),
        out_shape=jax.ShapeDtypeStruct((n_seq, seq_len, POOL_W), BF16),
        grid=(n_seq // ns,),
        in_specs=[
            pl.BlockSpec((ns, seq_len, POOL_W), lambda b: (b, 0, POOL_BLK512)),
            pl.BlockSpec((None, POOL_GROUPS, HEAD_W, HEAD_W), lambda b: (layer, 0, 0, 0)),
            pl.BlockSpec((1, POOL_W), lambda b: (0, 0)),
        ],
        out_specs=pl.BlockSpec((ns, seq_len, POOL_W), lambda b: (b, 0, 0)),
        scratch_shapes=[pltpu.VMEM((POOL_GROUPS, seq_len, HEAD_W), F32)],
        compiler_params=pltpu.CompilerParams(vmem_limit_bytes=V7X_VMEM_LIMIT),
        name="pool",
    )(proj3, pool_w_bf, pool_scale.reshape(1, POOL_W))


def _post_kernel(x_ref, o_ref, p_ref, ada_ref, nf_ref, fin_ref, wo_ref, wgu_ref, wd_ref, y_ref, *, final):
    mix = (jnp.dot(o_ref[...], wo_ref[:DN_W, :], preferred_element_type=F32)
           + jnp.dot(p_ref[...], wo_ref[DN_W:, :], preferred_element_type=F32))
    x = x_ref[...] + ada_ref[2:3, :] * mix
    h = _modulated_norm(x, nf_ref[...], ada_ref[3:4, :], ada_ref[4:5, :])
    gu = jnp.dot(h.astype(BF16), wgu_ref[...], preferred_element_type=F32)
    act = _silu(gu[:, :D_FF]) * gu[:, D_FF:]
    x = x + ada_ref[5:6, :] * jnp.dot(act.astype(BF16), wd_ref[...], preferred_element_type=F32)
    if final:
        x = x * lax.rsqrt(jnp.mean(x * x, axis=-1, keepdims=True) + EPS) * fin_ref[...]
    y_ref[...] = x


def _post_call(x2, o2, p2, ada_l, norm_ffn, final_norm, w_out_bf, w_gu_bf, w_down_bf, layer, seq_len, per_seq,
               final):
    n_tok = x2.shape[0]
    tm = _token_tile(TM_POST, n_tok, seq_len, per_seq)
    const = lambda shape: pl.BlockSpec((None,) + shape, lambda i: (layer, 0, 0), pipeline_mode=pl.Buffered(1))
    return pl.pallas_call(
        functools.partial(_post_kernel, final=final),
        out_shape=jax.ShapeDtypeStruct((n_tok, D_MODEL), F32),
        grid=(n_tok // tm,),
        in_specs=[
            pl.BlockSpec((tm, D_MODEL), lambda i: (i, 0)),
            pl.BlockSpec((tm, DN_W), lambda i: (i, 0)),
            pl.BlockSpec((tm, POOL_W), lambda i: (i, 0)),
            pl.BlockSpec((None, 6, D_MODEL), _ada_row_map(seq_len, tm, per_seq)),
            pl.BlockSpec((1, D_MODEL), lambda i: (0, 0)),
            pl.BlockSpec((1, D_MODEL), lambda i: (0, 0)),
            const((D_MODEL, D_MODEL)),
            const((D_MODEL, 2 * D_FF)),
            const((D_FF, D_MODEL)),
        ],
        out_specs=pl.BlockSpec((tm, D_MODEL), lambda i: (i, 0)),
        compiler_params=pltpu.CompilerParams(vmem_limit_bytes=V7X_VMEM_LIMIT),
        name="post",
    )(x2, o2, p2, ada_l, norm_ffn.reshape(1, D_MODEL), final_norm.reshape(1, D_MODEL),
      w_out_bf, w_gu_bf, w_down_bf)


def _gate_row(p):
    one = jnp.concatenate([jnp.zeros((2 * DN_HEADS,), F32), p.reshape(-1)])
    return jnp.tile(one, HEAD_W // GATE_COLS).reshape(1, HEAD_W)


def _stream_layer(x2, seq_len, rows, per_seq, ada_l, lw, state_delta, states_so_far, layer, depth, final):
    n_seq = x2.shape[0] // seq_len
    proj = _inproj_call(x2, ada_l, lw["norm_mix"], lw["w_in"], layer, seq_len, per_seq)
    proj3 = proj.reshape(n_seq, seq_len, PROJ_W)
    o, st = _dn_call(proj3, lw["conv_w"], lw["alog_row"], lw["dtb_row"], lw["dn_norm"],
                     state_delta, layer, depth, state_delta is None, states_so_far)
    p = _pool_call(proj3, lw["pool_w"], lw["pool_scale"], layer, rows)
    x2 = _post_call(x2, o.reshape(-1, DN_W), p.reshape(-1, POOL_W), ada_l, lw["norm_ffn"], lw["final_norm"],
                    lw["w_out"], lw["w_gu"], lw["w_down"], layer, seq_len, per_seq, final)
    return x2, st


def kernel(x_prompt, x_sample, c, state_delta, c_ctx, w_ada, b_ada, norm_mix, norm_ffn, w_in, conv_w, a_log, dt_bias, dn_norm, pool_w, pool_scale, w_out, w_gu, w_down, final_norm):
    depth = w_ada.shape[0]
    _, ctx_len, _ = x_prompt.shape
    n_lat, lat_len, _ = x_sample.shape
    assert 1 + n_lat <= ADA_ROWS
    c_all = jnp.concatenate([c_ctx[None, :], c, jnp.zeros((ADA_ROWS - 1 - n_lat, D_MODEL), F32)], axis=0)
    ada = _ada_call(c_all, w_ada, b_ada).reshape(depth, ADA_ROWS, 6, D_MODEL)

    qkvz = 4 * DN_W
    n_gate = 4 * DN_HEADS
    xp = x_prompt.reshape(-1, D_MODEL)
    xs = x_sample.reshape(-1, D_MODEL)
    new_state = None
    w_in_bf = w_in.astype(BF16)
    w_in_p = jnp.concatenate(
        [w_in_bf[:, :, :qkvz], w_in_bf[:, :, qkvz + n_gate:], w_in_bf[:, :, qkvz:qkvz + n_gate],
         jnp.zeros((depth, D_MODEL, HEAD_W - n_gate), BF16)], axis=2)
    pool_w_bf, w_out_bf, w_gu_bf, w_down_bf = (w.astype(BF16) for w in (pool_w, w_out, w_gu, w_down))
    for l in range(depth):
        lw = dict(norm_mix=norm_mix[l], norm_ffn=norm_ffn[l], w_in=w_in_p, conv_w=conv_w[l],
                  alog_row=_gate_row(a_log[l]), dtb_row=_gate_row(dt_bias[l]), dn_norm=dn_norm[l],
                  pool_w=pool_w_bf, pool_scale=pool_scale[l], w_out=w_out_bf,
                  w_gu=w_gu_bf, w_down=w_down_bf, final_norm=final_norm)
        final = l == depth - 1
        xp, new_state = _stream_layer(xp, ctx_len, None, False, ada[l], lw, None, new_state, l, depth, final)
        xs, _ = _stream_layer(xs, lat_len, lat_len // GRID_W, True, ada[l], lw, state_delta, None, l, depth, final)
    return (xp.reshape(x_prompt.shape), xs.reshape(x_sample.shape), new_state)
```

```python
import functools

import jax
import jax.numpy as jnp
from jax import lax
from jax.experimental import pallas as pl
from jax.experimental.pallas import tpu as pltpu

F32 = jnp.float32
BF16 = jnp.bfloat16

D_MODEL = 1024
DN_HEADS = 4
HEAD_W = 128
DN_W = DN_HEADS * HEAD_W
POOL_GROUPS = 4
POOL_W = POOL_GROUPS * HEAD_W
POOL_WINDOWS = (2, 4, 8, 16)
GRID_W = 64
CONV_K = 5
CHUNK = 64
D_FF = 2816
EPS = 1e-6

PROJ_W = 3 * DN_W + DN_W + POOL_W + HEAD_W
POOL_BLK512 = 4
GATE_BLK = 20
GATE_COLS = 4 * DN_HEADS
ADA_ROWS = 16

V7X_VMEM_LIMIT = 56 * 1024 * 1024

TM_INPROJ = 1024
TM_POST = 512
POOL_TB = 256
POOL_BATCH = 8
POOL_TOKENS_PER_STEP = 2048
PREP_UNROLL = 8
DN_HEADS_PER_STEP = 2
YIELD_EVERY = 16
DN_CHUNKS_PER_STEP = 32


def _sigmoid(x):
    return 0.5 * jnp.tanh(0.5 * x) + 0.5


def _silu(x):
    return x * _sigmoid(x)


def _softplus(x):
    return jnp.maximum(x, 0.0) + jnp.log(1.0 + jnp.exp(-jnp.abs(x)))


def _bdot(a, b):
    return jnp.dot(a.astype(BF16), b.astype(BF16), preferred_element_type=F32)


def _iota(shape, dim):
    return lax.broadcasted_iota(jnp.int32, shape, dim)


def _ada_kernel(c_ref, w_ref, b_ref, o_ref):
    o_ref[...] = jnp.dot(_silu(c_ref[...]), w_ref[...], preferred_element_type=F32,
                         precision=lax.Precision.HIGHEST) + b_ref[...]


def _ada_call(c_all, w_ada, b_ada):
    depth, _, n_out = w_ada.shape
    tn = 1536
    return pl.pallas_call(
        _ada_kernel,
        out_shape=jax.ShapeDtypeStruct((depth, ADA_ROWS, n_out), F32),
        grid=(depth, n_out // tn),
        in_specs=[
            pl.BlockSpec((ADA_ROWS, D_MODEL), lambda l, j: (0, 0)),
            pl.BlockSpec((None, D_MODEL, tn), lambda l, j: (l, 0, j)),
            pl.BlockSpec((None, 1, tn), lambda l, j: (l, 0, j)),
        ],
        out_specs=pl.BlockSpec((None, ADA_ROWS, tn), lambda l, j: (l, 0, j)),
        compiler_params=pltpu.CompilerParams(vmem_limit_bytes=V7X_VMEM_LIMIT),
        name="ada",
    )(c_all, w_ada, b_ada.reshape(depth, 1, n_out))


def _modulated_norm(x, gain, shift, scale):
    y = x * lax.rsqrt(jnp.mean(x * x, axis=-1, keepdims=True) + EPS) * gain
    return y * (1.0 + scale) + shift


def _inproj_kernel(x_ref, ada_ref, nw_ref, w_ref, o_ref):
    h = _modulated_norm(x_ref[...], nw_ref[...], ada_ref[0:1, :], ada_ref[1:2, :])
    o_ref[...] = jnp.dot(h.astype(BF16), w_ref[...], preferred_element_type=F32)


def _token_tile(tm, n_tok, seq_len, per_seq):
    tm = min(tm, seq_len) if per_seq else tm
    assert n_tok % tm == 0 and (not per_seq or seq_len % tm == 0)
    return tm


def _ada_row_map(seq_len, tm, per_seq):
    if not per_seq:
        return lambda i: (0, 0, 0)
    return lambda i: (1 + (i * tm) // seq_len, 0, 0)


def _inproj_call(x2, ada_l, norm_w, w_in_p, layer, seq_len, per_seq):
    n_tok = x2.shape[0]
    tm = _token_tile(TM_INPROJ, n_tok, seq_len, per_seq)
    return pl.pallas_call(
        _inproj_kernel,
        out_shape=jax.ShapeDtypeStruct((n_tok, PROJ_W), F32),
        grid=(n_tok // tm,),
        in_specs=[
            pl.BlockSpec((tm, D_MODEL), lambda i: (i, 0)),
            pl.BlockSpec((None, 6, D_MODEL), _ada_row_map(seq_len, tm, per_seq)),
            pl.BlockSpec((1, D_MODEL), lambda i: (0, 0)),
            pl.BlockSpec((None, D_MODEL, PROJ_W), lambda i: (layer, 0, 0), pipeline_mode=pl.Buffered(1)),
        ],
        out_specs=pl.BlockSpec((tm, PROJ_W), lambda i: (i, 0)),
        compiler_params=pltpu.CompilerParams(vmem_limit_bytes=V7X_VMEM_LIMIT),
        name="inproj",
    )(x2, ada_l, norm_w.reshape(1, D_MODEL), w_in_p)


def _col_bcast(x, lane, idx):
    col = jnp.sum(jnp.where(lane == idx, x, 0.0), axis=-1, keepdims=True)
    return jnp.broadcast_to(col, x.shape)


def _each(fn, *lists):
    return [fn(*args) for args in zip(*lists)]


def _each_y(fn, *lists, every=YIELD_EVERY):
    out = []
    for i, args in enumerate(zip(*lists)):
        out.append(fn(*args))
        if i % every == every - 1:
            yield
    return out


def _interleave(*gens):
    gens = list(gens)
    while gens:
        for g in list(gens):
            try:
                next(g)
            except StopIteration:
                gens.remove(g)


def _dn_kernel(*refs, seq_len, ns, hp, has_s0, out_state, has_acc):
    n_ch = seq_len // CHUNK
    gc = min(PREP_UNROLL, ns * n_ch)
    n_grp = ns * n_ch // gc
    assert gc * GATE_COLS <= HEAD_W
    units = [(t, hh) for t in range(gc) for hh in range(hp)]
    nu = len(units)
    it = iter(refs)
    q_ref, k_ref, v_ref, z_ref, g_ref = (next(it) for _ in range(5))
    cwq_ref, cwk_ref, cwv_ref = (next(it) for _ in range(3))
    alog_ref, dtb_ref, dnn_ref = (next(it) for _ in range(3))
    s0_ref = next(it) if has_s0 else None
    if has_acc:
        next(it)
    o_ref = next(it)
    st_ref = next(it) if out_state else None
    wq_s, u_s, kdt_s, dl_s, at_s, vn_s, qs_s, s_s, ext_s, lbd_s, rhs_s = (next(it) for _ in range(11))

    head0 = pl.program_id(1) * hp
    row = _iota((CHUNK, 128), 0)
    lane = _iota((CHUNK, 128), 1)
    is_f = lane < CHUNK
    j = lane & (CHUNK - 1)
    incl = (is_f & (row >= j)) | (~is_f & (row <= j))
    strict = (is_f & (row > j)) | (~is_f & (row < j))
    diag = row == j
    eye_p = jnp.where(diag, 1.0, 0.0).astype(F32)
    couple = []
    for lvl in range(CHUNK.bit_length() - 1):
        rb, cb = row >> lvl, j >> lvl
        couple.append((is_f & ((rb & 1) == 1) & (cb == rb - 1)) | (~is_f & ((rb & 1) == 0) & (cb == rb + 1)))
    r128 = _iota((128, 128), 0)
    c128 = _iota((128, 128), 1)
    lane_f128 = c128 < CHUNK
    cum_sel = ((r128 < CHUNK) & (c128 <= r128)) | ((r128 >= CHUNK) & (c128 >= r128 - CHUNK))
    cum_mat = jnp.where(cum_sel, 1.0, 0.0).astype(F32)[:, :CHUNK]

    neg_a = -jnp.exp(alog_ref[...])
    dtb = dtb_ref[...]
    cols = lambda hh: slice(hh * HEAD_W, (hh + 1) * HEAD_W)

    def group_chunks(g):
        ms = [g * gc + t for t in range(gc)]
        return ms, [m // n_ch for m in ms], [m % n_ch for m in ms]

    def conv_silu(ref, cw_ref, hh, s, c, r0, slot):
        p0 = pl.multiple_of(jnp.maximum(r0 - 8, 0), 8)
        n0 = pl.multiple_of(jnp.minimum(r0 + CHUNK, seq_len - 8), 8)
        ext_s[slot, 0:8, :] = jnp.where(c > 0, ref[s, pl.ds(p0, 8), cols(hh)], 0.0)
        ext_s[slot, 8:8 + CHUNK, :] = ref[s, pl.ds(r0, CHUNK), cols(hh)]
        ext_s[slot, 8 + CHUNK:16 + CHUNK, :] = jnp.where(c < n_ch - 1, ref[s, pl.ds(n0, 8), cols(hh)], 0.0)
        acc = ext_s[slot, 6:6 + CHUNK, :] * cw_ref[0:1, cols(hh)]
        for t in range(1, CONV_K):
            acc = acc + ext_s[slot, 6 + t:6 + t + CHUNK, :] * cw_ref[t:t + 1, cols(hh)]
        return _silu(acc)

    def l2n(x):
        return x * lax.rsqrt(jnp.sum(x * x, axis=-1, keepdims=True) + EPS)

    def gates(ss, r0s):
        packed = g_ref[ss[0], pl.ds(r0s[0], CHUNK), :]
        for t in range(1, gc):
            packed = packed + pltpu.roll(g_ref[ss[t], pl.ds(r0s[t], CHUNK), :], GATE_COLS * t, axis=1)
        is_beta = (lane & (GATE_COLS - 1)) < 2 * DN_HEADS
        gact = jnp.where(is_beta, _sigmoid(packed), neg_a * _softplus(packed + dtb))
        cums = jnp.dot(cum_mat, gact, preferred_element_type=F32, precision=lax.Precision.HIGHEST)
        e_cum = jnp.exp(cums)
        e_rest = jnp.concatenate([jnp.exp(cums[CHUNK - 1:CHUNK] - cums[:CHUNK]),
                                  jnp.exp(cums[CHUNK:CHUNK + 1] - cums[CHUNK:])], axis=0)
        return gact, cums, e_cum, e_rest

    def front(g, slot):
        ms, ss, cs = group_chunks(g)
        r0s = [pl.multiple_of(c * CHUNK, CHUNK) for c in cs]
        gact, cums, e_cum, e_rest = gates(ss, r0s)
        yield
        qk, kk, decay, bdecay = [], [], [], []
        for n, (t, hh) in enumerate(units):
            col = lambda x, which: _col_bcast(x, lane, GATE_COLS * t + which * DN_HEADS + head0 + hh)
            beta_f, beta_b = col(gact, 0), col(gact, 1)
            gc_col = jnp.where(is_f, col(cums[:CHUNK], 2), col(cums[CHUNK:], 3))
            e_f, e_b = col(e_cum[:CHUNK], 2), col(e_cum[CHUNK:], 3)
            q = l2n(conv_silu(q_ref, cwq_ref, hh, ss[t], cs[t], r0s[t], n)) * (HEAD_W ** -0.5)
            k = l2n(conv_silu(k_ref, cwk_ref, hh, ss[t], cs[t], r0s[t], nu + n))
            v = conv_silu(v_ref, cwv_ref, hh, ss[t], cs[t], r0s[t], 2 * nu + n)
            rhs_s[slot, n] = jnp.concatenate([
                jnp.concatenate([v * beta_f, k * beta_f * e_f], axis=1),
                jnp.concatenate([v * beta_b, k * beta_b * e_b], axis=1)], axis=0).astype(BF16)
            wq_s[hh, 0, ms[t], CHUNK:, :] = (q * e_f).astype(BF16)
            wq_s[hh, 1, ms[t], CHUNK:, :] = (q * e_b).astype(BF16)
            kdec = jnp.concatenate([k * col(e_rest[:CHUNK], 2), k * col(e_rest[CHUNK:], 3)], axis=0)
            kdt_s[hh, ms[t]] = kdec.T.astype(BF16)
            dl_s[hh, 0, ms[t]] = jnp.broadcast_to(e_f[CHUNK - 1:CHUNK, :], (8, 128))
            dl_s[hh, 1, ms[t]] = jnp.broadcast_to(e_b[0:1, :], (8, 128))
            kb = k.astype(BF16)
            qk.append(jnp.concatenate([q.astype(BF16), kb], axis=0))
            kk.append(jnp.concatenate([kb, kb], axis=0))
            gc_row = jnp.sum(jnp.where(diag, gc_col, 0.0), axis=0, keepdims=True)
            decay.append(jnp.where(incl, jnp.exp(jnp.where(incl, gc_col - gc_row, 0.0)), 0.0))
            bdecay.append(jnp.where(strict, jnp.where(is_f, beta_f, beta_b) * decay[-1], 0.0))
            yield
        gram = yield from _each_y(
            lambda a, b: lax.dot_general(a, b, (((1,), (1,)), ((), ())), preferred_element_type=F32),
            qk, kk, every=2)
        for n, (t, hh) in enumerate(units):
            lbd_s[slot, n] = (bdecay[n] * gram[n][CHUNK:]).astype(BF16)
            at_s[hh, ms[t]] = (gram[n][:CHUNK] * decay[n]).astype(BF16)
            if n % 2 == 1:
                yield

    def back(g, slot):
        ms = [g * gc + t for t in range(gc)]
        l2 = [lbd_s[slot, n] for n in range(nu)]
        zero = jnp.zeros((CHUNK, 128), BF16)

        def bd(x):
            x = x.astype(BF16)
            return jnp.concatenate([jnp.where(is_f, x, zero), jnp.where(is_f, zero, x)], axis=0)

        pdot = lambda a, b_bd: jnp.dot(a.astype(BF16), b_bd, preferred_element_type=F32)
        inv = _each(lambda x: eye_p - jnp.where(couple[0], x, zero).astype(F32), l2)
        for lvl in range(1, len(couple)):
            c_bd = _each(lambda x: bd(jnp.where(couple[lvl], x, zero)), l2)
            xc = yield from _each_y(lambda a, b: pdot(a, b).astype(BF16), inv, c_bd)
            inv = yield from _each_y(lambda x, y: x - pdot(y, bd(x)), inv, xc)
        uw = yield from _each_y(lambda tt, n: jnp.dot(bd(tt), rhs_s[slot, n], preferred_element_type=F32),
                                inv, range(nu), every=2)
        for n, (t, hh) in enumerate(units):
            u_s[hh, 0, ms[t]] = uw[n][:CHUNK, :HEAD_W]
            u_s[hh, 1, ms[t]] = uw[n][CHUNK:, :HEAD_W]
            wq_s[hh, 0, ms[t], :CHUNK, :] = uw[n][:CHUNK, HEAD_W:].astype(BF16)
            wq_s[hh, 1, ms[t], :CHUNK, :] = uw[n][CHUNK:, HEAD_W:].astype(BF16)

    chains = [(s, hh) for s in range(ns) for hh in range(hp)]

    dnn = dnn_ref[...]

    def emit_out(c):
        r0 = c * CHUNK if isinstance(c, int) else pl.multiple_of(c * CHUNK, CHUNK)
        for s, hh in chains:
            m = s * n_ch + c
            vst = jnp.concatenate([vn_s[hh, 0, m], vn_s[hh, 1, m]], axis=0)
            o = qs_s[hh, m] + jnp.dot(at_s[hh, m], vst, preferred_element_type=F32)
            y = o * lax.rsqrt(jnp.mean(o * o, axis=-1, keepdims=True) + EPS) * dnn
            o_ref[s, pl.ds(r0, CHUNK), cols(hh)] = (
                y * _silu(z_ref[s, pl.ds(r0, CHUNK), cols(hh)])).astype(o_ref.dtype)

    def scan_body(first, with_out, i, carry):
        if with_out:
            emit_out(i - 1)
            emit_out(n_ch - i)
        todo = [(s, hh, d, s * n_ch + (i if d == 0 else n_ch - 1 - i)) for s, hh in chains for d in range(2)]
        st = [s_s[s * hp + hh, d] for s, hh, d, m in todo]
        r = [jnp.dot(wq_s[hh, d, m], x.astype(BF16), preferred_element_type=F32)
             for (s, hh, d, m), x in zip(todo, st)]
        vn = []
        for (s, hh, d, m), rr in zip(todo, r):
            vn.append((u_s[hh, d, m] - rr[:CHUNK]).astype(BF16))
            vn_s[hh, d, m] = vn[-1]
        upd = []
        for (s, hh, d, m), x in zip(todo, vn):
            kdt = kdt_s[hh, m]
            zero = jnp.zeros_like(kdt)
            kdt = jnp.where(lane_f128, kdt, zero) if d == 0 else jnp.where(lane_f128, zero, kdt)
            upd.append(jnp.dot(kdt, jnp.concatenate([x, x], axis=0), preferred_element_type=F32))
        for (s, hh, d, m), x, rr, up in zip(todo, st, r, upd):
            s_s[s * hp + hh, d] = x * dl_s[hh, d, m][0:1, :] + up
            qs_s[hh, m] = rr[CHUNK:] if first else qs_s[hh, m] + rr[CHUNK:]
        return carry

    half = n_ch // 2
    early = ns == 1 and n_grp >= 3 and gc <= half
    order = (lambda i: (i + n_grp - 1) % n_grp) if early else (lambda i: i)
    _interleave(front(order(0), 0))

    def prep_body(i, carry):
        slot = i & 1
        _interleave(back(order(i), slot), front(order(i + 1), 1 - slot))
        return carry

    lax.fori_loop(0, n_grp - 1, prep_body, 0)
    for s, hh in chains:
        for d in range(2):
            s_s[s * hp + hh, d] = s0_ref[s, d, hh] if has_s0 else jnp.zeros((HEAD_W, HEAD_W), F32)

    def early_scan():
        for i in range(gc):
            scan_body(True, False, i, 0)
            yield

    last = back(order(n_grp - 1), (n_grp - 1) & 1)
    _interleave(last, early_scan()) if early else _interleave(last)
    n_early = gc if early else 0

    lax.fori_loop(n_early, half, functools.partial(scan_body, True, False), 0)
    scan_body(False, False, half, 0)
    lax.fori_loop(half + 1, n_ch, functools.partial(scan_body, False, True), 0)
    emit_out(n_ch - 1)
    emit_out(0)
    if out_state:
        for s, hh in chains:
            for d in range(2):
                st_ref[s, d, hh] = s_s[s * hp + hh, d]


def _dn_call(proj3, conv_w, alog_row, dtb_row, dn_norm, state_delta, layer, depth, out_state, states_so_far):
    n_seq, seq_len, _ = proj3.shape
    n_ch = seq_len // CHUNK
    hp = DN_HEADS_PER_STEP
    ns = max(1, min(n_seq, DN_CHUNKS_PER_STEP // n_ch))
    assert n_seq % ns == 0
    tot = ns * n_ch
    nu = min(PREP_UNROLL, tot) * hp
    wid = hp * HEAD_W
    nblk = DN_HEADS // hp
    has_s0 = state_delta is not None
    tile = lambda off: pl.BlockSpec((ns, seq_len, wid), lambda b, h: (b, 0, off + h))
    cw = lambda off: pl.BlockSpec((CONV_K, wid), lambda b, h: (0, off + h))
    row = pl.BlockSpec((1, HEAD_W), lambda b, h: (0, 0))
    in_specs = [tile(0), tile(nblk), tile(2 * nblk), tile(3 * nblk),
                pl.BlockSpec((ns, seq_len, HEAD_W), lambda b, h: (b, 0, GATE_BLK)),
                cw(0), cw(nblk), cw(2 * nblk), row, row, row]
    args = [proj3, proj3, proj3, proj3, proj3, conv_w, conv_w, conv_w, alog_row, dtb_row,
            dn_norm.reshape(1, HEAD_W)]
    if has_s0:
        in_specs.append(pl.BlockSpec((ns, None, 2, hp, HEAD_W, HEAD_W),
                                     lambda b, h: (b, layer, 0, h, 0, 0)))
        args.append(state_delta)
    out_shape = [jax.ShapeDtypeStruct((n_seq, seq_len, DN_W), BF16)]
    out_specs = [pl.BlockSpec((ns, seq_len, wid), lambda b, h: (b, 0, h))]
    aliases = {}
    if out_state:
        out_shape.append(jax.ShapeDtypeStruct((n_seq, depth, 2, DN_HEADS, HEAD_W, HEAD_W), F32))
        out_specs.append(pl.BlockSpec((ns, None, 2, hp, HEAD_W, HEAD_W), lambda b, h: (b, layer, 0, h, 0, 0)))
        if states_so_far is not None:
            in_specs.append(pl.BlockSpec(memory_space=pl.ANY))
            args.append(states_so_far)
            aliases = {len(args) - 1: 1}
    scratch = [
        pltpu.VMEM((hp, 2, tot, 2 * CHUNK, HEAD_W), BF16),
        pltpu.VMEM((hp, 2, tot, CHUNK, HEAD_W), F32),
        pltpu.VMEM((hp, tot, HEAD_W, 2 * CHUNK), BF16),
        pltpu.VMEM((hp, 2, tot, 8, HEAD_W), F32),
        pltpu.VMEM((hp, tot, CHUNK, 2 * CHUNK), BF16),
        pltpu.VMEM((hp, 2, tot, CHUNK, HEAD_W), BF16),
        pltpu.VMEM((hp, tot, CHUNK, HEAD_W), F32),
        pltpu.VMEM((ns * hp, 2, HEAD_W, HEAD_W), F32),
        pltpu.VMEM((3 * nu, CHUNK + 16, HEAD_W), F32),
        pltpu.VMEM((2, nu, CHUNK, 2 * CHUNK), BF16),
        pltpu.VMEM((2, nu, 2 * CHUNK, 2 * HEAD_W), BF16),
    ]
    outs = pl.pallas_call(
        functools.partial(_dn_kernel, seq_len=seq_len, ns=ns, hp=hp, has_s0=has_s0, out_state=out_state,
                          has_acc=bool(aliases)),
        out_shape=out_shape,
        grid=(n_seq // ns, nblk),
        in_specs=in_specs,
        out_specs=out_specs,
        scratch_shapes=scratch,
        input_output_aliases=aliases,
        compiler_params=pltpu.CompilerParams(vmem_limit_bytes=V7X_VMEM_LIMIT),
        name="deltanet",
    )(*args)
    return (outs[0], outs[1]) if out_state else (outs[0], None)


def _split2(x):
    hi = x.astype(BF16)
    lo = (x - hi.astype(F32)).astype(BF16)
    return hi, lo


def _window(pos, win, n):
    lo = jnp.clip(pos - win // 2, 0, n)
    hi = jnp.clip(pos - win // 2 + win, 0, n)
    return lo, hi


def _pool_kernel(u_ref, pw_ref, ps_ref, o_ref, m_s, *, seq_len, rows, ns):
    period = GRID_W if rows else seq_len
    tb = min(POOL_TB, seq_len)
    r_i = _iota((tb, tb), 0)
    c_i = _iota((tb, tb), 1)
    shift = period.bit_length() - 1
    same_line = (r_i >> shift) == (c_i >> shift)
    pos_r = r_i & (period - 1)
    pos_c = c_i & (period - 1)
    pos_col = _iota((tb, HEAD_W), 0) & (period - 1)
    band, inv_cnt = {}, {}
    for g, win in enumerate(POOL_WINDOWS):
        cols = slice(g * HEAD_W, (g + 1) * HEAD_W)
        if rows:
            run = None
            prev_lo = prev_hi = 0
            for r in range(rows):
                lo, hi = max(r - win // 2, 0), min(r - win // 2 + win, rows)
                for a in range(prev_hi, hi):
                    slab = u_ref[0, a * GRID_W:(a + 1) * GRID_W, cols]
                    run = slab if run is None else run + slab
                for a in range(prev_lo, lo):
                    run = run - u_ref[0, a * GRID_W:(a + 1) * GRID_W, cols]
                prev_lo, prev_hi = lo, hi
                m_s[g, r * GRID_W:(r + 1) * GRID_W, :] = run / float(hi - lo)
        lo_r, hi_r = _window(pos_r, win, period)
        band[g] = jnp.where(same_line & (pos_c >= lo_r) & (pos_c < hi_r), 1.0, 0.0).astype(BF16)
        lo_c, hi_c = _window(pos_col, win, period)
        inv_cnt[g] = 1.0 / (hi_c - lo_c).astype(F32)

    items = [(s, g, t0) for s in range(ns) for g in range(POOL_GROUPS) for t0 in range(0, seq_len, tb)]
    gcols = lambda g: slice(g * HEAD_W, (g + 1) * HEAD_W)
    for b0 in range(0, len(items), POOL_BATCH):
        batch = items[b0:b0 + POOL_BATCH]
        ug = [u_ref[s, t0:t0 + tb, gcols(g)] for s, g, t0 in batch]
        src = [m_s[g, t0:t0 + tb, :] for _, g, t0 in batch] if rows else ug
        parts = _each(_split2, src)
        box = [jnp.dot(band[g], hi, preferred_element_type=F32) + jnp.dot(band[g], lo, preferred_element_type=F32)
               for (_, g, _), (hi, lo) in zip(batch, parts)]
        dev = [(bx * inv_cnt[g] - u).astype(BF16) for (_, g, _), bx, u in zip(batch, box, ug)]
        mix = [jnp.dot(d, pw_ref[g], preferred_element_type=F32) for (_, g, _), d in zip(batch, dev)]
        for (s, g, t0), y in zip(batch, mix):
            o_ref[s, t0:t0 + tb, gcols(g)] = (y * ps_ref[:, gcols(g)]).astype(o_ref.dtype)


def _pool_call(proj3, pool_w_bf, pool_scale, layer, rows):
    n_seq, seq_len, _ = proj3.shape
    ns = 1 if rows else max(1, min(n_seq, POOL_TOKENS_PER_STEP // seq_len))
    assert n_seq % ns == 0
    return pl.pallas_call(
        functools.partial(_pool_kernel, seq_len=seq_len, rows=rows, ns=ns),
        out_shape=jax.ShapeDtypeStruct((n_seq, seq_len, POOL_W), BF16),
        grid=(n_seq // ns,),
        in_specs=[
            pl.BlockSpec((ns, seq_len, POOL_W), lambda b: (b, 0, POOL_BLK512)),
            pl.BlockSpec((None, POOL_GROUPS, HEAD_W, HEAD_W), lambda b: (layer, 0, 0, 0)),
            pl.BlockSpec((1, POOL_W), lambda b: (0, 0)),
        ],
        out_specs=pl.BlockSpec((ns, seq_len, POOL_W), lambda b: (b, 0, 0)),
        scratch_shapes=[pltpu.VMEM((POOL_GROUPS, seq_len, HEAD_W), F32)],
        compiler_params=pltpu.CompilerParams(vmem_limit_bytes=V7X_VMEM_LIMIT),
        name="pool",
    )(proj3, pool_w_bf, pool_scale.reshape(1, POOL_W))


def _post_kernel(x_ref, o_ref, p_ref, ada_ref, nf_ref, fin_ref, wo_ref, wgu_ref, wd_ref, y_ref, *, final):
    mix = (jnp.dot(o_ref[...], wo_ref[:DN_W, :], preferred_element_type=F32)
           + jnp.dot(p_ref[...], wo_ref[DN_W:, :], preferred_element_type=F32))
    x = x_ref[...] + ada_ref[2:3, :] * mix
    h = _modulated_norm(x, nf_ref[...], ada_ref[3:4, :], ada_ref[4:5, :])
    gu = jnp.dot(h.astype(BF16), wgu_ref[...], preferred_element_type=F32)
    act = _silu(gu[:, :D_FF]) * gu[:, D_FF:]
    x = x + ada_ref[5:6, :] * jnp.dot(act.astype(BF16), wd_ref[...], preferred_element_type=F32)
    if final:
        x = x * lax.rsqrt(jnp.mean(x * x, axis=-1, keepdims=True) + EPS) * fin_ref[...]
    y_ref[...] = x


def _post_call(x2, o2, p2, ada_l, norm_ffn, final_norm, w_out_bf, w_gu_bf, w_down_bf, layer, seq_len, per_seq,
               final):
    n_tok = x2.shape[0]
    tm = _token_tile(TM_POST, n_tok, seq_len, per_seq)
    const = lambda shape: pl.BlockSpec((None,) + shape, lambda i: (layer, 0, 0), pipeline_mode=pl.Buffered(1))
    return pl.pallas_call(
        functools.partial(_post_kernel, final=final),
        out_shape=jax.ShapeDtypeStruct((n_tok, D_MODEL), F32),
        grid=(n_tok // tm,),
        in_specs=[
            pl.BlockSpec((tm, D_MODEL), lambda i: (i, 0)),
            pl.BlockSpec((tm, DN_W), lambda i: (i, 0)),
            pl.BlockSpec((tm, POOL_W), lambda i: (i, 0)),
            pl.BlockSpec((None, 6, D_MODEL), _ada_row_map(seq_len, tm, per_seq)),
            pl.BlockSpec((1, D_MODEL), lambda i: (0, 0)),
            pl.BlockSpec((1, D_MODEL), lambda i: (0, 0)),
            const((D_MODEL, D_MODEL)),
            const((D_MODEL, 2 * D_FF)),
            const((D_FF, D_MODEL)),
        ],
        out_specs=pl.BlockSpec((tm, D_MODEL), lambda i: (i, 0)),
        compiler_params=pltpu.CompilerParams(vmem_limit_bytes=V7X_VMEM_LIMIT),
        name="post",
    )(x2, o2, p2, ada_l, norm_ffn.reshape(1, D_MODEL), final_norm.reshape(1, D_MODEL),
      w_out_bf, w_gu_bf, w_down_bf)


def _gate_row(p):
    one = jnp.concatenate([jnp.zeros((2 * DN_HEADS,), F32), p.reshape(-1)])
    return jnp.tile(one, HEAD_W // GATE_COLS).reshape(1, HEAD_W)


def _stream_layer(x2, seq_len, rows, per_seq, ada_l, lw, state_delta, states_so_far, layer, depth, final):
    n_seq = x2.shape[0] // seq_len
    proj = _inproj_call(x2, ada_l, lw["norm_mix"], lw["w_in"], layer, seq_len, per_seq)
    proj3 = proj.reshape(n_seq, seq_len, PROJ_W)
    o, st = _dn_call(proj3, lw["conv_w"], lw["alog_row"], lw["dtb_row"], lw["dn_norm"],
                     state_delta, layer, depth, state_delta is None, states_so_far)
    p = _pool_call(proj3, lw["pool_w"], lw["pool_scale"], layer, rows)
    x2 = _post_call(x2, o.reshape(-1, DN_W), p.reshape(-1, POOL_W), ada_l, lw["norm_ffn"], lw["final_norm"],
                    lw["w_out"], lw["w_gu"], lw["w_down"], layer, seq_len, per_seq, final)
    return x2, st


def kernel(x_prompt, x_sample, c, state_delta, c_ctx, w_ada, b_ada, norm_mix, norm_ffn, w_in, conv_w, a_log, dt_bias, dn_norm, pool_w, pool_scale, w_out, w_gu, w_down, final_norm):
    depth = w_ada.shape[0]
    _, ctx_len, _ = x_prompt.shape
    n_lat, lat_len, _ = x_sample.shape
    assert 1 + n_lat <= ADA_ROWS
    c_all = jnp.concatenate([c_ctx[None, :], c, jnp.zeros((ADA_ROWS - 1 - n_lat, D_MODEL), F32)], axis=0)
    ada = _ada_call(c_all, w_ada, b_ada).reshape(depth, ADA_ROWS, 6, D_MODEL)

    qkvz = 4 * DN_W
    n_gate = 4 * DN_HEADS
    xp = x_prompt.reshape(-1, D_MODEL)
    xs = x_sample.reshape(-1, D_MODEL)
    new_state = None
    w_in_bf = w_in.astype(BF16)
    w_in_p = jnp.concatenate(
        [w_in_bf[:, :, :qkvz], w_in_bf[:, :, qkvz + n_gate:], w_in_bf[:, :, qkvz:qkvz + n_gate],
         jnp.zeros((depth, D_MODEL, HEAD_W - n_gate), BF16)], axis=2)
    pool_w_bf, w_out_bf, w_gu_bf, w_down_bf = (w.astype(BF16) for w in (pool_w, w_out, w_gu, w_down))
    for l in range(depth):
        lw = dict(norm_mix=norm_mix[l], norm_ffn=norm_ffn[l], w_in=w_in_p, conv_w=conv_w[l],
                  alog_row=_gate_row(a_log[l]), dtb_row=_gate_row(dt_bias[l]), dn_norm=dn_norm[l],
                  pool_w=pool_w_bf, pool_scale=pool_scale[l], w_out=w_out_bf,
                  w_gu=w_gu_bf, w_down=w_down_bf, final_norm=final_norm)
        final = l == depth - 1
        xp, new_state = _stream_layer(xp, ctx_len, None, False, ada[l], lw, None, new_state, l, depth, final)
        xs, _ = _stream_layer(xs, lat_len, lat_len // GRID_W, True, ada[l], lw, state_delta, None, l, depth, final)
    return (xp.reshape(x_prompt.shape), xs.reshape(x_sample.shape), new_state)
```

```python
import functools

import jax
import jax.numpy as jnp
from jax import lax
from jax.experimental import pallas as pl
from jax.experimental.pallas import tpu as pltpu

F32 = jnp.float32
BF16 = jnp.bfloat16

D_MODEL = 1024
DN_HEADS = 4
HEAD_W = 128
DN_W = DN_HEADS * HEAD_W
POOL_GROUPS = 4
POOL_W = POOL_GROUPS * HEAD_W
POOL_WINDOWS = (2, 4, 8, 16)
GRID_W = 64
CONV_K = 5
CHUNK = 64
D_FF = 2816
EPS = 1e-6

PROJ_W = 3 * DN_W + DN_W + POOL_W + HEAD_W
POOL_BLK512 = 4
GATE_BLK = 20
GATE_COLS = 4 * DN_HEADS
ADA_ROWS = 16

V7X_VMEM_LIMIT = 56 * 1024 * 1024

TM_INPROJ = 512
TM_POST = 512
POOL_TB = 256
POOL_BATCH = 8
POOL_TOKENS_PER_STEP = 2048
PREP_UNROLL = 8
DN_HEADS_PER_STEP = 2
YIELD_EVERY = 16
DN_CHUNKS_PER_STEP = 32


def _sigmoid(x):
    return 0.5 * jnp.tanh(0.5 * x) + 0.5


def _silu(x):
    return x * _sigmoid(x)


def _softplus(x):
    return jnp.maximum(x, 0.0) + jnp.log(1.0 + jnp.exp(-jnp.abs(x)))


def _bdot(a, b):
    return jnp.dot(a.astype(BF16), b.astype(BF16), preferred_element_type=F32)


def _iota(shape, dim):
    return lax.broadcasted_iota(jnp.int32, shape, dim)


def _ada_kernel(c_ref, w_ref, b_ref, o_ref):
    o_ref[...] = jnp.dot(_silu(c_ref[...]), w_ref[...], preferred_element_type=F32,
                         precision=lax.Precision.HIGHEST) + b_ref[...]


def _ada_call(c_all, w_ada, b_ada):
    depth, _, n_out = w_ada.shape
    tn = 1536
    return pl.pallas_call(
        _ada_kernel,
        out_shape=jax.ShapeDtypeStruct((depth, ADA_ROWS, n_out), F32),
        grid=(depth, n_out // tn),
        in_specs=[
            pl.BlockSpec((ADA_ROWS, D_MODEL), lambda l, j: (0, 0)),
            pl.BlockSpec((None, D_MODEL, tn), lambda l, j: (l, 0, j)),
            pl.BlockSpec((None, 1, tn), lambda l, j: (l, 0, j)),
        ],
        out_specs=pl.BlockSpec((None, ADA_ROWS, tn), lambda l, j: (l, 0, j)),
        compiler_params=pltpu.CompilerParams(vmem_limit_bytes=V7X_VMEM_LIMIT),
        name="ada",
    )(c_all, w_ada, b_ada.reshape(depth, 1, n_out))


def _modulated_norm(x, gain, shift, scale):
    y = x * lax.rsqrt(jnp.mean(x * x, axis=-1, keepdims=True) + EPS) * gain
    return y * (1.0 + scale) + shift


def _inproj_kernel(x_ref, ada_ref, nw_ref, w_ref, o_ref):
    h = _modulated_norm(x_ref[...], nw_ref[...], ada_ref[0:1, :], ada_ref[1:2, :])
    o_ref[...] = jnp.dot(h.astype(BF16), w_ref[...], preferred_element_type=F32)


def _token_tile(tm, n_tok, seq_len, per_seq):
    tm = min(tm, seq_len) if per_seq else tm
    assert n_tok % tm == 0 and (not per_seq or seq_len % tm == 0)
    return tm


def _ada_row_map(seq_len, tm, per_seq):
    if not per_seq:
        return lambda i: (0, 0, 0)
    return lambda i: (1 + (i * tm) // seq_len, 0, 0)


def _inproj_call(x2, ada_l, norm_w, w_in_p, layer, seq_len, per_seq):
    n_tok = x2.shape[0]
    tm = _token_tile(TM_INPROJ, n_tok, seq_len, per_seq)
    return pl.pallas_call(
        _inproj_kernel,
        out_shape=jax.ShapeDtypeStruct((n_tok, PROJ_W), F32),
        grid=(n_tok // tm,),
        in_specs=[
            pl.BlockSpec((tm, D_MODEL), lambda i: (i, 0)),
            pl.BlockSpec((None, 6, D_MODEL), _ada_row_map(seq_len, tm, per_seq)),
            pl.BlockSpec((1, D_MODEL), lambda i: (0, 0)),
            pl.BlockSpec((None, D_MODEL, PROJ_W), lambda i: (layer, 0, 0), pipeline_mode=pl.Buffered(1)),
        ],
        out_specs=pl.BlockSpec((tm, PROJ_W), lambda i: (i, 0)),
        compiler_params=pltpu.CompilerParams(vmem_limit_bytes=V7X_VMEM_LIMIT),
        name="inproj",
    )(x2, ada_l, norm_w.reshape(1, D_MODEL), w_in_p)


def _col_bcast(x, lane, idx):
    col = jnp.sum(jnp.where(lane == idx, x, 0.0), axis=-1, keepdims=True)
    return jnp.broadcast_to(col, x.shape)


def _each(fn, *lists):
    return [fn(*args) for args in zip(*lists)]


def _each_y(fn, *lists, every=YIELD_EVERY):
    out = []
    for i, args in enumerate(zip(*lists)):
        out.append(fn(*args))
        if i % every == every - 1:
            yield
    return out


def _interleave(*gens):
    gens = list(gens)
    while gens:
        for g in list(gens):
            try:
                next(g)
            except StopIteration:
                gens.remove(g)


def _dn_kernel(*refs, seq_len, ns, hp, has_s0, out_state, has_acc):
    n_ch = seq_len // CHUNK
    gc = min(PREP_UNROLL, ns * n_ch)
    n_grp = ns * n_ch // gc
    assert gc * GATE_COLS <= HEAD_W
    units = [(t, hh) for t in range(gc) for hh in range(hp)]
    nu = len(units)
    it = iter(refs)
    q_ref, k_ref, v_ref, z_ref, g_ref = (next(it) for _ in range(5))
    cwq_ref, cwk_ref, cwv_ref = (next(it) for _ in range(3))
    alog_ref, dtb_ref, dnn_ref = (next(it) for _ in range(3))
    s0_ref = next(it) if has_s0 else None
    if has_acc:
        next(it)
    o_ref = next(it)
    st_ref = next(it) if out_state else None
    wq_s, u_s, kdt_s, dl_s, at_s, vn_s, qs_s, s_s, ext_s, lbd_s, rhs_s = (next(it) for _ in range(11))

    head0 = pl.program_id(1) * hp
    row = _iota((CHUNK, 128), 0)
    lane = _iota((CHUNK, 128), 1)
    is_f = lane < CHUNK
    j = lane & (CHUNK - 1)
    incl = (is_f & (row >= j)) | (~is_f & (row <= j))
    strict = (is_f & (row > j)) | (~is_f & (row < j))
    diag = row == j
    eye_p = jnp.where(diag, 1.0, 0.0).astype(F32)
    couple = []
    for lvl in range(CHUNK.bit_length() - 1):
        rb, cb = row >> lvl, j >> lvl
        couple.append((is_f & ((rb & 1) == 1) & (cb == rb - 1)) | (~is_f & ((rb & 1) == 0) & (cb == rb + 1)))
    r128 = _iota((128, 128), 0)
    c128 = _iota((128, 128), 1)
    lane_f128 = c128 < CHUNK
    cum_sel = ((r128 < CHUNK) & (c128 <= r128)) | ((r128 >= CHUNK) & (c128 >= r128 - CHUNK))
    cum_mat = jnp.where(cum_sel, 1.0, 0.0).astype(F32)[:, :CHUNK]

    neg_a = -jnp.exp(alog_ref[...])
    dtb = dtb_ref[...]
    cols = lambda hh: slice(hh * HEAD_W, (hh + 1) * HEAD_W)

    def group_chunks(g):
        ms = [g * gc + t for t in range(gc)]
        return ms, [m // n_ch for m in ms], [m % n_ch for m in ms]

    def conv_silu(ref, cw_ref, hh, s, c, r0, slot):
        p0 = pl.multiple_of(jnp.maximum(r0 - 8, 0), 8)
        n0 = pl.multiple_of(jnp.minimum(r0 + CHUNK, seq_len - 8), 8)
        ext_s[slot, 0:8, :] = jnp.where(c > 0, ref[s, pl.ds(p0, 8), cols(hh)], 0.0)
        ext_s[slot, 8:8 + CHUNK, :] = ref[s, pl.ds(r0, CHUNK), cols(hh)]
        ext_s[slot, 8 + CHUNK:16 + CHUNK, :] = jnp.where(c < n_ch - 1, ref[s, pl.ds(n0, 8), cols(hh)], 0.0)
        acc = ext_s[slot, 6:6 + CHUNK, :] * cw_ref[0:1, cols(hh)]
        for t in range(1, CONV_K):
            acc = acc + ext_s[slot, 6 + t:6 + t + CHUNK, :] * cw_ref[t:t + 1, cols(hh)]
        return _silu(acc)

    def l2n(x):
        return x * lax.rsqrt(jnp.sum(x * x, axis=-1, keepdims=True) + EPS)

    def gates(ss, r0s):
        packed = g_ref[ss[0], pl.ds(r0s[0], CHUNK), :]
        for t in range(1, gc):
            packed = packed + pltpu.roll(g_ref[ss[t], pl.ds(r0s[t], CHUNK), :], GATE_COLS * t, axis=1)
        is_beta = (lane & (GATE_COLS - 1)) < 2 * DN_HEADS
        gact = jnp.where(is_beta, _sigmoid(packed), neg_a * _softplus(packed + dtb))
        cums = jnp.dot(cum_mat, gact, preferred_element_type=F32, precision=lax.Precision.HIGHEST)
        e_cum = jnp.exp(cums)
        e_rest = jnp.concatenate([jnp.exp(cums[CHUNK - 1:CHUNK] - cums[:CHUNK]),
                                  jnp.exp(cums[CHUNK:CHUNK + 1] - cums[CHUNK:])], axis=0)
        return gact, cums, e_cum, e_rest

    def front(g, slot):
        ms, ss, cs = group_chunks(g)
        r0s = [pl.multiple_of(c * CHUNK, CHUNK) for c in cs]
        gact, cums, e_cum, e_rest = gates(ss, r0s)
        yield
        qk, kk, decay, bdecay = [], [], [], []
        for n, (t, hh) in enumerate(units):
            col = lambda x, which: _col_bcast(x, lane, GATE_COLS * t + which * DN_HEADS + head0 + hh)
            beta_f, beta_b = col(gact, 0), col(gact, 1)
            gc_col = jnp.where(is_f, col(cums[:CHUNK], 2), col(cums[CHUNK:], 3))
            e_f, e_b = col(e_cum[:CHUNK], 2), col(e_cum[CHUNK:], 3)
            q = l2n(conv_silu(q_ref, cwq_ref, hh, ss[t], cs[t], r0s[t], n)) * (HEAD_W ** -0.5)
            k = l2n(conv_silu(k_ref, cwk_ref, hh, ss[t], cs[t], r0s[t], nu + n))
            v = conv_silu(v_ref, cwv_ref, hh, ss[t], cs[t], r0s[t], 2 * nu + n)
            rhs_s[slot, n] = jnp.concatenate([
                jnp.concatenate([v * beta_f, k * beta_f * e_f], axis=1),
                jnp.concatenate([v * beta_b, k * beta_b * e_b], axis=1)], axis=0).astype(BF16)
            wq_s[hh, 0, ms[t], CHUNK:, :] = (q * e_f).astype(BF16)
            wq_s[hh, 1, ms[t], CHUNK:, :] = (q * e_b).astype(BF16)
            kdec = jnp.concatenate([k * col(e_rest[:CHUNK], 2), k * col(e_rest[CHUNK:], 3)], axis=0)
            kdt_s[hh, ms[t]] = kdec.T.astype(BF16)
            dl_s[hh, 0, ms[t]] = jnp.broadcast_to(e_f[CHUNK - 1:CHUNK, :], (8, 128))
            dl_s[hh, 1, ms[t]] = jnp.broadcast_to(e_b[0:1, :], (8, 128))
            kb = k.astype(BF16)
            qk.append(jnp.concatenate([q.astype(BF16), kb], axis=0))
            kk.append(jnp.concatenate([kb, kb], axis=0))
            gc_row = jnp.sum(jnp.where(diag, gc_col, 0.0), axis=0, keepdims=True)
            decay.append(jnp.where(incl, jnp.exp(jnp.where(incl, gc_col - gc_row, 0.0)), 0.0))
            bdecay.append(jnp.where(strict, jnp.where(is_f, beta_f, beta_b) * decay[-1], 0.0))
            yield
        gram = yield from _each_y(
            lambda a, b: lax.dot_general(a, b, (((1,), (1,)), ((), ())), preferred_element_type=F32),
            qk, kk, every=2)
        for n, (t, hh) in enumerate(units):
            lbd_s[slot, n] = (bdecay[n] * gram[n][CHUNK:]).astype(BF16)
            at_s[hh, ms[t]] = (gram[n][:CHUNK] * decay[n]).astype(BF16)
            if n % 2 == 1:
                yield

    def back(g, slot):
        ms = [g * gc + t for t in range(gc)]
        l2 = [lbd_s[slot, n] for n in range(nu)]
        zero = jnp.zeros((CHUNK, 128), BF16)

        def bd(x):
            x = x.astype(BF16)
            return jnp.concatenate([jnp.where(is_f, x, zero), jnp.where(is_f, zero, x)], axis=0)

        pdot = lambda a, b_bd: jnp.dot(a.astype(BF16), b_bd, preferred_element_type=F32)
        inv = _each(lambda x: eye_p - jnp.where(couple[0], x, zero).astype(F32), l2)
        for lvl in range(1, len(couple)):
            c_bd = _each(lambda x: bd(jnp.where(couple[lvl], x, zero)), l2)
            xc = yield from _each_y(lambda a, b: pdot(a, b).astype(BF16), inv, c_bd)
            inv = yield from _each_y(lambda x, y: x - pdot(y, bd(x)), inv, xc)
        uw = yield from _each_y(lambda tt, n: jnp.dot(bd(tt), rhs_s[slot, n], preferred_element_type=F32),
                                inv, range(nu), every=2)
        for n, (t, hh) in enumerate(units):
            u_s[hh, 0, ms[t]] = uw[n][:CHUNK, :HEAD_W]
            u_s[hh, 1, ms[t]] = uw[n][CHUNK:, :HEAD_W]
            wq_s[hh, 0, ms[t], :CHUNK, :] = uw[n][:CHUNK, HEAD_W:].astype(BF16)
            wq_s[hh, 1, ms[t], :CHUNK, :] = uw[n][CHUNK:, HEAD_W:].astype(BF16)

    chains = [(s, hh) for s in range(ns) for hh in range(hp)]

    dnn = dnn_ref[...]

    def emit_out(c):
        r0 = c * CHUNK if isinstance(c, int) else pl.multiple_of(c * CHUNK, CHUNK)
        for s, hh in chains:
            m = s * n_ch + c
            vst = jnp.concatenate([vn_s[hh, 0, m], vn_s[hh, 1, m]], axis=0)
            o = qs_s[hh, m] + jnp.dot(at_s[hh, m], vst, preferred_element_type=F32)
            y = o * lax.rsqrt(jnp.mean(o * o, axis=-1, keepdims=True) + EPS) * dnn
            o_ref[s, pl.ds(r0, CHUNK), cols(hh)] = (
                y * _silu(z_ref[s, pl.ds(r0, CHUNK), cols(hh)])).astype(o_ref.dtype)

    def scan_body(first, with_out, i, carry):
        if with_out:
            emit_out(i - 1)
            emit_out(n_ch - i)
        todo = [(s, hh, d, s * n_ch + (i if d == 0 else n_ch - 1 - i)) for s, hh in chains for d in range(2)]
        st = [s_s[s * hp + hh, d] for s, hh, d, m in todo]
        r = [jnp.dot(wq_s[hh, d, m], x.astype(BF16), preferred_element_type=F32)
             for (s, hh, d, m), x in zip(todo, st)]
        vn = []
        for (s, hh, d, m), rr in zip(todo, r):
            vn.append((u_s[hh, d, m] - rr[:CHUNK]).astype(BF16))
            vn_s[hh, d, m] = vn[-1]
        upd = []
        for (s, hh, d, m), x in zip(todo, vn):
            kdt = kdt_s[hh, m]
            zero = jnp.zeros_like(kdt)
            kdt = jnp.where(lane_f128, kdt, zero) if d == 0 else jnp.where(lane_f128, zero, kdt)
            upd.append(jnp.dot(kdt, jnp.concatenate([x, x], axis=0), preferred_element_type=F32))
        for (s, hh, d, m), x, rr, up in zip(todo, st, r, upd):
            s_s[s * hp + hh, d] = x * dl_s[hh, d, m][0:1, :] + up
            qs_s[hh, m] = rr[CHUNK:] if first else qs_s[hh, m] + rr[CHUNK:]
        return carry

    half = n_ch // 2
    early = ns == 1 and n_grp >= 3 and gc <= half
    order = (lambda i: (i + n_grp - 1) % n_grp) if early else (lambda i: i)
    _interleave(front(order(0), 0))

    def prep_body(i, carry):
        slot = i & 1
        _interleave(back(order(i), slot), front(order(i + 1), 1 - slot))
        return carry

    lax.fori_loop(0, n_grp - 1, prep_body, 0)
    for s, hh in chains:
        for d in range(2):
            s_s[s * hp + hh, d] = s0_ref[s, d, hh] if has_s0 else jnp.zeros((HEAD_W, HEAD_W), F32)

    def early_scan():
        for i in range(gc):
            scan_body(True, False, i, 0)
            yield

    last = back(order(n_grp - 1), (n_grp - 1) & 1)
    _interleave(last, early_scan()) if early else _interleave(last)
    n_early = gc if early else 0

    lax.fori_loop(n_early, half, functools.partial(scan_body, True, False), 0)
    scan_body(False, False, half, 0)
    lax.fori_loop(half + 1, n_ch, functools.partial(scan_body, False, True), 0)
    emit_out(n_ch - 1)
    emit_out(0)
    if out_state:
        for s, hh in chains:
            for d in range(2):
                st_ref[s, d, hh] = s_s[s * hp + hh, d]


def _dn_call(proj3, conv_w, alog_row, dtb_row, dn_norm, state_delta, layer, depth, out_state, states_so_far):
    n_seq, seq_len, _ = proj3.shape
    n_ch = seq_len // CHUNK
    hp = DN_HEADS_PER_STEP
    ns = max(1, min(n_seq, DN_CHUNKS_PER_STEP // n_ch))
    assert n_seq % ns == 0
    tot = ns * n_ch
    nu = min(PREP_UNROLL, tot) * hp
    wid = hp * HEAD_W
    nblk = DN_HEADS // hp
    has_s0 = state_delta is not None
    tile = lambda off: pl.BlockSpec((ns, seq_len, wid), lambda b, h: (b, 0, off + h))
    cw = lambda off: pl.BlockSpec((CONV_K, wid), lambda b, h: (0, off + h))
    row = pl.BlockSpec((1, HEAD_W), lambda b, h: (0, 0))
    in_specs = [tile(0), tile(nblk), tile(2 * nblk), tile(3 * nblk),
                pl.BlockSpec((ns, seq_len, HEAD_W), lambda b, h: (b, 0, GATE_BLK)),
                cw(0), cw(nblk), cw(2 * nblk), row, row, row]
    args = [proj3, proj3, proj3, proj3, proj3, conv_w, conv_w, conv_w, alog_row, dtb_row,
            dn_norm.reshape(1, HEAD_W)]
    if has_s0:
        in_specs.append(pl.BlockSpec((ns, None, 2, hp, HEAD_W, HEAD_W),
                                     lambda b, h: (b, layer, 0, h, 0, 0)))
        args.append(state_delta)
    out_shape = [jax.ShapeDtypeStruct((n_seq, seq_len, DN_W), BF16)]
    out_specs = [pl.BlockSpec((ns, seq_len, wid), lambda b, h: (b, 0, h))]
    aliases = {}
    if out_state:
        out_shape.append(jax.ShapeDtypeStruct((n_seq, depth, 2, DN_HEADS, HEAD_W, HEAD_W), F32))
        out_specs.append(pl.BlockSpec((ns, None, 2, hp, HEAD_W, HEAD_W), lambda b, h: (b, layer, 0, h, 0, 0)))
        if states_so_far is not None:
            in_specs.append(pl.BlockSpec(memory_space=pl.ANY))
            args.append(states_so_far)
            aliases = {len(args) - 1: 1}
    scratch = [
        pltpu.VMEM((hp, 2, tot, 2 * CHUNK, HEAD_W), BF16),
        pltpu.VMEM((hp, 2, tot, CHUNK, HEAD_W), F32),
        pltpu.VMEM((hp, tot, HEAD_W, 2 * CHUNK), BF16),
        pltpu.VMEM((hp, 2, tot, 8, HEAD_W), F32),
        pltpu.VMEM((hp, tot, CHUNK, 2 * CHUNK), BF16),
        pltpu.VMEM((hp, 2, tot, CHUNK, HEAD_W), BF16),
        pltpu.VMEM((hp, tot, CHUNK, HEAD_W), F32),
        pltpu.VMEM((ns * hp, 2, HEAD_W, HEAD_W), F32),
        pltpu.VMEM((3 * nu, CHUNK + 16, HEAD_W), F32),
        pltpu.VMEM((2, nu, CHUNK, 2 * CHUNK), BF16),
        pltpu.VMEM((2, nu, 2 * CHUNK, 2 * HEAD_W), BF16),
    ]
    outs = pl.pallas_call(
        functools.partial(_dn_kernel, seq_len=seq_len, ns=ns, hp=hp, has_s0=has_s0, out_state=out_state,
                          has_acc=bool(aliases)),
        out_shape=out_shape,
        grid=(n_seq // ns, nblk),
        in_specs=in_specs,
        out_specs=out_specs,
        scratch_shapes=scratch,
        input_output_aliases=aliases,
        compiler_params=pltpu.CompilerParams(vmem_limit_bytes=V7X_VMEM_LIMIT),
        name="deltanet",
    )(*args)
    return (outs[0], outs[1]) if out_state else (outs[0], None)


def _split2(x):
    hi = x.astype(BF16)
    lo = (x - hi.astype(F32)).astype(BF16)
    return hi, lo


def _window(pos, win, n):
    lo = jnp.clip(pos - win // 2, 0, n)
    hi = jnp.clip(pos - win // 2 + win, 0, n)
    return lo, hi


def _pool_kernel(u_ref, pw_ref, ps_ref, o_ref, m_s, *, seq_len, rows, ns):
    period = GRID_W if rows else seq_len
    tb = min(POOL_TB, seq_len)
    r_i = _iota((tb, tb), 0)
    c_i = _iota((tb, tb), 1)
    shift = period.bit_length() - 1
    same_line = (r_i >> shift) == (c_i >> shift)
    pos_r = r_i & (period - 1)
    pos_c = c_i & (period - 1)
    pos_col = _iota((tb, HEAD_W), 0) & (period - 1)
    band, inv_cnt = {}, {}
    for g, win in enumerate(POOL_WINDOWS):
        cols = slice(g * HEAD_W, (g + 1) * HEAD_W)
        if rows:
            run = None
            prev_lo = prev_hi = 0
            for r in range(rows):
                lo, hi = max(r - win // 2, 0), min(r - win // 2 + win, rows)
                for a in range(prev_hi, hi):
                    slab = u_ref[0, a * GRID_W:(a + 1) * GRID_W, cols]
                    run = slab if run is None else run + slab
                for a in range(prev_lo, lo):
                    run = run - u_ref[0, a * GRID_W:(a + 1) * GRID_W, cols]
                prev_lo, prev_hi = lo, hi
                m_s[g, r * GRID_W:(r + 1) * GRID_W, :] = run / float(hi - lo)
        lo_r, hi_r = _window(pos_r, win, period)
        band[g] = jnp.where(same_line & (pos_c >= lo_r) & (pos_c < hi_r), 1.0, 0.0).astype(BF16)
        lo_c, hi_c = _window(pos_col, win, period)
        inv_cnt[g] = 1.0 / (hi_c - lo_c).astype(F32)

    items = [(s, g, t0) for s in range(ns) for g in range(POOL_GROUPS) for t0 in range(0, seq_len, tb)]
    gcols = lambda g: slice(g * HEAD_W, (g + 1) * HEAD_W)
    for b0 in range(0, len(items), POOL_BATCH):
        batch = items[b0:b0 + POOL_BATCH]
        ug = [u_ref[s, t0:t0 + tb, gcols(g)] for s, g, t0 in batch]
        src = [m_s[g, t0:t0 + tb, :] for _, g, t0 in batch] if rows else ug
        parts = _each(_split2, src)
        box = [jnp.dot(band[g], hi, preferred_element_type=F32) + jnp.dot(band[g], lo, preferred_element_type=F32)
               for (_, g, _), (hi, lo) in zip(batch, parts)]
        dev = [(bx * inv_cnt[g] - u).astype(BF16) for (_, g, _), bx, u in zip(batch, box, ug)]
        mix = [jnp.dot(d, pw_ref[g], preferred_element_type=F32) for (_, g, _), d in zip(batch, dev)]
        for (s, g, t0), y in zip(batch, mix):
            o_ref[s, t0:t0 + tb, gcols(g)] = (y * ps_ref[:, gcols(g)]).astype(o_ref.dtype)


def _pool_call(proj3, pool_w_bf, pool_scale, layer, rows):
    n_seq, seq_len, _ = proj3.shape
    ns = 1 if rows else max(1, min(n_seq, POOL_TOKENS_PER_STEP // seq_len))
    assert n_seq % ns == 0
    return pl.pallas_call(
        functools.partial(_pool_kernel, seq_len=seq_len, rows=rows, ns=ns),
        out_shape=jax.ShapeDtypeStruct((n_seq, seq_len, POOL_W), BF16),
        grid=(n_seq // ns,),
        in_specs=[
            pl.BlockSpec((ns, seq_len, POOL_W), lambda b: (b, 0, POOL_BLK512)),
            pl.BlockSpec((None, POOL_GROUPS, HEAD_W, HEAD_W), lambda b: (layer, 0, 0, 0)),
            pl.BlockSpec((1, POOL_W), lambda b: (0, 0)),
        ],
        out_specs=pl.BlockSpec((ns, seq_len, POOL_W), lambda b: (b, 0, 0)),
        scratch_shapes=[pltpu.VMEM((POOL_GROUPS, seq_len, HEAD_W), F32)],
        compiler_params=pltpu.CompilerParams(vmem_limit_bytes=V7X_VMEM_LIMIT),
        name="pool",
    )(proj3, pool_w_bf, pool_scale.reshape(1, POOL_W))


def _post_kernel(x_ref, o_ref, p_ref, ada_ref, nf_ref, fin_ref, wo_ref, wgu_ref, wd_ref, y_ref, *, final):
    mix = (jnp.dot(o_ref[...], wo_ref[:DN_W, :], preferred_element_type=F32)
           + jnp.dot(p_ref[...], wo_ref[DN_W:, :], preferred_element_type=F32))
    x = x_ref[...] + ada_ref[2:3, :] * mix
    h = _modulated_norm(x, nf_ref[...], ada_ref[3:4, :], ada_ref[4:5, :])
    gu = jnp.dot(h.astype(BF16), wgu_ref[...], preferred_element_type=F32)
    act = _silu(gu[:, :D_FF]) * gu[:, D_FF:]
    x = x + ada_ref[5:6, :] * jnp.dot(act.astype(BF16), wd_ref[...], preferred_element_type=F32)
    if final:
        x = x * lax.rsqrt(jnp.mean(x * x, axis=-1, keepdims=True) + EPS) * fin_ref[...]
    y_ref[...] = x


def _post_call(x2, o2, p2, ada_l, norm_ffn, final_norm, w_out_bf, w_gu_bf, w_down_bf, layer, seq_len, per_seq,
               final):
    n_tok = x2.shape[0]
    tm = _token_tile(TM_POST, n_tok, seq_len, per_seq)
    const = lambda shape: pl.BlockSpec((None,) + shape, lambda i: (layer, 0, 0), pipeline_mode=pl.Buffered(1))
    return pl.pallas_call(
        functools.partial(_post_kernel, final=final),
        out_shape=jax.ShapeDtypeStruct((n_tok, D_MODEL), F32),
        grid=(n_tok // tm,),
        in_specs=[
            pl.BlockSpec((tm, D_MODEL), lambda i: (i, 0)),
            pl.BlockSpec((tm, DN_W), lambda i: (i, 0)),
            pl.BlockSpec((tm, POOL_W), lambda i: (i, 0)),
            pl.BlockSpec((None, 6, D_MODEL), _ada_row_map(seq_len, tm, per_seq)),
            pl.BlockSpec((1, D_MODEL), lambda i: (0, 0)),
            pl.BlockSpec((1, D_MODEL), lambda i: (0, 0)),
            const((D_MODEL, D_MODEL)),
            const((D_MODEL, 2 * D_FF)),
            const((D_FF, D_MODEL)),
        ],
        out_specs=pl.BlockSpec((tm, D_MODEL), lambda i: (i, 0)),
        compiler_params=pltpu.CompilerParams(vmem_limit_bytes=V7X_VMEM_LIMIT),
        name="post",
    )(x2, o2, p2, ada_l, norm_ffn.reshape(1, D_MODEL), final_norm.reshape(1, D_MODEL),
      w_out_bf, w_gu_bf, w_down_bf)


def _gate_row(p):
    one = jnp.concatenate([jnp.zeros((2 * DN_HEADS,), F32), p.reshape(-1)])
    return jnp.tile(one, HEAD_W // GATE_COLS).reshape(1, HEAD_W)


def _stream_layer(x2, seq_len, rows, per_seq, ada_l, lw, state_delta, states_so_far, layer, depth, final):
    n_seq = x2.shape[0] // seq_len
    proj = _inproj_call(x2, ada_l, lw["norm_mix"], lw["w_in"], layer, seq_len, per_seq)
    proj3 = proj.reshape(n_seq, seq_len, PROJ_W)
    o, st = _dn_call(proj3, lw["conv_w"], lw["alog_row"], lw["dtb_row"], lw["dn_norm"],
                     state_delta, layer, depth, state_delta is None, states_so_far)
    p = _pool_call(proj3, lw["pool_w"], lw["pool_scale"], layer, rows)
    x2 = _post_call(x2, o.reshape(-1, DN_W), p.reshape(-1, POOL_W), ada_l, lw["norm_ffn"], lw["final_norm"],
                    lw["w_out"], lw["w_gu"], lw["w_down"], layer, seq_len, per_seq, final)
    return x2, st


def kernel(x_prompt, x_sample, c, state_delta, c_ctx, w_ada, b_ada, norm_mix, norm_ffn, w_in, conv_w, a_log, dt_bias, dn_norm, pool_w, pool_scale, w_out, w_gu, w_down, final_norm):
    depth = w_ada.shape[0]
    _, ctx_len, _ = x_prompt.shape
    n_lat, lat_len, _ = x_sample.shape
    assert 1 + n_lat <= ADA_ROWS
    c_all = jnp.concatenate([c_ctx[None, :], c, jnp.zeros((ADA_ROWS - 1 - n_lat, D_MODEL), F32)], axis=0)
    ada = _ada_call(c_all, w_ada, b_ada).reshape(depth, ADA_ROWS, 6, D_MODEL)

    qkvz = 4 * DN_W
    n_gate = 4 * DN_HEADS
    xp = x_prompt.reshape(-1, D_MODEL)
    xs = x_sample.reshape(-1, D_MODEL)
    new_state = None
    w_in_bf = w_in.astype(BF16)
    w_in_p = jnp.concatenate(
        [w_in_bf[:, :, :qkvz], w_in_bf[:, :, qkvz + n_gate:], w_in_bf[:, :, qkvz:qkvz + n_gate],
         jnp.zeros((depth, D_MODEL, HEAD_W - n_gate), BF16)], axis=2)
    pool_w_bf, w_out_bf, w_gu_bf, w_down_bf = (w.astype(BF16) for w in (pool_w, w_out, w_gu, w_down))
    for l in range(depth):
        lw = dict(norm_mix=norm_mix[l], norm_ffn=norm_ffn[l], w_in=w_in_p, conv_w=conv_w[l],
                  alog_row=_gate_row(a_log[l]), dtb_row=_gate_row(dt_bias[l]), dn_norm=dn_norm[l],
                  pool_w=pool_w_bf, pool_scale=pool_scale[l], w_out=w_out_bf,
                  w_gu=w_gu_bf, w_down=w_down_bf, final_norm=final_norm)
        final = l == depth - 1
        xp, new_state = _stream_layer(xp, ctx_len, None, False, ada[l], lw, None, new_state, l, depth, final)
        xs, _ = _stream_layer(xs, lat_len, lat_len // GRID_W, True, ada[l], lw, state_delta, None, l, depth, final)
    return (xp.reshape(x_prompt.shape), xs.reshape(x_sample.shape), new_state)
```
